```python
import math
import jax, jax.numpy as jnp
from jax import lax
import numpy as np

D_MODEL = 1024
BATCH = 4
SEQ = 8192
DEPTH = 4
DEC_BATCH = 16
DEC_SEQ = 64
PAST_LEN = 2048

CHUNK = 64
W_MIX = 512
N_BRANCH = 3
POOL_WINDOWS = (2, 4, 8, 16)
POOL_GROUPS = len(POOL_WINDOWS)
POOL_GROUP_W = W_MIX // POOL_GROUPS
POOL_HIST = max(POOL_WINDOWS) - 1
SSM_GROUP_CH = 16
SSM_GROUPS = W_MIX // SSM_GROUP_CH
SSM_STATE = 64
DT_MIN = 1e-3
DT_MAX = 1e-1
FOX_HEADS = 8
FOX_HEAD_DIM = W_MIX // FOX_HEADS
Q_BLOCK = 128
FORGET_BIAS_INIT = 2.0
NEG_INF = -1e30
D_FF = 2816
CONV_W = 3
EPS = 1e-6

COL_POOL = 0
COL_SSM = W_MIX
COL_Q = 2 * W_MIX
COL_K = 3 * W_MIX
COL_V = 4 * W_MIX
COL_F = 5 * W_MIX
COL_GATE = 5 * W_MIX + FOX_HEADS
D_IN = COL_GATE + N_BRANCH * D_MODEL

kernel_name = "hybrid_pool_s5_fox_streaming_step"


def rmsnorm(x, g):
    x32 = x.astype(jnp.float32)
    y = x32 * lax.rsqrt(jnp.mean(x32 * x32, axis=-1, keepdims=True) + EPS)
    return (y * g.astype(jnp.float32)).astype(x.dtype)


def pool_mixer(u, hist, pos, w_pool, pool_scale):
    bn, L, _ = u.shape
    full = jnp.concatenate([hist.astype(u.dtype), u], axis=1)
    cs = jnp.concatenate([jnp.zeros((bn, 1, W_MIX), jnp.float32),
                          jnp.cumsum(full.astype(jnp.float32), axis=1)], axis=1)
    end = cs[:, POOL_HIST + 1:]
    means = []
    for g, w in enumerate(POOL_WINDOWS):
        lo, hi = g * POOL_GROUP_W, (g + 1) * POOL_GROUP_W
        start = cs[:, POOL_HIST + 1 - w: POOL_HIST + 1 - w + L, lo:hi]
        cnt = jnp.minimum(pos + 1, w).astype(jnp.float32)[None, :, None]
        means.append((end[:, :, lo:hi] - start) / cnt)
    d = jnp.concatenate(means, axis=-1) - u.astype(jnp.float32)
    y = jnp.einsum('blgc,gcd->blgd', d.reshape(bn, L, POOL_GROUPS, POOL_GROUP_W),
                   w_pool.astype(jnp.float32))
    y = y.reshape(bn, L, W_MIX) * pool_scale.astype(jnp.float32)
    return y.astype(u.dtype), full[:, -POOL_HIST:]


def ssm_mixer(u, s0_re, s0_im, lam_re, lam_im, log_dt, b_re, b_im, c_re, c_im, d_skip, w_glu, b_glu):
    bn, L, _ = u.shape
    f32 = jnp.float32
    lam = lax.complex(lam_re.astype(f32), lam_im.astype(f32))
    dt = jnp.exp(log_dt.astype(f32))[:, None]
    lam_bar = jnp.exp(lam * dt)
    b_mat = lax.complex(b_re.astype(f32), b_im.astype(f32))
    b_bar = ((lam_bar - 1.0) / lam)[..., None] * b_mat
    c_mat = lax.complex(c_re.astype(f32), c_im.astype(f32))
    u32 = u.astype(f32)
    ug = u32.reshape(bn, L, SSM_GROUPS, SSM_GROUP_CH)
    bu = jnp.einsum('gpc,blgc->blgp', b_bar, ug)
    s0 = lax.complex(s0_re.astype(f32), s0_im.astype(f32))
    bu = bu.at[:, 0].add(lam_bar * s0)
    a = jnp.broadcast_to(lam_bar, bu.shape)

    def combine(e1, e2):
        a1, b1 = e1
        a2, b2 = e2
        return a1 * a2, a2 * b1 + b2

    _, states = lax.associative_scan(combine, (a, bu), axis=1)
    y = jnp.einsum('gcp,blgp->blgc', c_mat, states).real.reshape(bn, L, W_MIX)
    y = y + d_skip.astype(f32) * u32
    y = jax.nn.gelu(y)
    y = y * jax.nn.sigmoid(y @ w_glu.astype(f32) + b_glu.astype(f32))
    s_last = states[:, -1]
    return y.astype(u.dtype), jnp.real(s_last), jnp.imag(s_last)


def fox_attention(q, k, v, c_q, c_k, q_pos):
    bn, lq = q.shape[:2]
    lk = k.shape[1]
    k_pos = jnp.arange(lk, dtype=jnp.int32)
    c_kT = jnp.transpose(c_k, (0, 2, 1))
    scale = FOX_HEAD_DIM ** -0.5

    def attend(blk):
        qb, cqb, pb = blk
        s = jnp.einsum('bqhd,bkhd->bhqk', qb, k).astype(jnp.float32) * scale
        s = s + jnp.transpose(cqb, (0, 2, 1))[..., None] - c_kT[:, :, None, :]
        s = jnp.where((k_pos[None, :] <= pb[:, None])[None, None], s, NEG_INF)
        p = jax.nn.softmax(s, axis=-1)
        return jnp.einsum('bhqk,bkhd->bqhd', p.astype(v.dtype), v)

    if lq > Q_BLOCK and lq % Q_BLOCK == 0:
        nb = lq // Q_BLOCK
        qb = jnp.moveaxis(q.reshape(bn, nb, Q_BLOCK, FOX_HEADS, FOX_HEAD_DIM), 1, 0)
        cqb = jnp.moveaxis(c_q.reshape(bn, nb, Q_BLOCK, FOX_HEADS), 1, 0)
        pb = q_pos.reshape(nb, Q_BLOCK)
        o = lax.map(attend, (qb, cqb, pb))
        return jnp.moveaxis(o, 0, 1).reshape(bn, lq, FOX_HEADS, FOX_HEAD_DIM)
    return attend((q, c_q, q_pos))


def conv_ffn(h, hist, w_up, conv_w, conv_b, w_down):
    L = h.shape[1]
    up = h @ w_up
    full = jnp.concatenate([hist.astype(up.dtype), up], axis=1)
    conv = conv_b + full[:, 0:L] * conv_w[0]
    for j in range(1, CONV_W):
        conv = conv + full[:, j:j + L] * conv_w[j]
    gate, val = jnp.split(conv, 2, axis=-1)
    y = (jax.nn.silu(gate) * val) @ w_down
    return y, full[:, -(CONV_W - 1):]


def run_layer(x, pos, pool_hist, s_re, s_im, k_past, v_past, logf_past, conv_hist,
              norm_mix_g, w_in, b_f, w_pool, pool_scale, lam_re, lam_im, log_dt,
              b_re, b_im, c_re, c_im, d_skip, w_glu, b_glu, w_branch, w_out,
              norm_ffn_g, w_up, conv_w, conv_b, w_down):
    bn, L, _ = x.shape
    h = rmsnorm(x, norm_mix_g)
    z = h @ w_in
    u_pool = z[..., COL_POOL:COL_SSM]
    u_ssm = z[..., COL_SSM:COL_Q]
    q = z[..., COL_Q:COL_K].reshape(bn, L, FOX_HEADS, FOX_HEAD_DIM)
    k = z[..., COL_K:COL_V].reshape(bn, L, FOX_HEADS, FOX_HEAD_DIM)
    v = z[..., COL_V:COL_F].reshape(bn, L, FOX_HEADS, FOX_HEAD_DIM)
    logf = jax.nn.log_sigmoid(z[..., COL_F:COL_GATE].astype(jnp.float32)
                              + b_f.astype(jnp.float32))
    gates = jax.nn.sigmoid(z[..., COL_GATE:].reshape(bn, L, N_BRANCH, D_MODEL))

    y_a, pool_new = pool_mixer(u_pool, pool_hist, pos, w_pool, pool_scale)
    y_b, s_re_new, s_im_new = ssm_mixer(u_ssm, s_re, s_im, lam_re, lam_im, log_dt,
                                        b_re, b_im, c_re, c_im, d_skip, w_glu, b_glu)
    if k_past is None:
        k_all, v_all, logf_all = k, v, logf
    else:
        k_all = jnp.concatenate([k_past.astype(k.dtype), k], axis=1)
        v_all = jnp.concatenate([v_past.astype(v.dtype), v], axis=1)
        logf_all = jnp.concatenate([logf_past.astype(jnp.float32), logf], axis=1)
    c_all = jnp.cumsum(logf_all, axis=1)
    y_c = fox_attention(q, k_all, v_all, c_all[:, -L:], c_all, pos).reshape(bn, L, W_MIX)

    branches = jnp.stack([y_a, y_b, y_c.astype(y_a.dtype)], axis=2)
    proj = jnp.einsum('blnw,nwd->blnd', branches, w_branch)
    merged = jnp.sum(gates.astype(proj.dtype) * proj, axis=2)
    x = x + (merged @ w_out).astype(x.dtype)

    ff, conv_new = conv_ffn(rmsnorm(x, norm_ffn_g), conv_hist, w_up, conv_w, conv_b, w_down)
    x = x + ff.astype(x.dtype)
    return x, (pool_new, s_re_new, s_im_new, k, v, logf, conv_new)


def _stack(states, i):
    return jnp.stack([s[i] for s in states], axis=0)


def setup_inputs(seed: int = 0) -> dict:
    key = jax.random.key(seed)
    ks = iter(jax.random.split(key, 40))
    f32 = jnp.float32

    def nrm(shape, scale):
        return scale * jax.random.normal(next(ks), shape, f32)

    lam_im_base = jnp.broadcast_to(jnp.pi * jnp.arange(SSM_STATE, dtype=f32),
                                   (DEPTH, SSM_GROUPS, SSM_STATE))
    return {
        "x_prompt": nrm((BATCH, SEQ, D_MODEL), 1.0),
        "x_sample": nrm((DEC_BATCH, DEC_SEQ, D_MODEL), 1.0),
        "cache_pool": nrm((DEPTH, DEC_BATCH, POOL_HIST, W_MIX), 1.0),
        "state_ssm_re": nrm((DEPTH, DEC_BATCH, SSM_GROUPS, SSM_STATE), 0.5),
        "state_ssm_im": nrm((DEPTH, DEC_BATCH, SSM_GROUPS, SSM_STATE), 0.5),
        "cache_k": nrm((DEPTH, DEC_BATCH, PAST_LEN, FOX_HEADS, FOX_HEAD_DIM), 1.0),
        "cache_v": nrm((DEPTH, DEC_BATCH, PAST_LEN, FOX_HEADS, FOX_HEAD_DIM), 1.0),
        "cache_logf": jax.nn.log_sigmoid(nrm((DEPTH, DEC_BATCH, PAST_LEN, FOX_HEADS), 1.0)
                                         + FORGET_BIAS_INIT),
        "cache_ffn_conv": nrm((DEPTH, DEC_BATCH, CONV_W - 1, 2 * D_FF), 1.0),
        "norm_mix_g": 1.0 + nrm((DEPTH, D_MODEL), 0.02),
        "w_in": nrm((DEPTH, D_MODEL, D_IN), D_MODEL ** -0.5),
        "b_f": FORGET_BIAS_INIT + nrm((DEPTH, FOX_HEADS), 0.1),
        "w_pool": nrm((DEPTH, POOL_GROUPS, POOL_GROUP_W, POOL_GROUP_W), POOL_GROUP_W ** -0.5),
        "pool_scale": 1.0 + nrm((DEPTH, W_MIX), 0.02),
        "lam_re": -0.5 + nrm((DEPTH, SSM_GROUPS, SSM_STATE), 0.01),
        "lam_im": lam_im_base + nrm((DEPTH, SSM_GROUPS, SSM_STATE), 0.01),
        "log_dt": jax.random.uniform(next(ks), (DEPTH, SSM_GROUPS), f32,
                                     minval=math.log(DT_MIN), maxval=math.log(DT_MAX)),
        "b_re": nrm((DEPTH, SSM_GROUPS, SSM_STATE, SSM_GROUP_CH), (2 * SSM_GROUP_CH) ** -0.5),
        "b_im": nrm((DEPTH, SSM_GROUPS, SSM_STATE, SSM_GROUP_CH), (2 * SSM_GROUP_CH) ** -0.5),
        "c_re": nrm((DEPTH, SSM_GROUPS, SSM_GROUP_CH, SSM_STATE), (2 * SSM_STATE) ** -0.5),
        "c_im": nrm((DEPTH, SSM_GROUPS, SSM_GROUP_CH, SSM_STATE), (2 * SSM_STATE) ** -0.5),
        "d_skip": nrm((DEPTH, W_MIX), 0.5),
        "w_glu": nrm((DEPTH, W_MIX, W_MIX), W_MIX ** -0.5),
        "b_glu": nrm((DEPTH, W_MIX), 0.01),
        "w_branch": nrm((DEPTH, N_BRANCH, W_MIX, D_MODEL), W_MIX ** -0.5),
        "w_out": nrm((DEPTH, D_MODEL, D_MODEL), D_MODEL ** -0.5),
        "norm_ffn_g": 1.0 + nrm((DEPTH, D_MODEL), 0.02),
        "w_up": nrm((DEPTH, D_MODEL, 2 * D_FF), D_MODEL ** -0.5),
        "conv_w": nrm((DEPTH, CONV_W, 2 * D_FF), CONV_W ** -0.5),
        "conv_b": nrm((DEPTH, 2 * D_FF), 0.01),
        "w_down": nrm((DEPTH, D_FF, D_MODEL), D_FF ** -0.5),
        "norm_final_g": 1.0 + nrm((D_MODEL,), 0.02),
    }


def reference(x_prompt, x_sample, cache_pool, state_ssm_re, state_ssm_im, cache_k, cache_v,
              cache_logf, cache_ffn_conv, norm_mix_g, w_in, b_f, w_pool, pool_scale, lam_re,
              lam_im, log_dt, b_re, b_im, c_re, c_im, d_skip, w_glu, b_glu, w_branch, w_out,
              norm_ffn_g, w_up, conv_w, conv_b, w_down, norm_final_g):
    bp, lp, _ = x_prompt.shape
    ls = x_sample.shape[1]
    past = cache_k.shape[2]
    pos_p = jnp.arange(lp, dtype=jnp.int32)
    pos_s = past + jnp.arange(ls, dtype=jnp.int32)
    zero_pool = jnp.zeros((bp, POOL_HIST, W_MIX), x_prompt.dtype)
    zero_s = jnp.zeros((bp, SSM_GROUPS, SSM_STATE), jnp.float32)
    zero_conv = jnp.zeros((bp, CONV_W - 1, 2 * D_FF), x_prompt.dtype)

    hp, hs = x_prompt, x_sample
    st_p, st_s = [], []
    for l in range(DEPTH):
        layer_params = (norm_mix_g[l], w_in[l], b_f[l], w_pool[l], pool_scale[l], lam_re[l],
                        lam_im[l], log_dt[l], b_re[l], b_im[l], c_re[l], c_im[l], d_skip[l],
                        w_glu[l], b_glu[l], w_branch[l], w_out[l], norm_ffn_g[l], w_up[l],
                        conv_w[l], conv_b[l], w_down[l])
        hp, sp = run_layer(hp, pos_p, zero_pool, zero_s, zero_s, None, None, None, zero_conv,
                           *layer_params)
        hs, ss = run_layer(hs, pos_s, cache_pool[l], state_ssm_re[l], state_ssm_im[l],
                           cache_k[l], cache_v[l], cache_logf[l], cache_ffn_conv[l],
                           *layer_params)
        st_p.append(sp)
        st_s.append(ss)

    y_prompt = rmsnorm(hp, norm_final_g)
    y_sample = rmsnorm(hs, norm_final_g)
    return (y_prompt, y_sample,
            _stack(st_p, 0), _stack(st_s, 0),
            _stack(st_p, 1), _stack(st_s, 1),
            _stack(st_p, 2), _stack(st_s, 2),
            _stack(st_p, 3), _stack(st_s, 3),
            _stack(st_p, 4), _stack(st_s, 4),
            _stack(st_p, 5), _stack(st_s, 5),
            _stack(st_p, 6), _stack(st_s, 6))
```

```python
import functools
import math

import jax
import jax.numpy as jnp
from jax import lax
from jax.experimental import pallas as pl
from jax.experimental.pallas import tpu as pltpu

F32 = jnp.float32
BF16 = jnp.bfloat16
HIGHEST = lax.Precision.HIGHEST

D_MODEL = 1024
W_MIX = 512
POOL_WINDOWS = (2, 4, 8, 16)
POOL_GROUP_W = 128
POOL_HIST = 15
SSM_GROUP_CH = 16
SSM_GROUPS = 32
SSM_STATE = 64
SSM_CHUNK = 16
SSM_CHUNK_W = SSM_CHUNK * SSM_GROUP_CH
FOX_HEADS = 8
FOX_HEAD_DIM = 64
HEAD_PAIRS = FOX_HEADS // 2
D_FF = 2816
EPS = 1e-6
NEG_INF = -1e30
VMEM_LIMIT = 56 * 1024 * 1024


def _params(*sem):
    return pltpu.CompilerParams(dimension_semantics=sem, vmem_limit_bytes=VMEM_LIMIT)


def _tile(n, target, mult=8):
    if n <= target:
        return n
    for t in range(target, 0, -1):
        if n % t == 0 and t % mult == 0:
            return t
    return n


def _rms_bf16(x, g):
    ms = jnp.mean(x * x, axis=-1, keepdims=True)
    return (x * lax.rsqrt(ms + EPS) * g).astype(BF16)


def _sigmoid(x):
    return 1.0 / (1.0 + jnp.exp(-x))


def _inproj_kernel(x_ref, g_ref, w_ref, wf_ref, bf_ref,
                   up_ref, us_ref, q_ref, k_ref, v_ref, lf_ref, h_scr):
    j = pl.program_id(1)

    @pl.when(j == 0)
    def _():
        hb = _rms_bf16(x_ref[...], g_ref[...])
        h_scr[...] = hb
        zf = jnp.dot(hb, wf_ref[...], preferred_element_type=F32)
        a = zf[:, :FOX_HEADS] + bf_ref[...]
        lf_ref[...] = jnp.minimum(a, 0.0) - jnp.log(1.0 + jnp.exp(-jnp.abs(a)))

    acc = jnp.dot(h_scr[...], w_ref[...], preferred_element_type=F32)
    for idx, ref in enumerate((up_ref, us_ref, q_ref, k_ref, v_ref)):
        @pl.when(j == idx)
        def _(ref=ref):
            ref[...] = acc


def _inproj(x2, g, w_main, w_f, b_f, tm):
    n = x2.shape[0]
    row = lambda i, j: (i, 0)
    out = jax.ShapeDtypeStruct((n, W_MIX), F32)
    return pl.pallas_call(
        _inproj_kernel,
        grid=(n // tm, 5),
        in_specs=[
            pl.BlockSpec((tm, D_MODEL), row),
            pl.BlockSpec((1, D_MODEL), lambda i, j: (0, 0)),
            pl.BlockSpec((D_MODEL, W_MIX), lambda i, j: (0, j)),
            pl.BlockSpec((D_MODEL, 128), lambda i, j: (0, 0)),
            pl.BlockSpec((1, FOX_HEADS), lambda i, j: (0, 0)),
        ],
        out_specs=[pl.BlockSpec((tm, W_MIX), row)] * 5 + [pl.BlockSpec((tm, FOX_HEADS), row)],
        out_shape=[out] * 5 + [jax.ShapeDtypeStruct((n, FOX_HEADS), F32)],
        scratch_shapes=[pltpu.VMEM((tm, D_MODEL), BF16)],
        compiler_params=_params("parallel", "arbitrary"),
    )(x2, g, w_main, w_f, b_f)


def _pool_kernel(u_ref, hist_ref, w_ref, sc_ref, y_ref, pn_ref, buf, *, tm, pos0):
    t = pl.program_id(1)

    @pl.when(t == 0)
    def _():
        buf[0:1, :] = jnp.zeros((1, W_MIX), F32)
        buf[1:16, :] = hist_ref[0]

    @pl.when(t > 0)
    def _():
        buf[0:16, :] = buf[tm:tm + 16, :]

    buf[16:16 + tm, :] = u_ref[0]
    pos = pos0 + t * tm + lax.broadcasted_iota(jnp.int32, (tm, 1), 0)
    for g, w in enumerate(POOL_WINDOWS):
        lo, hi = g * POOL_GROUP_W, (g + 1) * POOL_GROUP_W
        full = buf[:, lo:hi]
        s = full
        k = 1
        while k < w:
            s = s + pltpu.roll(s, k, axis=0)
            k *= 2
        cnt = jnp.minimum(pos + 1, w).astype(F32)
        d = s[16:] * (1.0 / cnt) - full[16:]
        y = jnp.dot(d.astype(BF16), w_ref[g], preferred_element_type=F32) * sc_ref[:, lo:hi]
        y_ref[0, :, lo:hi] = y.astype(BF16)
    pn_ref[0] = buf[tm + 1:tm + 16, :]


def _pool(u, hist, w_pool, scale, pos0, tm):
    nb, L, _ = u.shape
    return pl.pallas_call(
        functools.partial(_pool_kernel, tm=tm, pos0=pos0),
        grid=(nb, L // tm),
        in_specs=[
            pl.BlockSpec((1, tm, W_MIX), lambda b, t: (b, t, 0)),
            pl.BlockSpec((1, POOL_HIST, W_MIX), lambda b, t: (b, 0, 0)),
            pl.BlockSpec((4, POOL_GROUP_W, POOL_GROUP_W), lambda b, t: (0, 0, 0)),
            pl.BlockSpec((1, W_MIX), lambda b, t: (0, 0)),
        ],
        out_specs=[
            pl.BlockSpec((1, tm, W_MIX), lambda b, t: (b, t, 0)),
            pl.BlockSpec((1, POOL_HIST, W_MIX), lambda b, t: (b, 0, 0)),
        ],
        out_shape=[jax.ShapeDtypeStruct((nb, L, W_MIX), BF16),
                   jax.ShapeDtypeStruct((nb, POOL_HIST, W_MIX), F32)],
        scratch_shapes=[pltpu.VMEM((tm + 16, W_MIX), F32)],
        compiler_params=_params("parallel", "arbitrary"),
    )(u, hist, w_pool, scale)


def _ssm_operators(lam_re, lam_im, log_dt, b_re, b_im, c_re, c_im):
    T = SSM_CHUNK
    dt = jnp.exp(log_dt)[:, None]
    ar, ai = lam_re * dt, lam_im * dt
    taus = jnp.arange(T + 1, dtype=F32)[:, None, None]
    mag = jnp.exp(taus * ar)
    pr, pi = mag * jnp.cos(taus * ai), mag * jnp.sin(taus * ai)
    lbr, lbi = pr[1], pi[1]
    den = lam_re * lam_re + lam_im * lam_im
    fr = ((lbr - 1.0) * lam_re + lbi * lam_im) / den
    fi = (lbi * lam_re - (lbr - 1.0) * lam_im) / den
    bbr = fr[..., None] * b_re - fi[..., None] * b_im
    bbi = fr[..., None] * b_im + fi[..., None] * b_re
    er = pr[:T, :, :, None] * bbr - pi[:T, :, :, None] * bbi
    ei = pr[:T, :, :, None] * bbi + pi[:T, :, :, None] * bbr
    kt = (jnp.einsum('gcp,tgpd->tgcd', c_re, er, precision=HIGHEST)
          - jnp.einsum('gcp,tgpd->tgcd', c_im, ei, precision=HIGHEST))
    jj = jnp.arange(T)[:, None]
    tt = jnp.arange(T)[None, :]
    lag = tt - jj
    toe = jnp.where((lag >= 0)[:, :, None, None, None], kt[jnp.clip(lag, 0, T - 1)], 0.0)
    m1 = jnp.transpose(toe, (2, 0, 4, 1, 3)).reshape(SSM_GROUPS, SSM_CHUNK_W, SSM_CHUNK_W)
    w1r = jnp.transpose(er[::-1], (1, 0, 3, 2)).reshape(SSM_GROUPS, SSM_CHUNK_W, SSM_STATE)
    w1i = jnp.transpose(ei[::-1], (1, 0, 3, 2)).reshape(SSM_GROUPS, SSM_CHUNK_W, SSM_STATE)
    mw = jnp.concatenate([m1, w1r, w1i], axis=-1)
    qr, qi = pr[1:], pi[1:]
    m2r = c_re[None] * qr[:, :, None, :] - c_im[None] * qi[:, :, None, :]
    m2i = -(c_re[None] * qi[:, :, None, :] + c_im[None] * qr[:, :, None, :])
    m2 = jnp.concatenate([jnp.transpose(m2r, (1, 3, 0, 2)), jnp.transpose(m2i, (1, 3, 0, 2))],
                         axis=1).reshape(SSM_GROUPS, 2 * SSM_STATE, SSM_CHUNK_W)
    a_vec = jnp.concatenate([pr[T], pr[T]], axis=-1)
    b_vec = jnp.concatenate([-pi[T], pi[T]], axis=-1)
    return mw, m2, a_vec, b_vec


def _ssm_intra_kernel(u_ref, mw_ref, yi_ref, v_ref):
    r = jnp.dot(u_ref[0, 0], mw_ref[0], precision=HIGHEST, preferred_element_type=F32)
    yi_ref[0, 0] = r[:, :SSM_CHUNK_W]
    v_ref[0, 0] = r[:, SSM_CHUNK_W:]


def _ssm_intra(uc, mw):
    nb, G, nc, _ = uc.shape
    blk = lambda w: pl.BlockSpec((1, 1, nc, w), lambda b, g: (b, g, 0, 0))
    return pl.pallas_call(
        _ssm_intra_kernel,
        grid=(nb, G),
        in_specs=[blk(SSM_CHUNK_W),
                  pl.BlockSpec((1, SSM_CHUNK_W, SSM_CHUNK_W + 2 * SSM_STATE), lambda b, g: (g, 0, 0))],
        out_specs=[blk(SSM_CHUNK_W), blk(2 * SSM_STATE)],
        out_shape=[jax.ShapeDtypeStruct((nb, G, nc, SSM_CHUNK_W), F32),
                   jax.ShapeDtypeStruct((nb, G, nc, 2 * SSM_STATE), F32)],
        compiler_params=_params("parallel", "parallel"),
    )(uc, mw)


def _ssm_scan_kernel(v_ref, s0_ref, a_ref, b_ref, sin_ref, sl_ref, *, nc):
    a = a_ref[...]
    bv = b_ref[...]

    def body(c, s):
        sin_ref[0, c] = s
        return a * s + bv * pltpu.roll(s, SSM_STATE, axis=1) + v_ref[0, c]

    sl_ref[0] = lax.fori_loop(0, nc, body, s0_ref[0])


def _ssm_scan(vt, s0, a_vec, b_vec):
    nb, nc, G, _ = vt.shape
    return pl.pallas_call(
        functools.partial(_ssm_scan_kernel, nc=nc),
        grid=(nb,),
        in_specs=[pl.BlockSpec((1, nc, G, 128), lambda b: (b, 0, 0, 0)),
                  pl.BlockSpec((1, G, 128), lambda b: (b, 0, 0)),
                  pl.BlockSpec((G, 128), lambda b: (0, 0)),
                  pl.BlockSpec((G, 128), lambda b: (0, 0))],
        out_specs=[pl.BlockSpec((1, nc, G, 128), lambda b: (b, 0, 0, 0)),
                   pl.BlockSpec((1, G, 128), lambda b: (b, 0, 0))],
        out_shape=[jax.ShapeDtypeStruct((nb, nc, G, 128), F32),
                   jax.ShapeDtypeStruct((nb, G, 128), F32)],
        compiler_params=_params("parallel"),
    )(vt, s0, a_vec, b_vec)


def _ssm_inter_kernel(s_ref, m2_ref, yi_ref, y_ref):
    y_ref[0, 0] = yi_ref[0, 0] + jnp.dot(s_ref[0, 0], m2_ref[0], precision=HIGHEST,
                                         preferred_element_type=F32)


def _ssm_inter(sg, m2, yi):
    nb, G, nc, _ = sg.shape
    blk = lambda w: pl.BlockSpec((1, 1, nc, w), lambda b, g: (b, g, 0, 0))
    return pl.pallas_call(
        _ssm_inter_kernel,
        grid=(nb, G),
        in_specs=[blk(2 * SSM_STATE),
                  pl.BlockSpec((1, 2 * SSM_STATE, SSM_CHUNK_W), lambda b, g: (g, 0, 0)),
                  blk(SSM_CHUNK_W)],
        out_specs=blk(SSM_CHUNK_W),
        out_shape=jax.ShapeDtypeStruct((nb, G, nc, SSM_CHUNK_W), F32),
        compiler_params=_params("parallel", "parallel"),
    )(sg, m2, yi)


def _glu_kernel(y_ref, u_ref, d_ref, w_ref, b_ref, o_ref):
    y = y_ref[...] + d_ref[...] * u_ref[...]
    y = 0.5 * y * (1.0 + jnp.tanh(math.sqrt(2.0 / math.pi) * (y + 0.044715 * (y * y * y))))
    z = jnp.dot(y.astype(BF16), w_ref[...], preferred_element_type=F32) + b_ref[...]
    o_ref[...] = (y * _sigmoid(z)).astype(BF16)


def _glu(y2, u2, d_skip, w_glu, b_glu, tm):
    n = y2.shape[0]
    row = pl.BlockSpec((tm, W_MIX), lambda i: (i, 0))
    vec = pl.BlockSpec((1, W_MIX), lambda i: (0, 0))
    return pl.pallas_call(
        _glu_kernel,
        grid=(n // tm,),
        in_specs=[row, row, vec, pl.BlockSpec((W_MIX, W_MIX), lambda i: (0, 0)), vec],
        out_specs=row,
        out_shape=jax.ShapeDtypeStruct((n, W_MIX), BF16),
        compiler_params=_params("parallel"),
    )(y2, u2, d_skip, w_glu, b_glu)


def _cumsum_kernel(lf_ref, c_ref, carry, *, tc):
    @pl.when(pl.program_id(1) == 0)
    def _():
        carry[...] = jnp.zeros_like(carry)

    r = lax.broadcasted_iota(jnp.int32, (tc, tc), 0)
    c = lax.broadcasted_iota(jnp.int32, (tc, tc), 1)
    tri = (r >= c).astype(F32)
    cs = jnp.dot(tri, lf_ref[0], precision=HIGHEST, preferred_element_type=F32) + carry[...]
    c_ref[0] = cs
    carry[...] = cs[tc - 1:tc, :]


def _cumsum(lf, tc):
    nb, lk, _ = lf.shape
    blk = pl.BlockSpec((1, tc, FOX_HEADS), lambda b, t: (b, t, 0))
    return pl.pallas_call(
        functools.partial(_cumsum_kernel, tc=tc),
        grid=(nb, lk // tc),
        in_specs=[blk],
        out_specs=blk,
        out_shape=jax.ShapeDtypeStruct((nb, lk, FOX_HEADS), F32),
        scratch_shapes=[pltpu.VMEM((1, FOX_HEADS), F32)],
        compiler_params=_params("parallel", "arbitrary"),
    )(lf)


def _attn_kernel(q_ref, k_ref, v_ref, cq_ref, ck_ref, o_ref, m_scr, l_scr, acc_scr,
                 *, tq, tk, q_off, nk):
    qi = pl.program_id(2)
    kj = pl.program_id(3)

    @pl.when(kj == 0)
    def _():
        m_scr[...] = jnp.full_like(m_scr, NEG_INF)
        l_scr[...] = jnp.zeros_like(l_scr)
        acc_scr[...] = jnp.zeros_like(acc_scr)

    q_lo = q_off + qi * tq
    last = (q_lo + tq - 1) // tk

    @pl.when(kj <= last)
    def _():
        q = q_ref[0] * (FOX_HEAD_DIM ** -0.5)
        kb = k_ref[0].astype(BF16)
        vb = v_ref[0].astype(BF16)
        lane = lax.broadcasted_iota(jnp.int32, (1, 2 * FOX_HEAD_DIM), 1)
        qpos = q_lo + lax.broadcasted_iota(jnp.int32, (tq, 1), 0)
        kpos = kj * tk + lax.broadcasted_iota(jnp.int32, (1, tk), 1)
        mask = kpos <= qpos
        cq = cq_ref[0, 0]
        ck = ck_ref[0, 0]
        for hh in range(2):
            in_head = (lane >= hh * FOX_HEAD_DIM) & (lane < (hh + 1) * FOX_HEAD_DIM)
            qm = jnp.where(in_head, q, 0.0).astype(BF16)
            s = lax.dot_general(qm, kb, (((1,), (1,)), ((), ())), preferred_element_type=F32)
            s = s + cq[:, hh:hh + 1] - ck[hh:hh + 1, :]
            s = jnp.where(mask, s, NEG_INF)
            m_prev = m_scr[hh]
            m_new = jnp.maximum(m_prev, jnp.max(s, axis=1, keepdims=True))
            alpha = jnp.exp(m_prev - m_new)
            p = jnp.exp(s - m_new)
            l_scr[hh] = alpha * l_scr[hh] + jnp.sum(p, axis=1, keepdims=True)
            acc_scr[hh] = alpha * acc_scr[hh] + jnp.dot(p.astype(BF16), vb,
                                                        preferred_element_type=F32)
            m_scr[hh] = m_new

    @pl.when(kj == nk - 1)
    def _():
        lane = lax.broadcasted_iota(jnp.int32, (1, 2 * FOX_HEAD_DIM), 1)
        o = jnp.where(lane < FOX_HEAD_DIM, acc_scr[0] * (1.0 / l_scr[0]), acc_scr[1] * (1.0 / l_scr[1]))
        o_ref[0] = o.astype(BF16)


def _attention(q, k_all, v_all, cq, ck, q_off, tq, tk):
    nb, lq, _ = q.shape
    lk = k_all.shape[1]
    nq, nk = lq // tq, lk // tk

    def kidx(b, hp, qi, kj):
        return jnp.minimum(kj, (q_off + qi * tq + tq - 1) // tk)

    return pl.pallas_call(
        functools.partial(_attn_kernel, tq=tq, tk=tk, q_off=q_off, nk=nk),
        grid=(nb, HEAD_PAIRS, nq, nk),
        in_specs=[
            pl.BlockSpec((1, tq, 128), lambda b, hp, qi, kj: (b, qi, hp)),
            pl.BlockSpec((1, tk, 128), lambda b, hp, qi, kj: (b, kidx(b, hp, qi, kj), hp)),
            pl.BlockSpec((1, tk, 128), lambda b, hp, qi, kj: (b, kidx(b, hp, qi, kj), hp)),
            pl.BlockSpec((1, 1, tq, 2), lambda b, hp, qi, kj: (b, hp, qi, 0)),
            pl.BlockSpec((1, 1, 2, tk), lambda b, hp, qi, kj: (b, hp, 0, kidx(b, hp, qi, kj))),
        ],
        out_specs=pl.BlockSpec((1, tq, 128), lambda b, hp, qi, kj: (b, qi, hp)),
        out_shape=jax.ShapeDtypeStruct((nb, lq, W_MIX), BF16),
        scratch_shapes=[pltpu.VMEM((2, tq, 1), F32), pltpu.VMEM((2, tq, 1), F32),
                        pltpu.VMEM((2, tq, 128), F32)],
        compiler_params=_params("parallel", "parallel", "parallel", "arbitrary"),
    )(q, k_all, v_all, cq, ck)


def _merge_kernel(x_ref, g_ref, ya_ref, yb_ref, yc_ref, wg_ref, wb_ref, wo_ref, o_ref):
    x = x_ref[...]
    hb = _rms_bf16(x, g_ref[...])
    merged = None
    for b, y_ref in enumerate((ya_ref, yb_ref, yc_ref)):
        gate = _sigmoid(jnp.dot(hb, wg_ref[:, b * D_MODEL:(b + 1) * D_MODEL],
                                preferred_element_type=F32))
        term = gate * jnp.dot(y_ref[...], wb_ref[b], preferred_element_type=F32)
        merged = term if merged is None else merged + term
    o_ref[...] = x + jnp.dot(merged.astype(BF16), wo_ref[...], preferred_element_type=F32)


def _merge(x2, g, ya, yb, yc, w_gate, w_branch, w_out, tm):
    n = x2.shape[0]
    row = lambda w: pl.BlockSpec((tm, w), lambda i: (i, 0))
    const = lambda shape: pl.BlockSpec(shape, lambda i: (0,) * len(shape))
    return pl.pallas_call(
        _merge_kernel,
        grid=(n // tm,),
        in_specs=[row(D_MODEL), const((1, D_MODEL)), row(W_MIX), row(W_MIX), row(W_MIX),
                  const((D_MODEL, 3 * D_MODEL)), const((3, W_MIX, D_MODEL)),
                  const((D_MODEL, D_MODEL))],
        out_specs=row(D_MODEL),
        out_shape=jax.ShapeDtypeStruct((n, D_MODEL), F32),
        compiler_params=_params("parallel"),
    )(x2, g, ya, yb, yc, w_gate, w_branch, w_out)


FFN_HALO = 16


def _ffn_kernel(x_ref, xh_ref, g_ref, hist_ref, wug_ref, wuv_ref, cwg_ref, cwv_ref,
                cbg_ref, cbv_ref, wd_ref, o_ref, cn_ref, h_scr, hh_scr, acc_scr, up_scr,
                *, tm, nj):
    t = pl.program_id(1)
    j = pl.program_id(2)

    @pl.when(j == 0)
    def _():
        h_scr[...] = _rms_bf16(x_ref[...], g_ref[...])
        hh_scr[...] = _rms_bf16(xh_ref[...], g_ref[...])
        acc_scr[...] = jnp.zeros_like(acc_scr)

    conv = []
    for hf, (wu_ref, cw_ref, cb_ref) in enumerate(((wug_ref, cwg_ref, cbg_ref),
                                                   (wuv_ref, cwv_ref, cbv_ref))):
        w = wu_ref[...]
        up = jnp.dot(h_scr[...], w, preferred_element_type=F32)
        uph = jnp.dot(hh_scr[...], w, preferred_element_type=F32)
        up_scr[hf, 0:8, :] = uph[FFN_HALO - 8:, :]

        @pl.when(t == 0)
        def _(hf=hf):
            up_scr[hf, 6:8, :] = hist_ref[0, hf]

        up_scr[hf, 8:8 + tm, :] = up
        cn_ref[0, hf, j] = up[tm - 2:, :]
        cw = cw_ref[...]
        conv.append(cb_ref[...] + up_scr[hf, 6:6 + tm, :] * cw[0:1, :]
                    + up_scr[hf, 7:7 + tm, :] * cw[1:2, :] + up * cw[2:3, :])
    gate, val = conv
    act = (gate * _sigmoid(gate) * val).astype(BF16)
    acc_scr[...] += jnp.dot(act, wd_ref[...], preferred_element_type=F32)

    @pl.when(j == nj - 1)
    def _():
        o_ref[...] = x_ref[...] + acc_scr[...]


def _ffn(x2, g, hist, wug, wuv, cwg, cwv, cbg, cbv, wd, nb, L, tm, tn):
    n = x2.shape[0]
    nt, nj = L // tm, D_FF // tn
    hpt = tm // FFN_HALO
    row = lambda b, t, j: (b * nt + t, 0)
    col = lambda b, t, j: (0, j)
    return pl.pallas_call(
        functools.partial(_ffn_kernel, tm=tm, nj=nj),
        grid=(nb, nt, nj),
        in_specs=[
            pl.BlockSpec((tm, D_MODEL), row),
            pl.BlockSpec((FFN_HALO, D_MODEL), lambda b, t, j: (jnp.maximum((b * nt + t) * hpt - 1, 0), 0)),
            pl.BlockSpec((1, D_MODEL), lambda b, t, j: (0, 0)),
            pl.BlockSpec((1, 2, 2, tn), lambda b, t, j: (b, 0, 0, j)),
            pl.BlockSpec((D_MODEL, tn), col),
            pl.BlockSpec((D_MODEL, tn), col),
            pl.BlockSpec((3, tn), col),
            pl.BlockSpec((3, tn), col),
            pl.BlockSpec((1, tn), col),
            pl.BlockSpec((1, tn), col),
            pl.BlockSpec((tn, D_MODEL), lambda b, t, j: (j, 0)),
        ],
        out_specs=[pl.BlockSpec((tm, D_MODEL), row),
                   pl.BlockSpec((1, 2, nj, 2, tn), lambda b, t, j: (b, 0, 0, 0, 0))],
        out_shape=[jax.ShapeDtypeStruct((n, D_MODEL), F32),
                   jax.ShapeDtypeStruct((nb, 2, nj, 2, tn), F32)],
        scratch_shapes=[pltpu.VMEM((tm, D_MODEL), BF16), pltpu.VMEM((FFN_HALO, D_MODEL), BF16),
                        pltpu.VMEM((tm, D_MODEL), F32), pltpu.VMEM((2, tm + 8, tn), F32)],
        compiler_params=_params("arbitrary", "arbitrary", "arbitrary"),
    )(x2, x2, g, hist, wug, wuv, cwg, cwv, cbg, cbv, wd)


def _final_norm_kernel(x_ref, g_ref, o_ref):
    x = x_ref[...]
    ms = jnp.mean(x * x, axis=-1, keepdims=True)
    o_ref[...] = x * lax.rsqrt(ms + EPS) * g_ref[...]


def _final_norm(x2, g, tm):
    n = x2.shape[0]
    row = pl.BlockSpec((tm, D_MODEL), lambda i: (i, 0))
    return pl.pallas_call(
        _final_norm_kernel,
        grid=(n // tm,),
        in_specs=[row, pl.BlockSpec((1, D_MODEL), lambda i: (0, 0))],
        out_specs=row,
        out_shape=jax.ShapeDtypeStruct((n, D_MODEL), F32),
        compiler_params=_params("parallel"),
    )(x2, g)


def _layer(x2, nb, L, pos0, pool_hist, s0, k_past, v_past, logf_past, conv_hist, p):
    n = nb * L
    tm = _tile(L, 512)
    u_pool, u_ssm, q, k, v, logf = _inproj(x2, p["g_mix"], p["w_main"], p["w_f"], p["b_f"], tm)

    y_a, pool_new = _pool(u_pool.reshape(nb, L, W_MIX), pool_hist, p["w_pool"], p["pool_scale"],
                          pos0, tm)

    nc = L // SSM_CHUNK
    uc = jnp.transpose(u_ssm.reshape(nb, nc, SSM_CHUNK, SSM_GROUPS, SSM_GROUP_CH),
                       (0, 3, 1, 2, 4)).reshape(nb, SSM_GROUPS, nc, SSM_CHUNK_W)
    yi, vg = _ssm_intra(uc, p["mw"])
    s_in, s_last = _ssm_scan(jnp.transpose(vg, (0, 2, 1, 3)), s0, p["a_vec"], p["b_vec"])
    yg = _ssm_inter(jnp.transpose(s_in, (0, 2, 1, 3)), p["m2"], yi)
    y_ssm = jnp.transpose(yg.reshape(nb, SSM_GROUPS, nc, SSM_CHUNK, SSM_GROUP_CH),
                          (0, 2, 3, 1, 4)).reshape(n, W_MIX)
    y_b = _glu(y_ssm, u_ssm, p["d_skip"], p["w_glu"], p["b_glu"], tm)

    q3, k3, v3 = (a.reshape(nb, L, W_MIX) for a in (q, k, v))
    logf3 = logf.reshape(nb, L, FOX_HEADS)
    if k_past is None:
        k_all, v_all, logf_all = k3, v3, logf3
    else:
        k_all = jnp.concatenate([k_past, k3], axis=1)
        v_all = jnp.concatenate([v_past, v3], axis=1)
        logf_all = jnp.concatenate([logf_past, logf3], axis=1)
    lk = k_all.shape[1]
    c_all = _cumsum(logf_all, _tile(lk, 512))
    cq = jnp.transpose(c_all[:, lk - L:].reshape(nb, L, HEAD_PAIRS, 2), (0, 2, 1, 3))
    ck = jnp.transpose(c_all.reshape(nb, lk, HEAD_PAIRS, 2), (0, 2, 3, 1))
    tk = 512 if lk % 512 == 0 else lk
    y_c = _attention(q3, k_all, v_all, cq, ck, pos0, tm, tk)

    x2 = _merge(x2, p["g_mix"], y_a.reshape(n, W_MIX), y_b, y_c.reshape(n, W_MIX),
                p["w_gate"], p["w_branch"], p["w_out"], tm)
    x2, conv_new = _ffn(x2, p["g_ffn"], conv_hist, p["wug"], p["wuv"], p["cwg"], p["cwv"],
                        p["cbg"], p["cbv"], p["w_down"], nb, L, tm, 256)
    states = (pool_new,
              s_last[:, :, :SSM_STATE], s_last[:, :, SSM_STATE:],
              k3.reshape(nb, L, FOX_HEADS, FOX_HEAD_DIM), v3.reshape(nb, L, FOX_HEADS, FOX_HEAD_DIM),
              logf3,
              jnp.transpose(conv_new, (0, 3, 1, 2, 4)).reshape(nb, 2, 2 * D_FF))
    return x2, states


def kernel(x_prompt, x_sample, cache_pool, state_ssm_re, state_ssm_im, cache_k, cache_v, cache_logf, cache_ffn_conv, norm_mix_g, w_in, b_f, w_pool, pool_scale, lam_re, lam_im, log_dt, b_re, b_im, c_re, c_im, d_skip, w_glu, b_glu, w_branch, w_out, norm_ffn_g, w_up, conv_w, conv_b, w_down, norm_final_g):
    bp, lp, _ = x_prompt.shape
    bs, ls, _ = x_sample.shape
    depth = w_in.shape[0]
    past = cache_k.shape[2]
    col_f = 5 * W_MIX
    col_gate = col_f + FOX_HEADS

    hp = x_prompt.reshape(bp * lp, D_MODEL)
    hs = x_sample.reshape(bs * ls, D_MODEL)
    zero_pool = jnp.zeros((bp, POOL_HIST, W_MIX), F32)
    zero_s = jnp.zeros((bp, SSM_GROUPS, 2 * SSM_STATE), F32)
    zero_conv = jnp.zeros((bp, 2, 2, D_FF), F32)

    st_p, st_s = [], []
    for l in range(depth):
        mw, m2, a_vec, b_vec = _ssm_operators(lam_re[l], lam_im[l], log_dt[l], b_re[l], b_im[l],
                                              c_re[l], c_im[l])
        p = dict(
            g_mix=norm_mix_g[l][None], g_ffn=norm_ffn_g[l][None],
            w_main=w_in[l][:, :col_f].astype(BF16),
            w_f=jnp.pad(w_in[l][:, col_f:col_gate], ((0, 0), (0, 128 - FOX_HEADS))).astype(BF16),
            b_f=b_f[l][None],
            w_gate=w_in[l][:, col_gate:].astype(BF16),
            w_pool=w_pool[l].astype(BF16), pool_scale=pool_scale[l][None],
            mw=mw, m2=m2, a_vec=a_vec, b_vec=b_vec,
            d_skip=d_skip[l][None], w_glu=w_glu[l].astype(BF16), b_glu=b_glu[l][None],
            w_branch=w_branch[l].astype(BF16), w_out=w_out[l].astype(BF16),
            wug=w_up[l][:, :D_FF].astype(BF16), wuv=w_up[l][:, D_FF:].astype(BF16),
            cwg=conv_w[l][:, :D_FF], cwv=conv_w[l][:, D_FF:],
            cbg=conv_b[l][None, :D_FF], cbv=conv_b[l][None, D_FF:],
            w_down=w_down[l].astype(BF16),
        )
        hp, sp = _layer(hp, bp, lp, 0, zero_pool, zero_s, None, None, None, zero_conv, p)
        s0 = jnp.concatenate([state_ssm_re[l], state_ssm_im[l]], axis=-1)
        conv_hist = jnp.transpose(cache_ffn_conv[l].reshape(bs, 2, 2, D_FF), (0, 2, 1, 3))
        hs, ss = _layer(hs, bs, ls, past, cache_pool[l], s0,
                        cache_k[l].reshape(bs, past, W_MIX), cache_v[l].reshape(bs, past, W_MIX),
                        cache_logf[l], conv_hist, p)
        st_p.append(sp)
        st_s.append(ss)

    y_prompt = _final_norm(hp, norm_final_g[None], _tile(bp * lp, 512)).reshape(bp, lp, D_MODEL)
    y_sample = _final_norm(hs, norm_final_g[None], _tile(bs * ls, 512)).reshape(bs, ls, D_MODEL)
    outs = [y_prompt, y_sample]
    for i in range(7):
        outs.append(jnp.stack([s[i] for s in st_p], axis=0))
        outs.append(jnp.stack([s[i] for s in st_s], axis=0))
    return tuple(outs)
```

```python
import functools
import math

import jax
import jax.numpy as jnp
from jax import lax
from jax.experimental import pallas as pl
from jax.experimental.pallas import tpu as pltpu

F32 = jnp.float32
BF16 = jnp.bfloat16
HIGHEST = lax.Precision.HIGHEST

D_MODEL = 1024
W_MIX = 512
POOL_WINDOWS = (2, 4, 8, 16)
POOL_GROUP_W = 128
POOL_HIST = 15
SSM_GROUP_CH = 16
SSM_GROUPS = 32
SSM_STATE = 64
SSM_CHUNK = 16
SSM_CHUNK_W = SSM_CHUNK * SSM_GROUP_CH
FOX_HEADS = 8
FOX_HEAD_DIM = 64
HEAD_PAIRS = FOX_HEADS // 2
D_FF = 2816
EPS = 1e-6
NEG_INF = -1e30
VMEM_LIMIT = 56 * 1024 * 1024


def _params(*sem):
    return pltpu.CompilerParams(dimension_semantics=sem, vmem_limit_bytes=VMEM_LIMIT)


def _tile(n, target, mult=8):
    if n <= target:
        return n
    for t in range(target, 0, -1):
        if n % t == 0 and t % mult == 0:
            return t
    return n


def _rms_bf16(x, g):
    ms = jnp.mean(x * x, axis=-1, keepdims=True)
    return (x * lax.rsqrt(ms + EPS) * g).astype(BF16)


def _sigmoid(x):
    return 1.0 / (1.0 + jnp.exp(-x))


def _inproj_kernel(x_ref, g_ref, w_ref, wf_ref, bf_ref,
                   up_ref, us_ref, q_ref, k_ref, v_ref, lf_ref, h_scr):
    j = pl.program_id(1)

    @pl.when(j == 0)
    def _():
        hb = _rms_bf16(x_ref[...], g_ref[...])
        h_scr[...] = hb
        zf = jnp.dot(hb, wf_ref[...], preferred_element_type=F32)
        a = zf[:, :FOX_HEADS] + bf_ref[...]
        lf_ref[...] = jnp.minimum(a, 0.0) - jnp.log(1.0 + jnp.exp(-jnp.abs(a)))

    acc = jnp.dot(h_scr[...], w_ref[...], preferred_element_type=F32)
    for idx, ref in enumerate((up_ref, us_ref, q_ref, k_ref, v_ref)):
        @pl.when(j == idx)
        def _(ref=ref):
            ref[...] = acc


def _inproj(x2, g, w_main, w_f, b_f, tm):
    n = x2.shape[0]
    row = lambda i, j: (i, 0)
    out = jax.ShapeDtypeStruct((n, W_MIX), F32)
    return pl.pallas_call(
        _inproj_kernel,
        grid=(n // tm, 5),
        in_specs=[
            pl.BlockSpec((tm, D_MODEL), row),
            pl.BlockSpec((1, D_MODEL), lambda i, j: (0, 0)),
            pl.BlockSpec((D_MODEL, W_MIX), lambda i, j: (0, j)),
            pl.BlockSpec((D_MODEL, 128), lambda i, j: (0, 0)),
            pl.BlockSpec((1, FOX_HEADS), lambda i, j: (0, 0)),
        ],
        out_specs=[pl.BlockSpec((tm, W_MIX), row)] * 5 + [pl.BlockSpec((tm, FOX_HEADS), row)],
        out_shape=[out] * 5 + [jax.ShapeDtypeStruct((n, FOX_HEADS), F32)],
        scratch_shapes=[pltpu.VMEM((tm, D_MODEL), BF16)],
        compiler_params=_params("parallel", "arbitrary"),
        name="in_proj",
    )(x2, g, w_main, w_f, b_f)


def _pool_kernel(u_ref, hist_ref, w_ref, sc_ref, y_ref, pn_ref, buf, *, tm, pos0):
    t = pl.program_id(1)

    @pl.when(t == 0)
    def _():
        buf[0:1, :] = jnp.zeros((1, W_MIX), F32)
        buf[1:16, :] = hist_ref[0]

    @pl.when(t > 0)
    def _():
        buf[0:16, :] = buf[tm:tm + 16, :]

    buf[16:16 + tm, :] = u_ref[0]
    pos = pos0 + t * tm + lax.broadcasted_iota(jnp.int32, (tm, 1), 0)
    for g, w in enumerate(POOL_WINDOWS):
        lo, hi = g * POOL_GROUP_W, (g + 1) * POOL_GROUP_W
        full = buf[:, lo:hi]
        s = full
        k = 1
        while k < w:
            s = s + pltpu.roll(s, k, axis=0)
            k *= 2
        cnt = jnp.minimum(pos + 1, w).astype(F32)
        d = s[16:] * (1.0 / cnt) - full[16:]
        y = jnp.dot(d.astype(BF16), w_ref[g], preferred_element_type=F32) * sc_ref[:, lo:hi]
        y_ref[0, :, lo:hi] = y.astype(BF16)
    pn_ref[0] = buf[tm + 1:tm + 16, :]


def _pool(u, hist, w_pool, scale, pos0, tm):
    nb, L, _ = u.shape
    return pl.pallas_call(
        functools.partial(_pool_kernel, tm=tm, pos0=pos0),
        grid=(nb, L // tm),
        in_specs=[
            pl.BlockSpec((1, tm, W_MIX), lambda b, t: (b, t, 0)),
            pl.BlockSpec((1, POOL_HIST, W_MIX), lambda b, t: (b, 0, 0)),
            pl.BlockSpec((4, POOL_GROUP_W, POOL_GROUP_W), lambda b, t: (0, 0, 0)),
            pl.BlockSpec((1, W_MIX), lambda b, t: (0, 0)),
        ],
        out_specs=[
            pl.BlockSpec((1, tm, W_MIX), lambda b, t: (b, t, 0)),
            pl.BlockSpec((1, POOL_HIST, W_MIX), lambda b, t: (b, 0, 0)),
        ],
        out_shape=[jax.ShapeDtypeStruct((nb, L, W_MIX), BF16),
                   jax.ShapeDtypeStruct((nb, POOL_HIST, W_MIX), F32)],
        scratch_shapes=[pltpu.VMEM((tm + 16, W_MIX), F32)],
        compiler_params=_params("parallel", "arbitrary"),
        name="pool_mixer",
    )(u, hist, w_pool, scale)


def _ssm_operators(lam_re, lam_im, log_dt, b_re, b_im, c_re, c_im):
    T = SSM_CHUNK
    dt = jnp.exp(log_dt)[:, None]
    ar, ai = lam_re * dt, lam_im * dt
    taus = jnp.arange(T + 1, dtype=F32)[:, None, None]
    mag = jnp.exp(taus * ar)
    pr, pi = mag * jnp.cos(taus * ai), mag * jnp.sin(taus * ai)
    lbr, lbi = pr[1], pi[1]
    den = lam_re * lam_re + lam_im * lam_im
    fr = ((lbr - 1.0) * lam_re + lbi * lam_im) / den
    fi = (lbi * lam_re - (lbr - 1.0) * lam_im) / den
    bbr = fr[..., None] * b_re - fi[..., None] * b_im
    bbi = fr[..., None] * b_im + fi[..., None] * b_re
    er = pr[:T, :, :, None] * bbr - pi[:T, :, :, None] * bbi
    ei = pr[:T, :, :, None] * bbi + pi[:T, :, :, None] * bbr
    kt = (jnp.einsum('gcp,tgpd->tgcd', c_re, er, precision=HIGHEST)
          - jnp.einsum('gcp,tgpd->tgcd', c_im, ei, precision=HIGHEST))
    jj = jnp.arange(T)[:, None]
    tt = jnp.arange(T)[None, :]
    lag = tt - jj
    toe = jnp.where((lag >= 0)[:, :, None, None, None], kt[jnp.clip(lag, 0, T - 1)], 0.0)
    m1 = jnp.transpose(toe, (2, 0, 4, 1, 3)).reshape(SSM_GROUPS, SSM_CHUNK_W, SSM_CHUNK_W)
    w1r = jnp.transpose(er[::-1], (1, 0, 3, 2)).reshape(SSM_GROUPS, SSM_CHUNK_W, SSM_STATE)
    w1i = jnp.transpose(ei[::-1], (1, 0, 3, 2)).reshape(SSM_GROUPS, SSM_CHUNK_W, SSM_STATE)
    mw = jnp.concatenate([m1, w1r, w1i], axis=-1)
    qr, qi = pr[1:], pi[1:]
    m2r = c_re[None] * qr[:, :, None, :] - c_im[None] * qi[:, :, None, :]
    m2i = -(c_re[None] * qi[:, :, None, :] + c_im[None] * qr[:, :, None, :])
    m2 = jnp.concatenate([jnp.transpose(m2r, (1, 3, 0, 2)), jnp.transpose(m2i, (1, 3, 0, 2))],
                         axis=1).reshape(SSM_GROUPS, 2 * SSM_STATE, SSM_CHUNK_W)
    a_vec = jnp.concatenate([pr[T], pr[T]], axis=-1)
    b_vec = jnp.concatenate([-pi[T], pi[T]], axis=-1)
    return mw, m2, a_vec, b_vec


def _ssm_intra_kernel(u_ref, mw_ref, yi_ref, v_ref):
    r = jnp.dot(u_ref[0, 0], mw_ref[0], precision=HIGHEST, preferred_element_type=F32)
    yi_ref[0, 0] = r[:, :SSM_CHUNK_W]
    v_ref[0, 0] = r[:, SSM_CHUNK_W:]


def _ssm_intra(uc, mw):
    nb, G, nc, _ = uc.shape
    blk = lambda w: pl.BlockSpec((1, 1, nc, w), lambda b, g: (b, g, 0, 0))
    return pl.pallas_call(
        _ssm_intra_kernel,
        grid=(nb, G),
        in_specs=[blk(SSM_CHUNK_W),
                  pl.BlockSpec((1, SSM_CHUNK_W, SSM_CHUNK_W + 2 * SSM_STATE), lambda b, g: (g, 0, 0))],
        out_specs=[blk(SSM_CHUNK_W), blk(2 * SSM_STATE)],
        out_shape=[jax.ShapeDtypeStruct((nb, G, nc, SSM_CHUNK_W), F32),
                   jax.ShapeDtypeStruct((nb, G, nc, 2 * SSM_STATE), F32)],
        compiler_params=_params("parallel", "parallel"),
        name="ssm_intra",
    )(uc, mw)


def _ssm_scan_kernel(v_ref, s0_ref, a_ref, b_ref, sin_ref, sl_ref, *, nc):
    a = a_ref[...]
    bv = b_ref[...]

    def body(c, s):
        sin_ref[0, c] = s
        return a * s + bv * pltpu.roll(s, SSM_STATE, axis=1) + v_ref[0, c]

    sl_ref[0] = lax.fori_loop(0, nc, body, s0_ref[0])


def _ssm_scan(vt, s0, a_vec, b_vec):
    nb, nc, G, _ = vt.shape
    return pl.pallas_call(
        functools.partial(_ssm_scan_kernel, nc=nc),
        grid=(nb,),
        in_specs=[pl.BlockSpec((1, nc, G, 128), lambda b: (b, 0, 0, 0)),
                  pl.BlockSpec((1, G, 128), lambda b: (b, 0, 0)),
                  pl.BlockSpec((G, 128), lambda b: (0, 0)),
                  pl.BlockSpec((G, 128), lambda b: (0, 0))],
        out_specs=[pl.BlockSpec((1, nc, G, 128), lambda b: (b, 0, 0, 0)),
                   pl.BlockSpec((1, G, 128), lambda b: (b, 0, 0))],
        out_shape=[jax.ShapeDtypeStruct((nb, nc, G, 128), F32),
                   jax.ShapeDtypeStruct((nb, G, 128), F32)],
        compiler_params=_params("parallel"),
        name="ssm_scan",
    )(vt, s0, a_vec, b_vec)


def _ssm_inter_kernel(s_ref, m2_ref, yi_ref, y_ref):
    y_ref[0, 0] = yi_ref[0, 0] + jnp.dot(s_ref[0, 0], m2_ref[0], precision=HIGHEST,
                                         preferred_element_type=F32)


def _ssm_inter(sg, m2, yi):
    nb, G, nc, _ = sg.shape
    blk = lambda w: pl.BlockSpec((1, 1, nc, w), lambda b, g: (b, g, 0, 0))
    return pl.pallas_call(
        _ssm_inter_kernel,
        grid=(nb, G),
        in_specs=[blk(2 * SSM_STATE),
                  pl.BlockSpec((1, 2 * SSM_STATE, SSM_CHUNK_W), lambda b, g: (g, 0, 0)),
                  blk(SSM_CHUNK_W)],
        out_specs=blk(SSM_CHUNK_W),
        out_shape=jax.ShapeDtypeStruct((nb, G, nc, SSM_CHUNK_W), F32),
        compiler_params=_params("parallel", "parallel"),
        name="ssm_inter",
    )(sg, m2, yi)


def _glu_kernel(y_ref, u_ref, d_ref, w_ref, b_ref, o_ref):
    y = y_ref[...] + d_ref[...] * u_ref[...]
    y = 0.5 * y * (1.0 + jnp.tanh(math.sqrt(2.0 / math.pi) * (y + 0.044715 * (y * y * y))))
    z = jnp.dot(y.astype(BF16), w_ref[...], preferred_element_type=F32) + b_ref[...]
    o_ref[...] = (y * _sigmoid(z)).astype(BF16)


def _glu(y2, u2, d_skip, w_glu, b_glu, tm):
    n = y2.shape[0]
    row = pl.BlockSpec((tm, W_MIX), lambda i: (i, 0))
    vec = pl.BlockSpec((1, W_MIX), lambda i: (0, 0))
    return pl.pallas_call(
        _glu_kernel,
        grid=(n // tm,),
        in_specs=[row, row, vec, pl.BlockSpec((W_MIX, W_MIX), lambda i: (0, 0)), vec],
        out_specs=row,
        out_shape=jax.ShapeDtypeStruct((n, W_MIX), BF16),
        compiler_params=_params("parallel"),
        name="ssm_glu",
    )(y2, u2, d_skip, w_glu, b_glu)


def _cumsum_kernel(lf_ref, c_ref, carry, *, tc):
    @pl.when(pl.program_id(1) == 0)
    def _():
        carry[...] = jnp.zeros_like(carry)

    r = lax.broadcasted_iota(jnp.int32, (tc, tc), 0)
    c = lax.broadcasted_iota(jnp.int32, (tc, tc), 1)
    tri = (r >= c).astype(F32)
    cs = jnp.dot(tri, lf_ref[0], precision=HIGHEST, preferred_element_type=F32) + carry[...]
    c_ref[0] = cs
    carry[...] = cs[tc - 1:tc, :]


def _cumsum(lf, tc):
    nb, lk, _ = lf.shape
    blk = pl.BlockSpec((1, tc, FOX_HEADS), lambda b, t: (b, t, 0))
    return pl.pallas_call(
        functools.partial(_cumsum_kernel, tc=tc),
        grid=(nb, lk // tc),
        in_specs=[blk],
        out_specs=blk,
        out_shape=jax.ShapeDtypeStruct((nb, lk, FOX_HEADS), F32),
        scratch_shapes=[pltpu.VMEM((1, FOX_HEADS), F32)],
        compiler_params=_params("parallel", "arbitrary"),
        name="logf_cumsum",
    )(lf)


ATTN_ROWS = 16
LOG2E = math.log2(math.e)


def _attn_kernel(qi_tab, kj_tab, q_ref, k_ref, v_ref, cq_ref, ck_ref, o_ref,
                 s_scr, p_scr, m_scr, a_scr, cq_scr, acc_scr, *, tq, tk, q_off):
    pidx = pl.program_id(2)
    qi = qi_tab[pidx]
    kj = kj_tab[pidx]
    q_lo = q_off + qi * tq
    k_lo = kj * tk
    cols = [(c, min(c + 128, tk)) for c in range(0, tk, 128)]
    lane = lax.broadcasted_iota(jnp.int32, (1, 2 * FOX_HEAD_DIM), 1)

    @pl.when(kj == 0)
    def _():
        m_scr[...] = jnp.full_like(m_scr, NEG_INF)
        acc_scr[...] = jnp.zeros_like(acc_scr)
        cq = cq_ref[0, 0] * LOG2E
        for hh in range(2):
            cq_scr[hh] = jnp.broadcast_to(cq[:, hh:hh + 1], (tq, 128))

    q = q_ref[0] * (FOX_HEAD_DIM ** -0.5)
    kb = k_ref[0].astype(BF16)
    v = v_ref[0]
    ck = ck_ref[0, 0] * LOG2E

    def softmax_rows(hh, masked):
        ckb = [jnp.broadcast_to(ck[hh:hh + 1, lo:hi], (ATTN_ROWS, hi - lo)) for lo, hi in cols]
        for r in range(0, tq, ATTN_ROWS):
            t = [s_scr[hh, r:r + ATTN_ROWS, lo:hi] * LOG2E - ckb[c]
                 for c, (lo, hi) in enumerate(cols)]
            if masked:
                qpos = q_lo + r + lax.broadcasted_iota(jnp.int32, (ATTN_ROWS, 128), 0)
                kpos = k_lo + lax.broadcasted_iota(jnp.int32, (ATTN_ROWS, 128), 1)
                t = [jnp.where((kpos + lo <= qpos)[:, :hi - lo], t[c], NEG_INF)
                     for c, (lo, hi) in enumerate(cols)]
            full = [t[c] for c, (lo, hi) in enumerate(cols) if hi - lo == 128]
            tmax = full[0]
            for tc in full[1:]:
                tmax = jnp.maximum(tmax, tc)
            tmax = jnp.max(tmax, axis=1, keepdims=True)
            if len(full) < len(cols):
                tmax = jnp.maximum(tmax, jnp.max(t[-1], axis=1, keepdims=True))
            tmax = jnp.broadcast_to(tmax, (ATTN_ROWS, 128))
            cq2 = cq_scr[hh, r:r + ATTN_ROWS, :]
            m_old = m_scr[hh, r:r + ATTN_ROWS, :]
            m_new = jnp.maximum(m_old, tmax + cq2)
            m_scr[hh, r:r + ATTN_ROWS, :] = m_new
            a_scr[hh, r:r + ATTN_ROWS, :] = jnp.exp2(m_old - m_new)
            mt = m_new - cq2
            for c, (lo, hi) in enumerate(cols):
                p_scr[hh, r:r + ATTN_ROWS, lo:hi] = jnp.exp2(t[c] - mt[:, :hi - lo]).astype(BF16)

    def step(masked):
        for hh in range(2):
            in_head = (lane >= hh * FOX_HEAD_DIM) & (lane < (hh + 1) * FOX_HEAD_DIM)
            qm = jnp.where(in_head, q, 0.0).astype(BF16)
            s_scr[hh] = lax.dot_general(qm, kb, (((1,), (1,)), ((), ())),
                                        preferred_element_type=F32)
            softmax_rows(hh, masked)
            vh = jnp.where(in_head, v, 1.0).astype(BF16)
            pv = jnp.dot(p_scr[hh], vh, preferred_element_type=F32)
            acc_scr[hh] = a_scr[hh] * acc_scr[hh] + pv

    needs_mask = k_lo + tk - 1 > q_lo

    @pl.when(needs_mask)
    def _():
        step(True)

    @pl.when(jnp.logical_not(needs_mask))
    def _():
        step(False)

    @pl.when(kj == (q_lo + tq - 1) // tk)
    def _():
        o = []
        for hh in range(2):
            acc = acc_scr[hh]
            o.append(acc * (1.0 / pltpu.roll(acc, FOX_HEAD_DIM, axis=1)))
        o_ref[0] = jnp.where(lane < FOX_HEAD_DIM, o[0], o[1]).astype(BF16)


def _attention(q, k_all, v_all, cq, ck, q_off, tq, tk):
    nb, lq, _ = q.shape
    nq = lq // tq
    pairs = [(qi, kj) for qi in range(nq) for kj in range((q_off + qi * tq + tq - 1) // tk + 1)]
    qi_tab = jnp.asarray([p[0] for p in pairs], jnp.int32)
    kj_tab = jnp.asarray([p[1] for p in pairs], jnp.int32)
    qmap = lambda b, hp, i, qt, kt: (b, qt[i], hp)
    kmap = lambda b, hp, i, qt, kt: (b, kt[i], hp)
    return pl.pallas_call(
        functools.partial(_attn_kernel, tq=tq, tk=tk, q_off=q_off),
        grid_spec=pltpu.PrefetchScalarGridSpec(
            num_scalar_prefetch=2,
            grid=(nb, HEAD_PAIRS, len(pairs)),
            in_specs=[
                pl.BlockSpec((1, tq, 128), qmap),
                pl.BlockSpec((1, tk, 128), kmap),
                pl.BlockSpec((1, tk, 128), kmap),
                pl.BlockSpec((1, 1, tq, 2), lambda b, hp, i, qt, kt: (b, hp, qt[i], 0)),
                pl.BlockSpec((1, 1, 2, tk), lambda b, hp, i, qt, kt: (b, hp, 0, kt[i])),
            ],
            out_specs=pl.BlockSpec((1, tq, 128), qmap),
            scratch_shapes=[pltpu.VMEM((2, tq, tk), F32), pltpu.VMEM((2, tq, tk), BF16),
                            pltpu.VMEM((2, tq, 128), F32), pltpu.VMEM((2, tq, 128), F32),
                            pltpu.VMEM((2, tq, 128), F32), pltpu.VMEM((2, tq, 128), F32)],
        ),
        out_shape=jax.ShapeDtypeStruct((nb, lq, W_MIX), BF16),
        compiler_params=_params("parallel", "parallel", "arbitrary"),
        name="fox_attention",
    )(qi_tab, kj_tab, q, k_all, v_all, cq, ck)


def _merge_kernel(x_ref, g_ref, ya_ref, yb_ref, yc_ref, wg_ref, wb_ref, wo_ref, o_ref):
    x = x_ref[...]
    hb = _rms_bf16(x, g_ref[...])
    merged = None
    for b, y_ref in enumerate((ya_ref, yb_ref, yc_ref)):
        gate = _sigmoid(jnp.dot(hb, wg_ref[:, b * D_MODEL:(b + 1) * D_MODEL],
                                preferred_element_type=F32))
        term = gate * jnp.dot(y_ref[...], wb_ref[b], preferred_element_type=F32)
        merged = term if merged is None else merged + term
    o_ref[...] = x + jnp.dot(merged.astype(BF16), wo_ref[...], preferred_element_type=F32)


def _merge(x2, g, ya, yb, yc, w_gate, w_branch, w_out, tm):
    n = x2.shape[0]
    row = lambda w: pl.BlockSpec((tm, w), lambda i: (i, 0))
    const = lambda shape: pl.BlockSpec(shape, lambda i: (0,) * len(shape))
    return pl.pallas_call(
        _merge_kernel,
        grid=(n // tm,),
        in_specs=[row(D_MODEL), const((1, D_MODEL)), row(W_MIX), row(W_MIX), row(W_MIX),
                  const((D_MODEL, 3 * D_MODEL)), const((3, W_MIX, D_MODEL)),
                  const((D_MODEL, D_MODEL))],
        out_specs=row(D_MODEL),
        out_shape=jax.ShapeDtypeStruct((n, D_MODEL), F32),
        compiler_params=_params("parallel"),
        name="branch_merge",
    )(x2, g, ya, yb, yc, w_gate, w_branch, w_out)


FFN_HALO = 16


FFN_SUB = 256


def _ffn_kernel(x_ref, xh_ref, g_ref, hist_ref, wug_ref, wuv_ref, cwg_ref, cwv_ref,
                cbg_ref, cbv_ref, wd_ref, o_ref, cn_ref, h_scr, hh_scr, acc_scr, act_scr,
                *, tm, tn, nj):
    t = pl.program_id(1)
    j = pl.program_id(2)

    @pl.when(j == 0)
    def _():
        h_scr[...] = _rms_bf16(x_ref[...], g_ref[...])
        hh_scr[...] = _rms_bf16(xh_ref[...], g_ref[...])

    first = t == 0
    for lo in range(0, tn, FFN_SUB):
        hi = min(lo + FFN_SUB, tn)
        conv = []
        for hf, (wu_ref, cw_ref, cb_ref) in enumerate(((wug_ref, cwg_ref, cbg_ref),
                                                       (wuv_ref, cwv_ref, cbv_ref))):
            w = wu_ref[:, lo:hi]
            up = jnp.dot(h_scr[...], w, preferred_element_type=F32)
            uph = jnp.dot(hh_scr[...], w, preferred_element_type=F32)
            prev = jnp.where(first, hist_ref[0, hf, :, lo:hi], uph[FFN_HALO - 8:, :])
            full = jnp.concatenate([prev, up], axis=0)
            cn_ref[0, hf, j, :, lo:hi] = up[tm - 2:, :]
            cw = cw_ref[:, lo:hi]
            conv.append(cb_ref[:, lo:hi] + full[6:6 + tm, :] * cw[0:1, :]
                        + full[7:7 + tm, :] * cw[1:2, :] + up * cw[2:3, :])
        gate, val = conv
        act_scr[:, lo:hi] = (gate * _sigmoid(gate) * val).astype(BF16)
    down = jnp.dot(act_scr[...], wd_ref[...], preferred_element_type=F32)

    if nj == 1:
        o_ref[...] = x_ref[...] + down
        return

    @pl.when(j == 0)
    def _():
        acc_scr[...] = down

    @pl.when((j > 0) & (j < nj - 1))
    def _():
        acc_scr[...] += down

    @pl.when(j == nj - 1)
    def _():
        o_ref[...] = x_ref[...] + acc_scr[...] + down


def _ffn(x2, g, hist, wug, wuv, cwg, cwv, cbg, cbv, wd, nb, L, tm, tn):
    n = x2.shape[0]
    nt, nj = L // tm, D_FF // tn
    hpt = tm // FFN_HALO
    row = lambda b, t, j: (b * nt + t, 0)
    col = lambda b, t, j: (0, j)
    return pl.pallas_call(
        functools.partial(_ffn_kernel, tm=tm, tn=tn, nj=nj),
        grid=(nb, nt, nj),
        in_specs=[
            pl.BlockSpec((tm, D_MODEL), row),
            pl.BlockSpec((FFN_HALO, D_MODEL), lambda b, t, j: (jnp.maximum((b * nt + t) * hpt - 1, 0), 0)),
            pl.BlockSpec((1, D_MODEL), lambda b, t, j: (0, 0)),
            pl.BlockSpec((1, 2, 8, tn), lambda b, t, j: (b, 0, 0, j)),
            pl.BlockSpec((D_MODEL, tn), col),
            pl.BlockSpec((D_MODEL, tn), col),
            pl.BlockSpec((3, tn), col),
            pl.BlockSpec((3, tn), col),
            pl.BlockSpec((1, tn), col),
            pl.BlockSpec((1, tn), col),
            pl.BlockSpec((tn, D_MODEL), lambda b, t, j: (j, 0)),
        ],
        out_specs=[pl.BlockSpec((tm, D_MODEL), row),
                   pl.BlockSpec((1, 2, nj, 2, tn), lambda b, t, j: (b, 0, 0, 0, 0))],
        out_shape=[jax.ShapeDtypeStruct((n, D_MODEL), F32),
                   jax.ShapeDtypeStruct((nb, 2, nj, 2, tn), F32)],
        scratch_shapes=[pltpu.VMEM((tm, D_MODEL), BF16), pltpu.VMEM((FFN_HALO, D_MODEL), BF16),
                        pltpu.VMEM((tm, D_MODEL), F32), pltpu.VMEM((tm, tn), BF16)],
        compiler_params=_params("arbitrary", "arbitrary", "arbitrary"),
        name="conv_ffn",
    )(x2, x2, g, hist, wug, wuv, cwg, cwv, cbg, cbv, wd)


def _final_norm_kernel(x_ref, g_ref, o_ref):
    x = x_ref[...]
    ms = jnp.mean(x * x, axis=-1, keepdims=True)
    o_ref[...] = x * lax.rsqrt(ms + EPS) * g_ref[...]


def _final_norm(x2, g, tm):
    n = x2.shape[0]
    row = pl.BlockSpec((tm, D_MODEL), lambda i: (i, 0))
    return pl.pallas_call(
        _final_norm_kernel,
        grid=(n // tm,),
        in_specs=[row, pl.BlockSpec((1, D_MODEL), lambda i: (0, 0))],
        out_specs=row,
        out_shape=jax.ShapeDtypeStruct((n, D_MODEL), F32),
        compiler_params=_params("parallel"),
        name="final_norm",
    )(x2, g)


def _layer(x2, nb, L, pos0, pool_hist, s0, k_past, v_past, logf_past, conv_hist, p):
    n = nb * L
    tm = _tile(L, 512)
    u_pool, u_ssm, q, k, v, logf = _inproj(x2, p["g_mix"], p["w_main"], p["w_f"], p["b_f"], tm)

    y_a, pool_new = _pool(u_pool.reshape(nb, L, W_MIX), pool_hist, p["w_pool"], p["pool_scale"],
                          pos0, tm)

    nc = L // SSM_CHUNK
    uc = jnp.transpose(u_ssm.reshape(nb, nc, SSM_CHUNK, SSM_GROUPS, SSM_GROUP_CH),
                       (0, 3, 1, 2, 4)).reshape(nb, SSM_GROUPS, nc, SSM_CHUNK_W)
    yi, vg = _ssm_intra(uc, p["mw"])
    s_in, s_last = _ssm_scan(jnp.transpose(vg, (0, 2, 1, 3)), s0, p["a_vec"], p["b_vec"])
    yg = _ssm_inter(jnp.transpose(s_in, (0, 2, 1, 3)), p["m2"], yi)
    y_ssm = jnp.transpose(yg.reshape(nb, SSM_GROUPS, nc, SSM_CHUNK, SSM_GROUP_CH),
                          (0, 2, 3, 1, 4)).reshape(n, W_MIX)
    y_b = _glu(y_ssm, u_ssm, p["d_skip"], p["w_glu"], p["b_glu"], tm)

    q3, k3, v3 = (a.reshape(nb, L, W_MIX) for a in (q, k, v))
    logf3 = logf.reshape(nb, L, FOX_HEADS)
    if k_past is None:
        k_all, v_all, logf_all = k3, v3, logf3
    else:
        k_all = jnp.concatenate([k_past, k3], axis=1)
        v_all = jnp.concatenate([v_past, v3], axis=1)
        logf_all = jnp.concatenate([logf_past, logf3], axis=1)
    lk = k_all.shape[1]
    c_all = _cumsum(logf_all, _tile(lk, 512))
    cq = jnp.transpose(c_all[:, lk - L:].reshape(nb, L, HEAD_PAIRS, 2), (0, 2, 1, 3))
    ck = jnp.transpose(c_all.reshape(nb, lk, HEAD_PAIRS, 2), (0, 2, 3, 1))
    tk = 512 if lk % 512 == 0 else lk
    y_c = _attention(q3, k_all, v_all, cq, ck, pos0, tm, tk)

    x2 = _merge(x2, p["g_mix"], y_a.reshape(n, W_MIX), y_b, y_c.reshape(n, W_MIX),
                p["w_gate"], p["w_branch"], p["w_out"], tm)
    x2, conv_new = _ffn(x2, p["g_ffn"], conv_hist, p["wug"], p["wuv"], p["cwg"], p["cwv"],
                        p["cbg"], p["cbv"], p["w_down"], nb, L, tm, D_FF // 2)
    states = (pool_new,
              s_last[:, :, :SSM_STATE], s_last[:, :, SSM_STATE:],
              k3.reshape(nb, L, FOX_HEADS, FOX_HEAD_DIM), v3.reshape(nb, L, FOX_HEADS, FOX_HEAD_DIM),
              logf3,
              jnp.transpose(conv_new, (0, 3, 1, 2, 4)).reshape(nb, 2, 2 * D_FF))
    return x2, states


def kernel(x_prompt, x_sample, cache_pool, state_ssm_re, state_ssm_im, cache_k, cache_v, cache_logf, cache_ffn_conv, norm_mix_g, w_in, b_f, w_pool, pool_scale, lam_re, lam_im, log_dt, b_re, b_im, c_re, c_im, d_skip, w_glu, b_glu, w_branch, w_out, norm_ffn_g, w_up, conv_w, conv_b, w_down, norm_final_g):
    bp, lp, _ = x_prompt.shape
    bs, ls, _ = x_sample.shape
    depth = w_in.shape[0]
    past = cache_k.shape[2]
    col_f = 5 * W_MIX
    col_gate = col_f + FOX_HEADS

    hp = x_prompt.reshape(bp * lp, D_MODEL)
    hs = x_sample.reshape(bs * ls, D_MODEL)
    zero_pool = jnp.zeros((bp, POOL_HIST, W_MIX), F32)
    zero_s = jnp.zeros((bp, SSM_GROUPS, 2 * SSM_STATE), F32)
    zero_conv = jnp.zeros((bp, 2, 8, D_FF), F32)

    st_p, st_s = [], []
    for l in range(depth):
        mw, m2, a_vec, b_vec = _ssm_operators(lam_re[l], lam_im[l], log_dt[l], b_re[l], b_im[l],
                                              c_re[l], c_im[l])
        p = dict(
            g_mix=norm_mix_g[l][None], g_ffn=norm_ffn_g[l][None],
            w_main=w_in[l][:, :col_f].astype(BF16),
            w_f=jnp.pad(w_in[l][:, col_f:col_gate], ((0, 0), (0, 128 - FOX_HEADS))).astype(BF16),
            b_f=b_f[l][None],
            w_gate=w_in[l][:, col_gate:].astype(BF16),
            w_pool=w_pool[l].astype(BF16), pool_scale=pool_scale[l][None],
            mw=mw, m2=m2, a_vec=a_vec, b_vec=b_vec,
            d_skip=d_skip[l][None], w_glu=w_glu[l].astype(BF16), b_glu=b_glu[l][None],
            w_branch=w_branch[l].astype(BF16), w_out=w_out[l].astype(BF16),
            wug=w_up[l][:, :D_FF].astype(BF16), wuv=w_up[l][:, D_FF:].astype(BF16),
            cwg=conv_w[l][:, :D_FF], cwv=conv_w[l][:, D_FF:],
            cbg=conv_b[l][None, :D_FF], cbv=conv_b[l][None, D_FF:],
            w_down=w_down[l].astype(BF16),
        )
        hp, sp = _layer(hp, bp, lp, 0, zero_pool, zero_s, None, None, None, zero_conv, p)
        s0 = jnp.concatenate([state_ssm_re[l], state_ssm_im[l]], axis=-1)
        conv_hist = jnp.pad(jnp.transpose(cache_ffn_conv[l].reshape(bs, 2, 2, D_FF), (0, 2, 1, 3)),
                            ((0, 0), (0, 0), (6, 0), (0, 0)))
        hs, ss = _layer(hs, bs, ls, past, cache_pool[l], s0,
                        cache_k[l].reshape(bs, past, W_MIX), cache_v[l].reshape(bs, past, W_MIX),
                        cache_logf[l], conv_hist, p)
        st_p.append(sp)
        st_s.append(ss)

    y_prompt = _final_norm(hp, norm_final_g[None], _tile(bp * lp, 512)).reshape(bp, lp, D_MODEL)
    y_sample = _final_norm(hs, norm_final_g[None], _tile(bs * ls, 512)).reshape(bs, ls, D_MODEL)
    outs = [y_prompt, y_sample]
    for i in range(7):
        outs.append(jnp.stack([s[i] for s in st_p], axis=0))
        outs.append(jnp.stack([s[i] for s in st_s], axis=0))
    return tuple(outs)
```

```python
import functools
import math

import jax
import jax.numpy as jnp
from jax import lax
from jax.experimental import pallas as pl
from jax.experimental.pallas import tpu as pltpu

F32 = jnp.float32
BF16 = jnp.bfloat16
HIGHEST = lax.Precision.HIGHEST

D_MODEL = 1024
W_MIX = 512
POOL_WINDOWS = (2, 4, 8, 16)
POOL_GROUP_W = 128
POOL_HIST = 15
SSM_GROUP_CH = 16
SSM_GROUPS = 32
SSM_STATE = 64
SSM_CHUNK = 16
SSM_CHUNK_W = SSM_CHUNK * SSM_GROUP_CH
FOX_HEADS = 8
FOX_HEAD_DIM = 64
HEAD_PAIRS = FOX_HEADS // 2
D_FF = 2816
EPS = 1e-6
NEG_INF = -1e30
VMEM_LIMIT = 56 * 1024 * 1024


def _params(*sem):
    return pltpu.CompilerParams(dimension_semantics=sem, vmem_limit_bytes=VMEM_LIMIT)


def _tile(n, target, mult=8):
    if n <= target:
        return n
    for t in range(target, 0, -1):
        if n % t == 0 and t % mult == 0:
            return t
    return n


def _rms_bf16(x, g):
    ms = jnp.mean(x * x, axis=-1, keepdims=True)
    return (x * lax.rsqrt(ms + EPS) * g).astype(BF16)


def _sigmoid(x):
    return 1.0 / (1.0 + jnp.exp(-x))


def _inproj_kernel(*refs):
    x_ref, g_ref, w_ref, wf_ref, bf_ref = refs[:5]
    up_ref, us_ref, q_ref, k_ref, v_ref, kb_ref, vb_ref, lf_ref = refs[-8:]
    hb = _rms_bf16(x_ref[...], g_ref[...])

    def proj(idx):
        return jnp.dot(hb, w_ref[:, idx * W_MIX:(idx + 1) * W_MIX], preferred_element_type=F32)

    up_ref[...] = proj(0)
    us_ref[...] = proj(1)
    q_ref[...] = (proj(2) * (FOX_HEAD_DIM ** -0.5)).astype(BF16)
    k = proj(3)
    k_ref[0] = k
    kb_ref[...] = k.astype(BF16)
    v = proj(4)
    v_ref[0] = v
    vb_ref[...] = v.astype(BF16)
    zf = jnp.dot(hb, wf_ref[...], preferred_element_type=F32)
    a = zf[:, :FOX_HEADS] + bf_ref[...]
    lf_ref[...] = jnp.minimum(a, 0.0) - jnp.log(1.0 + jnp.exp(-jnp.abs(a)))


def _inproj(x2, g, w_main, w_f, b_f, tm, layer, depth, kbuf, vbuf):
    n = x2.shape[0]
    row = lambda w: pl.BlockSpec((tm, w), lambda i: (i, 0))
    const = lambda shape: pl.BlockSpec(shape, lambda i: (0,) * len(shape))
    stacked = pl.BlockSpec((1, tm, W_MIX), lambda i: (layer, i, 0))
    f32 = jax.ShapeDtypeStruct((n, W_MIX), F32)
    b16 = jax.ShapeDtypeStruct((n, W_MIX), BF16)
    big = jax.ShapeDtypeStruct((depth, n, W_MIX), F32)
    in_specs = [row(D_MODEL), const((1, D_MODEL)), const((D_MODEL, 5 * W_MIX)),
                const((D_MODEL, 128)), const((1, FOX_HEADS))]
    args = [x2, g, w_main, w_f, b_f]
    aliases = {}
    if kbuf is not None:
        in_specs += [pl.BlockSpec(memory_space=pl.ANY)] * 2
        args += [kbuf, vbuf]
        aliases = {5: 3, 6: 4}
    return pl.pallas_call(
        _inproj_kernel,
        grid=(n // tm,),
        in_specs=in_specs,
        out_specs=[row(W_MIX), row(W_MIX), row(W_MIX), stacked, stacked, row(W_MIX), row(W_MIX),
                   row(FOX_HEADS)],
        out_shape=[f32, f32, b16, big, big, b16, b16, jax.ShapeDtypeStruct((n, FOX_HEADS), F32)],
        input_output_aliases=aliases,
        compiler_params=_params("parallel"),
        name="in_proj",
    )(*args)


def _pool_kernel(u_ref, hist_ref, w_ref, sc_ref, y_ref, pn_ref, buf, *, tm, pos0):
    t = pl.program_id(1)

    @pl.when(t == 0)
    def _():
        buf[0:1, :] = jnp.zeros((1, W_MIX), F32)
        buf[1:16, :] = hist_ref[0]

    @pl.when(t > 0)
    def _():
        buf[0:16, :] = buf[tm:tm + 16, :]

    buf[16:16 + tm, :] = u_ref[0]
    pos = pos0 + t * tm + lax.broadcasted_iota(jnp.int32, (tm, 1), 0)
    for g, w in enumerate(POOL_WINDOWS):
        lo, hi = g * POOL_GROUP_W, (g + 1) * POOL_GROUP_W
        full = buf[:, lo:hi]
        s = full
        k = 1
        while k < w:
            s = s + pltpu.roll(s, k, axis=0)
            k *= 2
        cnt = jnp.minimum(pos + 1, w).astype(F32)
        d = s[16:] * (1.0 / cnt) - full[16:]
        y = jnp.dot(d.astype(BF16), w_ref[g], preferred_element_type=F32) * sc_ref[:, lo:hi]
        y_ref[0, :, lo:hi] = y.astype(BF16)
    pn_ref[0] = buf[tm + 1:tm + 16, :]


def _pool(u, hist, w_pool, scale, pos0, tm):
    nb, L, _ = u.shape
    return pl.pallas_call(
        functools.partial(_pool_kernel, tm=tm, pos0=pos0),
        grid=(nb, L // tm),
        in_specs=[
            pl.BlockSpec((1, tm, W_MIX), lambda b, t: (b, t, 0)),
            pl.BlockSpec((1, POOL_HIST, W_MIX), lambda b, t: (b, 0, 0)),
            pl.BlockSpec((4, POOL_GROUP_W, POOL_GROUP_W), lambda b, t: (0, 0, 0)),
            pl.BlockSpec((1, W_MIX), lambda b, t: (0, 0)),
        ],
        out_specs=[
            pl.BlockSpec((1, tm, W_MIX), lambda b, t: (b, t, 0)),
            pl.BlockSpec((1, POOL_HIST, W_MIX), lambda b, t: (b, 0, 0)),
        ],
        out_shape=[jax.ShapeDtypeStruct((nb, L, W_MIX), BF16),
                   jax.ShapeDtypeStruct((nb, POOL_HIST, W_MIX), F32)],
        scratch_shapes=[pltpu.VMEM((tm + 16, W_MIX), F32)],
        compiler_params=_params("parallel", "arbitrary"),
        name="pool_mixer",
    )(u, hist, w_pool, scale)


def _ssm_operators(lam_re, lam_im, log_dt, b_re, b_im, c_re, c_im):
    T = SSM_CHUNK
    dt = jnp.exp(log_dt)[:, None]
    ar, ai = lam_re * dt, lam_im * dt
    taus = jnp.arange(T + 1, dtype=F32)[:, None, None]
    mag = jnp.exp(taus * ar)
    pr, pi = mag * jnp.cos(taus * ai), mag * jnp.sin(taus * ai)
    lbr, lbi = pr[1], pi[1]
    den = lam_re * lam_re + lam_im * lam_im
    fr = ((lbr - 1.0) * lam_re + lbi * lam_im) / den
    fi = (lbi * lam_re - (lbr - 1.0) * lam_im) / den
    bbr = fr[..., None] * b_re - fi[..., None] * b_im
    bbi = fr[..., None] * b_im + fi[..., None] * b_re
    er = pr[:T, :, :, None] * bbr - pi[:T, :, :, None] * bbi
    ei = pr[:T, :, :, None] * bbi + pi[:T, :, :, None] * bbr
    kt = (jnp.einsum('gcp,tgpd->tgcd', c_re, er, precision=HIGHEST)
          - jnp.einsum('gcp,tgpd->tgcd', c_im, ei, precision=HIGHEST))
    jj = jnp.arange(T)[:, None]
    tt = jnp.arange(T)[None, :]
    lag = tt - jj
    toe = jnp.where((lag >= 0)[:, :, None, None, None], kt[jnp.clip(lag, 0, T - 1)], 0.0)
    m1 = jnp.transpose(toe, (2, 0, 4, 1, 3)).reshape(SSM_GROUPS, SSM_CHUNK_W, SSM_CHUNK_W)
    w1r = jnp.transpose(er[::-1], (1, 0, 3, 2)).reshape(SSM_GROUPS, SSM_CHUNK_W, SSM_STATE)
    w1i = jnp.transpose(ei[::-1], (1, 0, 3, 2)).reshape(SSM_GROUPS, SSM_CHUNK_W, SSM_STATE)
    mw = jnp.concatenate([m1, w1r, w1i], axis=-1)
    qr, qi = pr[1:], pi[1:]
    m2r = c_re[None] * qr[:, :, None, :] - c_im[None] * qi[:, :, None, :]
    m2i = -(c_re[None] * qi[:, :, None, :] + c_im[None] * qr[:, :, None, :])
    m2 = jnp.concatenate([jnp.transpose(m2r, (1, 3, 0, 2)), jnp.transpose(m2i, (1, 3, 0, 2))],
                         axis=1).reshape(SSM_GROUPS, 2 * SSM_STATE, SSM_CHUNK_W)
    a_vec = jnp.concatenate([pr[T], pr[T]], axis=-1)
    b_vec = jnp.concatenate([-pi[T], pi[T]], axis=-1)
    return mw, m2, a_vec, b_vec


def _ssm_intra_kernel(u_ref, mw_ref, yi_ref, v_ref):
    r = jnp.dot(u_ref[0, 0], mw_ref[0], precision=HIGHEST, preferred_element_type=F32)
    yi_ref[0, 0] = r[:, :SSM_CHUNK_W]
    v_ref[0, 0] = r[:, SSM_CHUNK_W:]


def _ssm_intra(uc, mw):
    nb, G, nc, _ = uc.shape
    blk = lambda w: pl.BlockSpec((1, 1, nc, w), lambda b, g: (b, g, 0, 0))
    return pl.pallas_call(
        _ssm_intra_kernel,
        grid=(nb, G),
        in_specs=[blk(SSM_CHUNK_W),
                  pl.BlockSpec((1, SSM_CHUNK_W, SSM_CHUNK_W + 2 * SSM_STATE), lambda b, g: (g, 0, 0))],
        out_specs=[blk(SSM_CHUNK_W), blk(2 * SSM_STATE)],
        out_shape=[jax.ShapeDtypeStruct((nb, G, nc, SSM_CHUNK_W), F32),
                   jax.ShapeDtypeStruct((nb, G, nc, 2 * SSM_STATE), F32)],
        compiler_params=_params("parallel", "parallel"),
        name="ssm_intra",
    )(uc, mw)


def _ssm_scan_kernel(v_ref, s0_ref, a_ref, b_ref, sin_ref, sl_ref, *, nc):
    a = a_ref[...]
    bv = b_ref[...]

    def body(c, s):
        sin_ref[0, c] = s
        return a * s + bv * pltpu.roll(s, SSM_STATE, axis=1) + v_ref[0, c]

    sl_ref[0] = lax.fori_loop(0, nc, body, s0_ref[0])


def _ssm_scan(vt, s0, a_vec, b_vec):
    nb, nc, G, _ = vt.shape
    return pl.pallas_call(
        functools.partial(_ssm_scan_kernel, nc=nc),
        grid=(nb,),
        in_specs=[pl.BlockSpec((1, nc, G, 128), lambda b: (b, 0, 0, 0)),
                  pl.BlockSpec((1, G, 128), lambda b: (b, 0, 0)),
                  pl.BlockSpec((G, 128), lambda b: (0, 0)),
                  pl.BlockSpec((G, 128), lambda b: (0, 0))],
        out_specs=[pl.BlockSpec((1, nc, G, 128), lambda b: (b, 0, 0, 0)),
                   pl.BlockSpec((1, G, 128), lambda b: (b, 0, 0))],
        out_shape=[jax.ShapeDtypeStruct((nb, nc, G, 128), F32),
                   jax.ShapeDtypeStruct((nb, G, 128), F32)],
        compiler_params=_params("parallel"),
        name="ssm_scan",
    )(vt, s0, a_vec, b_vec)


def _ssm_inter_kernel(s_ref, m2_ref, yi_ref, y_ref):
    y_ref[0, 0] = yi_ref[0, 0] + jnp.dot(s_ref[0, 0], m2_ref[0], precision=HIGHEST,
                                         preferred_element_type=F32)


def _ssm_inter(sg, m2, yi):
    nb, G, nc, _ = sg.shape
    blk = lambda w: pl.BlockSpec((1, 1, nc, w), lambda b, g: (b, g, 0, 0))
    return pl.pallas_call(
        _ssm_inter_kernel,
        grid=(nb, G),
        in_specs=[blk(2 * SSM_STATE),
                  pl.BlockSpec((1, 2 * SSM_STATE, SSM_CHUNK_W), lambda b, g: (g, 0, 0)),
                  blk(SSM_CHUNK_W)],
        out_specs=blk(SSM_CHUNK_W),
        out_shape=jax.ShapeDtypeStruct((nb, G, nc, SSM_CHUNK_W), F32),
        compiler_params=_params("parallel", "parallel"),
        name="ssm_inter",
    )(sg, m2, yi)


def _glu_kernel(y_ref, u_ref, d_ref, w_ref, b_ref, o_ref):
    y = y_ref[...] + d_ref[...] * u_ref[...]
    y = 0.5 * y * (1.0 + jnp.tanh(math.sqrt(2.0 / math.pi) * (y + 0.044715 * (y * y * y))))
    z = jnp.dot(y.astype(BF16), w_ref[...], preferred_element_type=F32) + b_ref[...]
    o_ref[...] = (y * _sigmoid(z)).astype(BF16)


def _glu(y2, u2, d_skip, w_glu, b_glu, tm):
    n = y2.shape[0]
    row = pl.BlockSpec((tm, W_MIX), lambda i: (i, 0))
    vec = pl.BlockSpec((1, W_MIX), lambda i: (0, 0))
    return pl.pallas_call(
        _glu_kernel,
        grid=(n // tm,),
        in_specs=[row, row, vec, pl.BlockSpec((W_MIX, W_MIX), lambda i: (0, 0)), vec],
        out_specs=row,
        out_shape=jax.ShapeDtypeStruct((n, W_MIX), BF16),
        compiler_params=_params("parallel"),
        name="ssm_glu",
    )(y2, u2, d_skip, w_glu, b_glu)


SSM_HALF_CH = 256
SSM_HALF_ST = 1024
SSM_ALL_ST = 2048


def _ssm_blockdiag(lam_re, lam_im, log_dt, b_re, b_im, c_re, c_im):
    dt = jnp.exp(log_dt)[:, None]
    ar, ai = lam_re * dt, lam_im * dt

    def power(n):
        mag = jnp.exp(n * ar)
        return mag * jnp.cos(n * ai), mag * jnp.sin(n * ai)

    lbr, lbi = power(1.0)
    a16r, a16i = power(float(SSM_CHUNK))
    den = lam_re * lam_re + lam_im * lam_im
    fr = ((lbr - 1.0) * lam_re + lbi * lam_im) / den
    fi = (lbi * lam_re - (lbr - 1.0) * lam_im) / den
    bbr = fr[..., None] * b_re - fi[..., None] * b_im
    bbi = fr[..., None] * b_im + fi[..., None] * b_re
    eye = jnp.eye(SSM_GROUPS, dtype=F32)

    def bd_in(m):
        return jnp.einsum('gpc,gh->gchp', m, eye).reshape(W_MIX, SSM_ALL_ST)

    def bd_out(m):
        return jnp.einsum('gcp,gh->gphc', m, eye).reshape(SSM_ALL_ST, W_MIX)

    def halves(m, rw, cw):
        return jnp.stack([m[h * rw:(h + 1) * rw, h * cw:(h + 1) * cw] for h in range(2)])

    bdb = jnp.concatenate([halves(bd_in(bbr), SSM_HALF_CH, SSM_HALF_ST),
                           halves(bd_in(bbi), SSM_HALF_CH, SSM_HALF_ST)], axis=2).astype(BF16)
    bdc = jnp.concatenate([halves(bd_out(c_re), SSM_HALF_ST, SSM_HALF_CH),
                           halves(bd_out(-c_im), SSM_HALF_ST, SSM_HALF_CH)], axis=1).astype(BF16)
    flat = lambda m: m.reshape(1, SSM_ALL_ST)
    return dict(bdb=bdb, bdc=bdc, lam_r=flat(lbr), lam_i=flat(lbi), a16r=flat(a16r), a16i=flat(a16i))


def _ssm_step(u_ref, bdb_ref, lr, li, t, h, s):
    lo = t * W_MIX + h * SSM_HALF_CH
    bu = jnp.dot(u_ref[:, lo:lo + SSM_HALF_CH].astype(BF16), bdb_ref[h], preferred_element_type=F32)
    bur, bui = bu[:, :SSM_HALF_ST], bu[:, SSM_HALF_ST:]
    if s is None:
        return bur, bui
    sr, si = s
    return lr * sr - li * si + bur, lr * si + li * sr + bui


def _ssm_state_kernel(u_ref, bdb_ref, lr_ref, li_ref, vr_ref, vi_ref):
    for h in range(2):
        st = slice(h * SSM_HALF_ST, (h + 1) * SSM_HALF_ST)
        lr, li = lr_ref[:, st], li_ref[:, st]
        s = None
        for t in range(SSM_CHUNK):
            s = _ssm_step(u_ref, bdb_ref, lr, li, t, h, s)
        vr_ref[:, st] = s[0]
        vi_ref[:, st] = s[1]


def _ssm_state(u2, sp, tc):
    r = u2.shape[0]
    row = lambda w: pl.BlockSpec((tc, w), lambda i: (i, 0))
    const = lambda shape: pl.BlockSpec(shape, lambda i: (0,) * len(shape))
    out = jax.ShapeDtypeStruct((r, SSM_ALL_ST), F32)
    return pl.pallas_call(
        _ssm_state_kernel,
        grid=(r // tc,),
        in_specs=[row(SSM_CHUNK * W_MIX), const((2, SSM_HALF_CH, 2 * SSM_HALF_ST)),
                  const((1, SSM_ALL_ST)), const((1, SSM_ALL_ST))],
        out_specs=[row(SSM_ALL_ST), row(SSM_ALL_ST)],
        out_shape=[out, out],
        compiler_params=_params("parallel"),
        name="ssm_state",
    )(u2, sp["bdb"], sp["lam_r"], sp["lam_i"])


def _ssm_carry_kernel(vr_ref, vi_ref, s0r_ref, s0i_ref, ar_ref, ai_ref,
                      sr_ref, si_ref, lr_ref, li_ref, *, nc):
    ar, ai = ar_ref[...], ai_ref[...]

    def body(c, s):
        sr, si = s
        sr_ref[0, pl.ds(c, 1), :] = sr
        si_ref[0, pl.ds(c, 1), :] = si
        return (ar * sr - ai * si + vr_ref[0, pl.ds(c, 1), :],
                ar * si + ai * sr + vi_ref[0, pl.ds(c, 1), :])

    sr, si = lax.fori_loop(0, nc, body, (s0r_ref[0], s0i_ref[0]))
    lr_ref[0] = sr
    li_ref[0] = si


def _ssm_carry(vr, vi, s0r, s0i, sp):
    nb, nc, _ = vr.shape
    seq = pl.BlockSpec((1, nc, SSM_ALL_ST), lambda b: (b, 0, 0))
    one = pl.BlockSpec((1, 1, SSM_ALL_ST), lambda b: (b, 0, 0))
    vec = pl.BlockSpec((1, SSM_ALL_ST), lambda b: (0, 0))
    return pl.pallas_call(
        functools.partial(_ssm_carry_kernel, nc=nc),
        grid=(nb,),
        in_specs=[seq, seq, one, one, vec, vec],
        out_specs=[seq, seq, one, one],
        out_shape=[jax.ShapeDtypeStruct((nb, nc, SSM_ALL_ST), F32)] * 2
                  + [jax.ShapeDtypeStruct((nb, 1, SSM_ALL_ST), F32)] * 2,
        compiler_params=_params("parallel"),
        name="ssm_carry",
    )(vr, vi, s0r, s0i, sp["a16r"], sp["a16i"])


def _ssm_out_kernel(u_ref, sr_ref, si_ref, bdb_ref, bdc_ref, lr_ref, li_ref, d_ref, w_ref, b_ref,
                    o_ref):
    lam, s = [], []
    for h in range(2):
        st = slice(h * SSM_HALF_ST, (h + 1) * SSM_HALF_ST)
        lam.append((lr_ref[:, st], li_ref[:, st]))
        s.append((sr_ref[:, st], si_ref[:, st]))
    for t in range(SSM_CHUNK):
        ys = []
        for h in range(2):
            s[h] = _ssm_step(u_ref, bdb_ref, lam[h][0], lam[h][1], t, h, s[h])
            ys.append(jnp.dot(s[h][0].astype(BF16), bdc_ref[h, :SSM_HALF_ST, :],
                              preferred_element_type=F32)
                      + jnp.dot(s[h][1].astype(BF16), bdc_ref[h, SSM_HALF_ST:, :],
                                preferred_element_type=F32))
        tok = slice(t * W_MIX, (t + 1) * W_MIX)
        y = jnp.concatenate(ys, axis=1) + d_ref[...] * u_ref[:, tok]
        y = 0.5 * y * (1.0 + jnp.tanh(math.sqrt(2.0 / math.pi) * (y + 0.044715 * (y * y * y))))
        z = jnp.dot(y.astype(BF16), w_ref[...], preferred_element_type=F32) + b_ref[...]
        o_ref[:, tok] = (y * _sigmoid(z)).astype(BF16)


def _ssm_out(u2, sr, si, sp, d_skip, w_glu, b_glu, tc):
    r = u2.shape[0]
    row = lambda w: pl.BlockSpec((tc, w), lambda i: (i, 0))
    const = lambda shape: pl.BlockSpec(shape, lambda i: (0,) * len(shape))
    return pl.pallas_call(
        _ssm_out_kernel,
        grid=(r // tc,),
        in_specs=[row(SSM_CHUNK * W_MIX), row(SSM_ALL_ST), row(SSM_ALL_ST),
                  const((2, SSM_HALF_CH, 2 * SSM_HALF_ST)), const((2, 2 * SSM_HALF_ST, SSM_HALF_CH)),
                  const((1, SSM_ALL_ST)), const((1, SSM_ALL_ST)),
                  const((1, W_MIX)), const((W_MIX, W_MIX)), const((1, W_MIX))],
        out_specs=row(SSM_CHUNK * W_MIX),
        out_shape=jax.ShapeDtypeStruct((r, SSM_CHUNK * W_MIX), BF16),
        compiler_params=_params("parallel"),
        name="ssm_out",
    )(u2, sr, si, sp["bdb"], sp["bdc"], sp["lam_r"], sp["lam_i"], d_skip, w_glu, b_glu)


def _cumsum_kernel(lf_ref, c_ref, carry, *, tc):
    @pl.when(pl.program_id(1) == 0)
    def _():
        carry[...] = jnp.zeros_like(carry)

    r = lax.broadcasted_iota(jnp.int32, (tc, tc), 0)
    c = lax.broadcasted_iota(jnp.int32, (tc, tc), 1)
    tri = (r >= c).astype(F32)
    cs = jnp.dot(tri, lf_ref[0], precision=HIGHEST, preferred_element_type=F32) + carry[...]
    c_ref[0] = cs
    carry[...] = cs[tc - 1:tc, :]


def _cumsum(lf, tc):
    nb, lk, _ = lf.shape
    blk = pl.BlockSpec((1, tc, FOX_HEADS), lambda b, t: (b, t, 0))
    return pl.pallas_call(
        functools.partial(_cumsum_kernel, tc=tc),
        grid=(nb, lk // tc),
        in_specs=[blk],
        out_specs=blk,
        out_shape=jax.ShapeDtypeStruct((nb, lk, FOX_HEADS), F32),
        scratch_shapes=[pltpu.VMEM((1, FOX_HEADS), F32)],
        compiler_params=_params("parallel", "arbitrary"),
        name="logf_cumsum",
    )(lf)


ATTN_ROWS = 16
LOG2E = math.log2(math.e)


def _attn_kernel(qi_tab, kj_tab, q_ref, k_ref, v_ref, cq_ref, ck_ref, o_ref,
                 s_scr, p_scr, m_scr, a_scr, cq_scr, acc_scr, *, tq, tk, q_off):
    pidx = pl.program_id(2)
    qi = qi_tab[pidx]
    kj = kj_tab[pidx]
    q_lo = q_off + qi * tq
    k_lo = kj * tk
    cols = [(c, min(c + 128, tk)) for c in range(0, tk, 128)]
    lane = lax.broadcasted_iota(jnp.int32, (1, 2 * FOX_HEAD_DIM), 1)

    @pl.when(kj == 0)
    def _():
        m_scr[...] = jnp.full_like(m_scr, NEG_INF)
        acc_scr[...] = jnp.zeros_like(acc_scr)
        cq = cq_ref[0, 0] * LOG2E
        for hh in range(2):
            cq_scr[hh] = jnp.broadcast_to(cq[:, hh:hh + 1], (tq, 128))

    q = q_ref[0]
    kb = k_ref[0].astype(BF16)
    v = v_ref[0]
    ck = ck_ref[0, 0] * LOG2E

    def softmax_rows(hh, masked):
        ckb = [jnp.broadcast_to(ck[hh:hh + 1, lo:hi], (ATTN_ROWS, hi - lo)) for lo, hi in cols]
        for r in range(0, tq, ATTN_ROWS):
            t = [s_scr[hh, r:r + ATTN_ROWS, lo:hi] * LOG2E - ckb[c]
                 for c, (lo, hi) in enumerate(cols)]
            if masked:
                qpos = q_lo + r + lax.broadcasted_iota(jnp.int32, (ATTN_ROWS, 128), 0)
                kpos = k_lo + lax.broadcasted_iota(jnp.int32, (ATTN_ROWS, 128), 1)
                t = [jnp.where((kpos + lo <= qpos)[:, :hi - lo], t[c], NEG_INF)
                     for c, (lo, hi) in enumerate(cols)]
            full = [t[c] for c, (lo, hi) in enumerate(cols) if hi - lo == 128]
            tmax = full[0]
            for tc in full[1:]:
                tmax = jnp.maximum(tmax, tc)
            tmax = jnp.max(tmax, axis=1, keepdims=True)
            if len(full) < len(cols):
                tmax = jnp.maximum(tmax, jnp.max(t[-1], axis=1, keepdims=True))
            tmax = jnp.broadcast_to(tmax, (ATTN_ROWS, 128))
            cq2 = cq_scr[hh, r:r + ATTN_ROWS, :]
            m_old = m_scr[hh, r:r + ATTN_ROWS, :]
            m_new = jnp.maximum(m_old, tmax + cq2)
            m_scr[hh, r:r + ATTN_ROWS, :] = m_new
            a_scr[hh, r:r + ATTN_ROWS, :] = jnp.exp2(m_old - m_new)
            mt = m_new - cq2
            for c, (lo, hi) in enumerate(cols):
                p_scr[hh, r:r + ATTN_ROWS, lo:hi] = jnp.exp2(t[c] - mt[:, :hi - lo]).astype(BF16)

    def step(masked):
        for hh in range(2):
            in_head = (lane >= hh * FOX_HEAD_DIM) & (lane < (hh + 1) * FOX_HEAD_DIM)
            qm = jnp.where(in_head, q, jnp.zeros_like(q))
            s_scr[hh] = lax.dot_general(qm, kb, (((1,), (1,)), ((), ())),
                                        preferred_element_type=F32)
            softmax_rows(hh, masked)
            vh = jnp.where(in_head, v, 1.0).astype(BF16)
            pv = jnp.dot(p_scr[hh], vh, preferred_element_type=F32)
            acc_scr[hh] = a_scr[hh] * acc_scr[hh] + pv

    needs_mask = k_lo + tk - 1 > q_lo

    @pl.when(needs_mask)
    def _():
        step(True)

    @pl.when(jnp.logical_not(needs_mask))
    def _():
        step(False)

    @pl.when(kj == (q_lo + tq - 1) // tk)
    def _():
        o = []
        for hh in range(2):
            acc = acc_scr[hh]
            o.append(acc * (1.0 / pltpu.roll(acc, FOX_HEAD_DIM, axis=1)))
        o_ref[0] = jnp.where(lane < FOX_HEAD_DIM, o[0], o[1]).astype(BF16)


def _attention(q, k_all, v_all, cq, ck, q_off, tq, tk):
    nb, lq, _ = q.shape
    nq = lq // tq
    pairs = [(qi, kj) for qi in range(nq) for kj in range((q_off + qi * tq + tq - 1) // tk + 1)]
    qi_tab = jnp.asarray([p[0] for p in pairs], jnp.int32)
    kj_tab = jnp.asarray([p[1] for p in pairs], jnp.int32)
    qmap = lambda b, hp, i, qt, kt: (b, qt[i], hp)
    kmap = lambda b, hp, i, qt, kt: (b, kt[i], hp)
    return pl.pallas_call(
        functools.partial(_attn_kernel, tq=tq, tk=tk, q_off=q_off),
        grid_spec=pltpu.PrefetchScalarGridSpec(
            num_scalar_prefetch=2,
            grid=(nb, HEAD_PAIRS, len(pairs)),
            in_specs=[
                pl.BlockSpec((1, tq, 128), qmap),
                pl.BlockSpec((1, tk, 128), kmap),
                pl.BlockSpec((1, tk, 128), kmap),
                pl.BlockSpec((1, 1, tq, 2), lambda b, hp, i, qt, kt: (b, hp, qt[i], 0)),
                pl.BlockSpec((1, 1, 2, tk), lambda b, hp, i, qt, kt: (b, hp, 0, kt[i])),
            ],
            out_specs=pl.BlockSpec((1, tq, 128), qmap),
            scratch_shapes=[pltpu.VMEM((2, tq, tk), F32), pltpu.VMEM((2, tq, tk), BF16),
                            pltpu.VMEM((2, tq, 128), F32), pltpu.VMEM((2, tq, 128), F32),
                            pltpu.VMEM((2, tq, 128), F32), pltpu.VMEM((2, tq, 128), F32)],
        ),
        out_shape=jax.ShapeDtypeStruct((nb, lq, W_MIX), BF16),
        compiler_params=_params("parallel", "parallel", "arbitrary"),
        name="fox_attention",
    )(qi_tab, kj_tab, q, k_all, v_all, cq, ck)


def _merge_kernel(x_ref, g_ref, ya_ref, yb_ref, yc_ref, wg_ref, wb_ref, wo_ref, o_ref):
    x = x_ref[...]
    hb = _rms_bf16(x, g_ref[...])
    merged = None
    for b, y_ref in enumerate((ya_ref, yb_ref, yc_ref)):
        gate = _sigmoid(jnp.dot(hb, wg_ref[:, b * D_MODEL:(b + 1) * D_MODEL],
                                preferred_element_type=F32))
        term = gate * jnp.dot(y_ref[...], wb_ref[b], preferred_element_type=F32)
        merged = term if merged is None else merged + term
    o_ref[...] = x + jnp.dot(merged.astype(BF16), wo_ref[...], preferred_element_type=F32)


def _merge(x2, g, ya, yb, yc, w_gate, w_branch, w_out, tm):
    n = x2.shape[0]
    row = lambda w: pl.BlockSpec((tm, w), lambda i: (i, 0))
    const = lambda shape: pl.BlockSpec(shape, lambda i: (0,) * len(shape))
    return pl.pallas_call(
        _merge_kernel,
        grid=(n // tm,),
        in_specs=[row(D_MODEL), const((1, D_MODEL)), row(W_MIX), row(W_MIX), row(W_MIX),
                  const((D_MODEL, 3 * D_MODEL)), const((3, W_MIX, D_MODEL)),
                  const((D_MODEL, D_MODEL))],
        out_specs=row(D_MODEL),
        out_shape=jax.ShapeDtypeStruct((n, D_MODEL), F32),
        compiler_params=_params("parallel"),
        name="branch_merge",
    )(x2, g, ya, yb, yc, w_gate, w_branch, w_out)


FFN_HALO = 16


FFN_SUB = 256


def _ffn_kernel(x_ref, xh_ref, g_ref, hist_ref, wug_ref, wuv_ref, cwg_ref, cwv_ref,
                cbg_ref, cbv_ref, wd_ref, o_ref, cn_ref, h_scr, hh_scr, acc_scr, act_scr,
                *, tm, tn, nj):
    t = pl.program_id(1)
    j = pl.program_id(2)

    @pl.when(j == 0)
    def _():
        h_scr[...] = _rms_bf16(x_ref[...], g_ref[...])
        hh_scr[...] = _rms_bf16(xh_ref[...], g_ref[...])

    first = t == 0
    for lo in range(0, tn, FFN_SUB):
        hi = min(lo + FFN_SUB, tn)
        conv = []
        for hf, (wu_ref, cw_ref, cb_ref) in enumerate(((wug_ref, cwg_ref, cbg_ref),
                                                       (wuv_ref, cwv_ref, cbv_ref))):
            w = wu_ref[:, lo:hi]
            up = jnp.dot(h_scr[...], w, preferred_element_type=F32)
            uph = jnp.dot(hh_scr[...], w, preferred_element_type=F32)
            prev = jnp.where(first, hist_ref[0, hf, :, lo:hi], uph[FFN_HALO - 8:, :])
            full = jnp.concatenate([prev, up], axis=0)
            cn_ref[0, hf, j, :, lo:hi] = up[tm - 2:, :]
            cw = cw_ref[:, lo:hi]
            conv.append(cb_ref[:, lo:hi] + full[6:6 + tm, :] * cw[0:1, :]
                        + full[7:7 + tm, :] * cw[1:2, :] + up * cw[2:3, :])
        gate, val = conv
        act_scr[:, lo:hi] = (gate * _sigmoid(gate) * val).astype(BF16)
    down = jnp.dot(act_scr[...], wd_ref[...], preferred_element_type=F32)

    if nj == 1:
        o_ref[...] = x_ref[...] + down
        return

    @pl.when(j == 0)
    def _():
        acc_scr[...] = down

    @pl.when((j > 0) & (j < nj - 1))
    def _():
        acc_scr[...] += down

    @pl.when(j == nj - 1)
    def _():
        o_ref[...] = x_ref[...] + acc_scr[...] + down


def _ffn(x2, g, hist, wug, wuv, cwg, cwv, cbg, cbv, wd, nb, L, tm, tn):
    n = x2.shape[0]
    nt, nj = L // tm, D_FF // tn
    hpt = tm // FFN_HALO
    row = lambda b, t, j: (b * nt + t, 0)
    col = lambda b, t, j: (0, j)
    return pl.pallas_call(
        functools.partial(_ffn_kernel, tm=tm, tn=tn, nj=nj),
        grid=(nb, nt, nj),
        in_specs=[
            pl.BlockSpec((tm, D_MODEL), row),
            pl.BlockSpec((FFN_HALO, D_MODEL), lambda b, t, j: (jnp.maximum((b * nt + t) * hpt - 1, 0), 0)),
            pl.BlockSpec((1, D_MODEL), lambda b, t, j: (0, 0)),
            pl.BlockSpec((1, 2, 8, tn), lambda b, t, j: (b, 0, 0, j)),
            pl.BlockSpec((D_MODEL, tn), col),
            pl.BlockSpec((D_MODEL, tn), col),
            pl.BlockSpec((3, tn), col),
            pl.BlockSpec((3, tn), col),
            pl.BlockSpec((1, tn), col),
            pl.BlockSpec((1, tn), col),
            pl.BlockSpec((tn, D_MODEL), lambda b, t, j: (j, 0)),
        ],
        out_specs=[pl.BlockSpec((tm, D_MODEL), row),
                   pl.BlockSpec((1, 2, nj, 2, tn), lambda b, t, j: (b, 0, 0, 0, 0))],
        out_shape=[jax.ShapeDtypeStruct((n, D_MODEL), F32),
                   jax.ShapeDtypeStruct((nb, 2, nj, 2, tn), F32)],
        scratch_shapes=[pltpu.VMEM((tm, D_MODEL), BF16), pltpu.VMEM((FFN_HALO, D_MODEL), BF16),
                        pltpu.VMEM((tm, D_MODEL), F32), pltpu.VMEM((tm, tn), BF16)],
        compiler_params=_params("arbitrary", "arbitrary", "arbitrary"),
        name="conv_ffn",
    )(x2, x2, g, hist, wug, wuv, cwg, cwv, cbg, cbv, wd)


def _final_norm_kernel(x_ref, g_ref, o_ref):
    x = x_ref[...]
    ms = jnp.mean(x * x, axis=-1, keepdims=True)
    o_ref[...] = x * lax.rsqrt(ms + EPS) * g_ref[...]


def _final_norm(x2, g, tm):
    n = x2.shape[0]
    row = pl.BlockSpec((tm, D_MODEL), lambda i: (i, 0))
    return pl.pallas_call(
        _final_norm_kernel,
        grid=(n // tm,),
        in_specs=[row, pl.BlockSpec((1, D_MODEL), lambda i: (0, 0))],
        out_specs=row,
        out_shape=jax.ShapeDtypeStruct((n, D_MODEL), F32),
        compiler_params=_params("parallel"),
        name="final_norm",
    )(x2, g)


def _layer(x2, nb, L, pos0, pool_hist, s0, k_past, v_past, logf_past, conv_hist, p,
           layer, depth, kbuf, vbuf):
    n = nb * L
    tm = _tile(L, 512)
    u_pool, u_ssm, q, kbuf, vbuf, kb, vb, logf = _inproj(
        x2, p["g_mix"], p["w_main"], p["w_f"], p["b_f"], tm, layer, depth, kbuf, vbuf)

    y_a, pool_new = _pool(u_pool.reshape(nb, L, W_MIX), pool_hist, p["w_pool"], p["pool_scale"],
                          pos0, tm)

    nc = L // SSM_CHUNK
    sp = p["ssm"]
    u2 = u_ssm.reshape(nb * nc, SSM_CHUNK * W_MIX)
    tc = _tile(nb * nc, 128)
    vr, vi = _ssm_state(u2, sp, tc)
    sr, si, sl_re, sl_im = _ssm_carry(vr.reshape(nb, nc, SSM_ALL_ST), vi.reshape(nb, nc, SSM_ALL_ST),
                                      s0[0], s0[1], sp)
    y_b = _ssm_out(u2, sr.reshape(nb * nc, SSM_ALL_ST), si.reshape(nb * nc, SSM_ALL_ST), sp,
                   p["d_skip"], p["w_glu"], p["b_glu"], tc).reshape(n, W_MIX)

    q3 = q.reshape(nb, L, W_MIX)
    logf3 = logf.reshape(nb, L, FOX_HEADS)
    if k_past is None:
        k_all, v_all, logf_all = kb.reshape(nb, L, W_MIX), vb.reshape(nb, L, W_MIX), logf3
    else:
        k_all = jnp.concatenate([k_past, kbuf[layer].reshape(nb, L, W_MIX)], axis=1)
        v_all = jnp.concatenate([v_past, vbuf[layer].reshape(nb, L, W_MIX)], axis=1)
        logf_all = jnp.concatenate([logf_past, logf3], axis=1)
    lk = k_all.shape[1]
    c_all = _cumsum(logf_all, _tile(lk, 512))
    cq = jnp.transpose(c_all[:, lk - L:].reshape(nb, L, HEAD_PAIRS, 2), (0, 2, 1, 3))
    ck = jnp.transpose(c_all.reshape(nb, lk, HEAD_PAIRS, 2), (0, 2, 3, 1))
    tk = 512 if lk % 512 == 0 else lk
    y_c = _attention(q3, k_all, v_all, cq, ck, pos0, tm, tk)

    x2 = _merge(x2, p["g_mix"], y_a.reshape(n, W_MIX), y_b, y_c.reshape(n, W_MIX),
                p["w_gate"], p["w_branch"], p["w_out"], tm)
    x2, conv_new = _ffn(x2, p["g_ffn"], conv_hist, p["wug"], p["wuv"], p["cwg"], p["cwv"],
                        p["cbg"], p["cbv"], p["w_down"], nb, L, tm, D_FF // 2)
    states = (pool_new,
              sl_re.reshape(nb, SSM_GROUPS, SSM_STATE), sl_im.reshape(nb, SSM_GROUPS, SSM_STATE),
              logf3,
              jnp.transpose(conv_new, (0, 3, 1, 2, 4)).reshape(nb, 2, 2 * D_FF))
    return x2, states, kbuf, vbuf


def kernel(x_prompt, x_sample, cache_pool, state_ssm_re, state_ssm_im, cache_k, cache_v, cache_logf, cache_ffn_conv, norm_mix_g, w_in, b_f, w_pool, pool_scale, lam_re, lam_im, log_dt, b_re, b_im, c_re, c_im, d_skip, w_glu, b_glu, w_branch, w_out, norm_ffn_g, w_up, conv_w, conv_b, w_down, norm_final_g):
    bp, lp, _ = x_prompt.shape
    bs, ls, _ = x_sample.shape
    depth = w_in.shape[0]
    past = cache_k.shape[2]
    col_f = 5 * W_MIX
    col_gate = col_f + FOX_HEADS

    hp = x_prompt.reshape(bp * lp, D_MODEL)
    hs = x_sample.reshape(bs * ls, D_MODEL)
    zero_pool = jnp.zeros((bp, POOL_HIST, W_MIX), F32)
    zero_s = (jnp.zeros((bp, 1, SSM_ALL_ST), F32),) * 2
    zero_conv = jnp.zeros((bp, 2, 8, D_FF), F32)

    st_p, st_s = [], []
    kp = vp = ks = vs = None
    for l in range(depth):
        p = dict(
            ssm=_ssm_blockdiag(lam_re[l], lam_im[l], log_dt[l], b_re[l], b_im[l], c_re[l], c_im[l]),
            g_mix=norm_mix_g[l][None], g_ffn=norm_ffn_g[l][None],
            w_main=w_in[l][:, :col_f].astype(BF16),
            w_f=jnp.pad(w_in[l][:, col_f:col_gate], ((0, 0), (0, 128 - FOX_HEADS))).astype(BF16),
            b_f=b_f[l][None],
            w_gate=w_in[l][:, col_gate:].astype(BF16),
            w_pool=w_pool[l].astype(BF16), pool_scale=pool_scale[l][None],
            d_skip=d_skip[l][None], w_glu=w_glu[l].astype(BF16), b_glu=b_glu[l][None],
            w_branch=w_branch[l].astype(BF16), w_out=w_out[l].astype(BF16),
            wug=w_up[l][:, :D_FF].astype(BF16), wuv=w_up[l][:, D_FF:].astype(BF16),
            cwg=conv_w[l][:, :D_FF], cwv=conv_w[l][:, D_FF:],
            cbg=conv_b[l][None, :D_FF], cbv=conv_b[l][None, D_FF:],
            w_down=w_down[l].astype(BF16),
        )
        hp, sp, kp, vp = _layer(hp, bp, lp, 0, zero_pool, zero_s, None, None, None, zero_conv, p,
                                l, depth, kp, vp)
        s0 = (state_ssm_re[l].reshape(bs, 1, SSM_ALL_ST), state_ssm_im[l].reshape(bs, 1, SSM_ALL_ST))
        conv_hist = jnp.pad(jnp.transpose(cache_ffn_conv[l].reshape(bs, 2, 2, D_FF), (0, 2, 1, 3)),
                            ((0, 0), (0, 0), (6, 0), (0, 0)))
        hs, ss, ks, vs = _layer(hs, bs, ls, past, cache_pool[l], s0,
                                cache_k[l].reshape(bs, past, W_MIX), cache_v[l].reshape(bs, past, W_MIX),
                                cache_logf[l], conv_hist, p, l, depth, ks, vs)
        st_p.append(sp)
        st_s.append(ss)

    y_prompt = _final_norm(hp, norm_final_g[None], _tile(bp * lp, 512)).reshape(bp, lp, D_MODEL)
    y_sample = _final_norm(hs, norm_final_g[None], _tile(bs * ls, 512)).reshape(bs, ls, D_MODEL)
    stack = lambda st, i: jnp.stack([s[i] for s in st], axis=0)
    heads_p = (depth, bp, lp, FOX_HEADS, FOX_HEAD_DIM)
    heads_s = (depth, bs, ls, FOX_HEADS, FOX_HEAD_DIM)
    return (y_prompt, y_sample,
            stack(st_p, 0), stack(st_s, 0), stack(st_p, 1), stack(st_s, 1),
            stack(st_p, 2), stack(st_s, 2),
            kp.reshape(heads_p), ks.reshape(heads_s), vp.reshape(heads_p), vs.reshape(heads_s),
            stack(st_p, 3), stack(st_s, 3), stack(st_p, 4), stack(st_s, 4))
```

```python
import functools
import math

import jax
import jax.numpy as jnp
from jax import lax
from jax.experimental import pallas as pl
from jax.experimental.pallas import tpu as pltpu

F32 = jnp.float32
BF16 = jnp.bfloat16
HIGHEST = lax.Precision.HIGHEST

D_MODEL = 1024
W_MIX = 512
POOL_WINDOWS = (2, 4, 8, 16)
POOL_GROUP_W = 128
POOL_HIST = 15
SSM_GROUP_CH = 16
SSM_GROUPS = 32
SSM_STATE = 64
SSM_CHUNK = 16
SSM_CHUNK_W = SSM_CHUNK * SSM_GROUP_CH
FOX_HEADS = 8
FOX_HEAD_DIM = 64
HEAD_PAIRS = FOX_HEADS // 2
D_FF = 2816
EPS = 1e-6
NEG_INF = -1e30
VMEM_LIMIT = 56 * 1024 * 1024


def _params(*sem):
    return pltpu.CompilerParams(dimension_semantics=sem, vmem_limit_bytes=VMEM_LIMIT)


def _tile(n, target, mult=8):
    if n <= target:
        return n
    for t in range(target, 0, -1):
        if n % t == 0 and t % mult == 0:
            return t
    return n


def _rms_bf16(x, g):
    ms = jnp.mean(x * x, axis=-1, keepdims=True)
    return (x * lax.rsqrt(ms + EPS) * g).astype(BF16)


def _sigmoid(x):
    return 1.0 / (1.0 + jnp.exp(-x))


LANE_BLOCKS = W_MIX // 128


def _load_lane_blocks(ref, rows, lo=0, hi=LANE_BLOCKS):
    return jnp.concatenate([ref[j, rows, :] for j in range(lo, hi)], axis=1)


def _store_lane_blocks(ref, rows, val):
    for j in range(LANE_BLOCKS):
        ref[j, rows, :] = val[:, j * 128:(j + 1) * 128]


def _inproj_kernel(*refs):
    x_ref, g_ref, w_ref, wf_ref, bf_ref, sel_ref = refs[:6]
    up_ref, us_ref, q_ref, k_ref, v_ref, kb_ref, vb_ref, lf_ref, st_ref = refs[-9:]
    hb = _rms_bf16(x_ref[...], g_ref[...])

    def proj(idx):
        return jnp.dot(hb, w_ref[:, idx * W_MIX:(idx + 1) * W_MIX], preferred_element_type=F32)

    def head_norm2_max(a):
        return jnp.max(jnp.dot((a * a).astype(BF16), sel_ref[...], preferred_element_type=F32),
                       axis=0, keepdims=True)

    up_ref[...] = proj(0)
    _store_lane_blocks(us_ref, slice(None), proj(1))
    qn = proj(2) * (FOX_HEAD_DIM ** -0.5)
    q_ref[...] = (qn * LOG2E).astype(BF16)
    k = proj(3)
    k_ref[0] = k
    kb_ref[...] = k.astype(BF16)
    row = lax.broadcasted_iota(jnp.int32, (8, 128), 0)
    st_ref[0] = jnp.where(row == 0, head_norm2_max(qn), jnp.where(row == 1, head_norm2_max(k), 0.0))
    v = proj(4)
    v_ref[0] = v
    vb_ref[...] = v.astype(BF16)
    zf = jnp.dot(hb, wf_ref[...], preferred_element_type=F32)
    a = zf[:, :FOX_HEADS] + bf_ref[...]
    lf_ref[...] = jnp.minimum(a, 0.0) - jnp.log(1.0 + jnp.exp(-jnp.abs(a)))


def _inproj(x2, g, w_main, w_f, b_f, tm, layer, depth, kbuf, vbuf):
    n = x2.shape[0]
    row = lambda w: pl.BlockSpec((tm, w), lambda i: (i, 0))
    const = lambda shape: pl.BlockSpec(shape, lambda i: (0,) * len(shape))
    stacked = pl.BlockSpec((1, tm, W_MIX), lambda i: (layer, i, 0))
    f32 = jax.ShapeDtypeStruct((n, W_MIX), F32)
    b16 = jax.ShapeDtypeStruct((n, W_MIX), BF16)
    big = jax.ShapeDtypeStruct((depth, n, W_MIX), F32)
    head_sel = (jnp.arange(W_MIX)[:, None] // FOX_HEAD_DIM == jnp.arange(128)[None, :]).astype(BF16)
    in_specs = [row(D_MODEL), const((1, D_MODEL)), const((D_MODEL, 5 * W_MIX)),
                const((D_MODEL, 128)), const((1, FOX_HEADS)), const((W_MIX, 128))]
    args = [x2, g, w_main, w_f, b_f, head_sel]
    aliases = {}
    if kbuf is not None:
        in_specs += [pl.BlockSpec(memory_space=pl.ANY)] * 2
        args += [kbuf, vbuf]
        aliases = {6: 3, 7: 4}
    return pl.pallas_call(
        _inproj_kernel,
        grid=(n // tm,),
        in_specs=in_specs,
        out_specs=[row(W_MIX), pl.BlockSpec((LANE_BLOCKS, tm, 128), lambda i: (0, i, 0)), row(W_MIX),
                   stacked, stacked, row(W_MIX), row(W_MIX), row(FOX_HEADS),
                   pl.BlockSpec((1, 8, 128), lambda i: (i, 0, 0))],
        out_shape=[f32, jax.ShapeDtypeStruct((LANE_BLOCKS, n, 128), F32), b16, big, big, b16, b16,
                   jax.ShapeDtypeStruct((n, FOX_HEADS), F32),
                   jax.ShapeDtypeStruct((n // tm, 8, 128), F32)],
        input_output_aliases=aliases,
        compiler_params=_params("parallel"),
        name="in_proj",
    )(*args)


def _pool_kernel(u_ref, hist_ref, w_ref, sc_ref, y_ref, pn_ref, buf, *, tm, pos0):
    t = pl.program_id(1)

    @pl.when(t == 0)
    def _():
        buf[0:1, :] = jnp.zeros((1, W_MIX), F32)
        buf[1:16, :] = hist_ref[0]

    @pl.when(t > 0)
    def _():
        buf[0:16, :] = buf[tm:tm + 16, :]

    buf[16:16 + tm, :] = u_ref[0]
    pos = pos0 + t * tm + lax.broadcasted_iota(jnp.int32, (tm, 1), 0)
    for g, w in enumerate(POOL_WINDOWS):
        lo, hi = g * POOL_GROUP_W, (g + 1) * POOL_GROUP_W
        full = buf[:, lo:hi]
        s = full
        k = 1
        while k < w:
            s = s + pltpu.roll(s, k, axis=0)
            k *= 2
        cnt = jnp.minimum(pos + 1, w).astype(F32)
        d = s[16:] * (1.0 / cnt) - full[16:]
        y = jnp.dot(d.astype(BF16), w_ref[g], preferred_element_type=F32) * sc_ref[:, lo:hi]
        y_ref[0, :, lo:hi] = y.astype(BF16)
    pn_ref[0] = buf[tm + 1:tm + 16, :]


def _pool(u, hist, w_pool, scale, pos0, tm):
    nb, L, _ = u.shape
    return pl.pallas_call(
        functools.partial(_pool_kernel, tm=tm, pos0=pos0),
        grid=(nb, L // tm),
        in_specs=[
            pl.BlockSpec((1, tm, W_MIX), lambda b, t: (b, t, 0)),
            pl.BlockSpec((1, POOL_HIST, W_MIX), lambda b, t: (b, 0, 0)),
            pl.BlockSpec((4, POOL_GROUP_W, POOL_GROUP_W), lambda b, t: (0, 0, 0)),
            pl.BlockSpec((1, W_MIX), lambda b, t: (0, 0)),
        ],
        out_specs=[
            pl.BlockSpec((1, tm, W_MIX), lambda b, t: (b, t, 0)),
            pl.BlockSpec((1, POOL_HIST, W_MIX), lambda b, t: (b, 0, 0)),
        ],
        out_shape=[jax.ShapeDtypeStruct((nb, L, W_MIX), BF16),
                   jax.ShapeDtypeStruct((nb, POOL_HIST, W_MIX), F32)],
        scratch_shapes=[pltpu.VMEM((tm + 16, W_MIX), F32)],
        compiler_params=_params("parallel", "arbitrary"),
        name="pool_mixer",
    )(u, hist, w_pool, scale)


def _ssm_operators(lam_re, lam_im, log_dt, b_re, b_im, c_re, c_im):
    T = SSM_CHUNK
    dt = jnp.exp(log_dt)[:, None]
    ar, ai = lam_re * dt, lam_im * dt
    taus = jnp.arange(T + 1, dtype=F32)[:, None, None]
    mag = jnp.exp(taus * ar)
    pr, pi = mag * jnp.cos(taus * ai), mag * jnp.sin(taus * ai)
    lbr, lbi = pr[1], pi[1]
    den = lam_re * lam_re + lam_im * lam_im
    fr = ((lbr - 1.0) * lam_re + lbi * lam_im) / den
    fi = (lbi * lam_re - (lbr - 1.0) * lam_im) / den
    bbr = fr[..., None] * b_re - fi[..., None] * b_im
    bbi = fr[..., None] * b_im + fi[..., None] * b_re
    er = pr[:T, :, :, None] * bbr - pi[:T, :, :, None] * bbi
    ei = pr[:T, :, :, None] * bbi + pi[:T, :, :, None] * bbr
    kt = (jnp.einsum('gcp,tgpd->tgcd', c_re, er, precision=HIGHEST)
          - jnp.einsum('gcp,tgpd->tgcd', c_im, ei, precision=HIGHEST))
    jj = jnp.arange(T)[:, None]
    tt = jnp.arange(T)[None, :]
    lag = tt - jj
    toe = jnp.where((lag >= 0)[:, :, None, None, None], kt[jnp.clip(lag, 0, T - 1)], 0.0)
    m1 = jnp.transpose(toe, (2, 0, 4, 1, 3)).reshape(SSM_GROUPS, SSM_CHUNK_W, SSM_CHUNK_W)
    w1r = jnp.transpose(er[::-1], (1, 0, 3, 2)).reshape(SSM_GROUPS, SSM_CHUNK_W, SSM_STATE)
    w1i = jnp.transpose(ei[::-1], (1, 0, 3, 2)).reshape(SSM_GROUPS, SSM_CHUNK_W, SSM_STATE)
    mw = jnp.concatenate([m1, w1r, w1i], axis=-1)
    qr, qi = pr[1:], pi[1:]
    m2r = c_re[None] * qr[:, :, None, :] - c_im[None] * qi[:, :, None, :]
    m2i = -(c_re[None] * qi[:, :, None, :] + c_im[None] * qr[:, :, None, :])
    m2 = jnp.concatenate([jnp.transpose(m2r, (1, 3, 0, 2)), jnp.transpose(m2i, (1, 3, 0, 2))],
                         axis=1).reshape(SSM_GROUPS, 2 * SSM_STATE, SSM_CHUNK_W)
    a_vec = jnp.concatenate([pr[T], pr[T]], axis=-1)
    b_vec = jnp.concatenate([-pi[T], pi[T]], axis=-1)
    return mw, m2, a_vec, b_vec


def _ssm_intra_kernel(u_ref, mw_ref, yi_ref, v_ref):
    r = jnp.dot(u_ref[0, 0], mw_ref[0], precision=HIGHEST, preferred_element_type=F32)
    yi_ref[0, 0] = r[:, :SSM_CHUNK_W]
    v_ref[0, 0] = r[:, SSM_CHUNK_W:]


def _ssm_intra(uc, mw):
    nb, G, nc, _ = uc.shape
    blk = lambda w: pl.BlockSpec((1, 1, nc, w), lambda b, g: (b, g, 0, 0))
    return pl.pallas_call(
        _ssm_intra_kernel,
        grid=(nb, G),
        in_specs=[blk(SSM_CHUNK_W),
                  pl.BlockSpec((1, SSM_CHUNK_W, SSM_CHUNK_W + 2 * SSM_STATE), lambda b, g: (g, 0, 0))],
        out_specs=[blk(SSM_CHUNK_W), blk(2 * SSM_STATE)],
        out_shape=[jax.ShapeDtypeStruct((nb, G, nc, SSM_CHUNK_W), F32),
                   jax.ShapeDtypeStruct((nb, G, nc, 2 * SSM_STATE), F32)],
        compiler_params=_params("parallel", "parallel"),
        name="ssm_intra",
    )(uc, mw)


def _ssm_scan_kernel(v_ref, s0_ref, a_ref, b_ref, sin_ref, sl_ref, *, nc):
    a = a_ref[...]
    bv = b_ref[...]

    def body(c, s):
        sin_ref[0, c] = s
        return a * s + bv * pltpu.roll(s, SSM_STATE, axis=1) + v_ref[0, c]

    sl_ref[0] = lax.fori_loop(0, nc, body, s0_ref[0])


def _ssm_scan(vt, s0, a_vec, b_vec):
    nb, nc, G, _ = vt.shape
    return pl.pallas_call(
        functools.partial(_ssm_scan_kernel, nc=nc),
        grid=(nb,),
        in_specs=[pl.BlockSpec((1, nc, G, 128), lambda b: (b, 0, 0, 0)),
                  pl.BlockSpec((1, G, 128), lambda b: (b, 0, 0)),
                  pl.BlockSpec((G, 128), lambda b: (0, 0)),
                  pl.BlockSpec((G, 128), lambda b: (0, 0))],
        out_specs=[pl.BlockSpec((1, nc, G, 128), lambda b: (b, 0, 0, 0)),
                   pl.BlockSpec((1, G, 128), lambda b: (b, 0, 0))],
        out_shape=[jax.ShapeDtypeStruct((nb, nc, G, 128), F32),
                   jax.ShapeDtypeStruct((nb, G, 128), F32)],
        compiler_params=_params("parallel"),
        name="ssm_scan",
    )(vt, s0, a_vec, b_vec)


def _ssm_inter_kernel(s_ref, m2_ref, yi_ref, y_ref):
    y_ref[0, 0] = yi_ref[0, 0] + jnp.dot(s_ref[0, 0], m2_ref[0], precision=HIGHEST,
                                         preferred_element_type=F32)


def _ssm_inter(sg, m2, yi):
    nb, G, nc, _ = sg.shape
    blk = lambda w: pl.BlockSpec((1, 1, nc, w), lambda b, g: (b, g, 0, 0))
    return pl.pallas_call(
        _ssm_inter_kernel,
        grid=(nb, G),
        in_specs=[blk(2 * SSM_STATE),
                  pl.BlockSpec((1, 2 * SSM_STATE, SSM_CHUNK_W), lambda b, g: (g, 0, 0)),
                  blk(SSM_CHUNK_W)],
        out_specs=blk(SSM_CHUNK_W),
        out_shape=jax.ShapeDtypeStruct((nb, G, nc, SSM_CHUNK_W), F32),
        compiler_params=_params("parallel", "parallel"),
        name="ssm_inter",
    )(sg, m2, yi)


def _glu_kernel(y_ref, u_ref, d_ref, w_ref, b_ref, o_ref):
    y = y_ref[...] + d_ref[...] * u_ref[...]
    y = 0.5 * y * (1.0 + jnp.tanh(math.sqrt(2.0 / math.pi) * (y + 0.044715 * (y * y * y))))
    z = jnp.dot(y.astype(BF16), w_ref[...], preferred_element_type=F32) + b_ref[...]
    o_ref[...] = (y * _sigmoid(z)).astype(BF16)


def _glu(y2, u2, d_skip, w_glu, b_glu, tm):
    n = y2.shape[0]
    row = pl.BlockSpec((tm, W_MIX), lambda i: (i, 0))
    vec = pl.BlockSpec((1, W_MIX), lambda i: (0, 0))
    return pl.pallas_call(
        _glu_kernel,
        grid=(n // tm,),
        in_specs=[row, row, vec, pl.BlockSpec((W_MIX, W_MIX), lambda i: (0, 0)), vec],
        out_specs=row,
        out_shape=jax.ShapeDtypeStruct((n, W_MIX), BF16),
        compiler_params=_params("parallel"),
        name="ssm_glu",
    )(y2, u2, d_skip, w_glu, b_glu)


SSM_HALF_CH = 256
SSM_HALF_ST = 1024
SSM_ALL_ST = 2048


def _ssm_blockdiag(lam_re, lam_im, log_dt, b_re, b_im, c_re, c_im):
    dt = jnp.exp(log_dt)[:, None]
    ar, ai = lam_re * dt, lam_im * dt

    def power(n):
        mag = jnp.exp(n * ar)
        return mag * jnp.cos(n * ai), mag * jnp.sin(n * ai)

    lbr, lbi = power(1.0)
    a16r, a16i = power(float(SSM_CHUNK))
    den = lam_re * lam_re + lam_im * lam_im
    fr = ((lbr - 1.0) * lam_re + lbi * lam_im) / den
    fi = (lbi * lam_re - (lbr - 1.0) * lam_im) / den
    bbr = fr[..., None] * b_re - fi[..., None] * b_im
    bbi = fr[..., None] * b_im + fi[..., None] * b_re
    eye = jnp.eye(SSM_GROUPS, dtype=F32)

    def bd_in(m):
        return jnp.einsum('gpc,gh->gchp', m, eye).reshape(W_MIX, SSM_ALL_ST)

    def bd_out(m):
        return jnp.einsum('gcp,gh->gphc', m, eye).reshape(SSM_ALL_ST, W_MIX)

    def halves(m, rw, cw):
        return jnp.stack([m[h * rw:(h + 1) * rw, h * cw:(h + 1) * cw] for h in range(2)])

    bdb = jnp.concatenate([halves(bd_in(bbr), SSM_HALF_CH, SSM_HALF_ST),
                           halves(bd_in(bbi), SSM_HALF_CH, SSM_HALF_ST)], axis=2).astype(BF16)
    bdc = jnp.concatenate([halves(bd_out(c_re), SSM_HALF_ST, SSM_HALF_CH),
                           halves(bd_out(-c_im), SSM_HALF_ST, SSM_HALF_CH)], axis=1).astype(BF16)
    flat = lambda m: m.reshape(1, SSM_ALL_ST)
    return dict(bdb=bdb, bdc=bdc, lam_r=flat(lbr), lam_i=flat(lbi), a16r=flat(a16r), a16i=flat(a16i))


def _chunk_rows(ref, t):
    return pl.ds(t, ref.shape[1] // SSM_CHUNK, stride=SSM_CHUNK)


def _ssm_step(u_ref, bdb_ref, lr, li, t, h, s):
    u_t = _load_lane_blocks(u_ref, _chunk_rows(u_ref, t), 2 * h, 2 * h + 2)
    bu = jnp.dot(u_t.astype(BF16), bdb_ref[h], preferred_element_type=F32)
    bur, bui = bu[:, :SSM_HALF_ST], bu[:, SSM_HALF_ST:]
    if s is None:
        return bur, bui
    sr, si = s
    return lr * sr - li * si + bur, lr * si + li * sr + bui


def _ssm_state_kernel(u_ref, bdb_ref, lr_ref, li_ref, vr_ref, vi_ref):
    for h in range(2):
        st = slice(h * SSM_HALF_ST, (h + 1) * SSM_HALF_ST)
        lr, li = lr_ref[:, st], li_ref[:, st]
        s = None
        for t in range(SSM_CHUNK):
            s = _ssm_step(u_ref, bdb_ref, lr, li, t, h, s)
        vr_ref[:, st] = s[0]
        vi_ref[:, st] = s[1]


def _ssm_state(u, sp, tc):
    r = u.shape[1] // SSM_CHUNK
    row = lambda w: pl.BlockSpec((tc, w), lambda i: (i, 0))
    const = lambda shape: pl.BlockSpec(shape, lambda i: (0,) * len(shape))
    out = jax.ShapeDtypeStruct((r, SSM_ALL_ST), F32)
    u2 = u
    return pl.pallas_call(
        _ssm_state_kernel,
        grid=(r // tc,),
        in_specs=[pl.BlockSpec((LANE_BLOCKS, tc * SSM_CHUNK, 128), lambda i: (0, i, 0)),
                  const((2, SSM_HALF_CH, 2 * SSM_HALF_ST)),
                  const((1, SSM_ALL_ST)), const((1, SSM_ALL_ST))],
        out_specs=[row(SSM_ALL_ST), row(SSM_ALL_ST)],
        out_shape=[out, out],
        compiler_params=_params("parallel"),
        name="ssm_state",
    )(u2, sp["bdb"], sp["lam_r"], sp["lam_i"])


def _ssm_carry_kernel(vr_ref, vi_ref, s0r_ref, s0i_ref, ar_ref, ai_ref,
                      sr_ref, si_ref, lr_ref, li_ref, *, nc):
    ar, ai = ar_ref[...], ai_ref[...]

    def body(c, s):
        sr, si = s
        sr_ref[0, pl.ds(c, 1), :] = sr
        si_ref[0, pl.ds(c, 1), :] = si
        return (ar * sr - ai * si + vr_ref[0, pl.ds(c, 1), :],
                ar * si + ai * sr + vi_ref[0, pl.ds(c, 1), :])

    sr, si = lax.fori_loop(0, nc, body, (s0r_ref[0], s0i_ref[0]))
    lr_ref[0] = sr
    li_ref[0] = si


def _ssm_carry(vr, vi, s0r, s0i, sp):
    nb, nc, _ = vr.shape
    seq = pl.BlockSpec((1, nc, SSM_ALL_ST), lambda b: (b, 0, 0))
    one = pl.BlockSpec((1, 1, SSM_ALL_ST), lambda b: (b, 0, 0))
    vec = pl.BlockSpec((1, SSM_ALL_ST), lambda b: (0, 0))
    return pl.pallas_call(
        functools.partial(_ssm_carry_kernel, nc=nc),
        grid=(nb,),
        in_specs=[seq, seq, one, one, vec, vec],
        out_specs=[seq, seq, one, one],
        out_shape=[jax.ShapeDtypeStruct((nb, nc, SSM_ALL_ST), F32)] * 2
                  + [jax.ShapeDtypeStruct((nb, 1, SSM_ALL_ST), F32)] * 2,
        compiler_params=_params("parallel"),
        name="ssm_carry",
    )(vr, vi, s0r, s0i, sp["a16r"], sp["a16i"])


def _ssm_out_kernel(u_ref, sr_ref, si_ref, bdb_ref, bdc_ref, lr_ref, li_ref, d_ref, w_ref, b_ref,
                    o_ref):
    lam, s = [], []
    for h in range(2):
        st = slice(h * SSM_HALF_ST, (h + 1) * SSM_HALF_ST)
        lam.append((lr_ref[:, st], li_ref[:, st]))
        s.append((sr_ref[:, st], si_ref[:, st]))
    for t in range(SSM_CHUNK):
        ys = []
        for h in range(2):
            s[h] = _ssm_step(u_ref, bdb_ref, lam[h][0], lam[h][1], t, h, s[h])
            ys.append(jnp.dot(s[h][0].astype(BF16), bdc_ref[h, :SSM_HALF_ST, :],
                              preferred_element_type=F32)
                      + jnp.dot(s[h][1].astype(BF16), bdc_ref[h, SSM_HALF_ST:, :],
                                preferred_element_type=F32))
        y = jnp.concatenate(ys, axis=1) + d_ref[...] * _load_lane_blocks(u_ref, _chunk_rows(u_ref, t))
        y = 0.5 * y * (1.0 + jnp.tanh(math.sqrt(2.0 / math.pi) * (y + 0.044715 * (y * y * y))))
        z = jnp.dot(y.astype(BF16), w_ref[...], preferred_element_type=F32) + b_ref[...]
        _store_lane_blocks(o_ref, _chunk_rows(o_ref, t), y * _sigmoid(z))


def _ssm_out(u2, sr, si, sp, d_skip, w_glu, b_glu, tc):
    r = u2.shape[1] // SSM_CHUNK
    row = lambda w: pl.BlockSpec((tc, w), lambda i: (i, 0))
    tok = pl.BlockSpec((LANE_BLOCKS, tc * SSM_CHUNK, 128), lambda i: (0, i, 0))
    const = lambda shape: pl.BlockSpec(shape, lambda i: (0,) * len(shape))
    return pl.pallas_call(
        _ssm_out_kernel,
        grid=(r // tc,),
        in_specs=[tok, row(SSM_ALL_ST), row(SSM_ALL_ST),
                  const((2, SSM_HALF_CH, 2 * SSM_HALF_ST)), const((2, 2 * SSM_HALF_ST, SSM_HALF_CH)),
                  const((1, SSM_ALL_ST)), const((1, SSM_ALL_ST)),
                  const((1, W_MIX)), const((W_MIX, W_MIX)), const((1, W_MIX))],
        out_specs=tok,
        out_shape=jax.ShapeDtypeStruct(u2.shape, F32),
        compiler_params=_params("parallel"),
        name="ssm_out",
    )(u2, sr, si, sp["bdb"], sp["bdc"], sp["lam_r"], sp["lam_i"], d_skip, w_glu, b_glu)


def _cumsum_kernel(lf_ref, c_ref, carry, *, tc):
    @pl.when(pl.program_id(1) == 0)
    def _():
        carry[...] = jnp.zeros_like(carry)

    r = lax.broadcasted_iota(jnp.int32, (tc, tc), 0)
    c = lax.broadcasted_iota(jnp.int32, (tc, tc), 1)
    tri = (r >= c).astype(F32)
    cs = jnp.dot(tri, lf_ref[0], precision=HIGHEST, preferred_element_type=F32) + carry[...]
    c_ref[0] = cs
    carry[...] = cs[tc - 1:tc, :]


def _cumsum(lf, tc):
    nb, lk, _ = lf.shape
    blk = pl.BlockSpec((1, tc, FOX_HEADS), lambda b, t: (b, t, 0))
    return pl.pallas_call(
        functools.partial(_cumsum_kernel, tc=tc),
        grid=(nb, lk // tc),
        in_specs=[blk],
        out_specs=blk,
        out_shape=jax.ShapeDtypeStruct((nb, lk, FOX_HEADS), F32),
        scratch_shapes=[pltpu.VMEM((1, FOX_HEADS), F32)],
        compiler_params=_params("parallel", "arbitrary"),
        name="logf_cumsum",
    )(lf)


ATTN_ROWS = 16
LOG2E = math.log2(math.e)
DEAD_LOGIT = 120.0
NORM_MARGIN = 1.02


def _attn_kernel(qi_tab, kj_tab, kfetch_tab, live_tab, q_ref, k_ref, v_ref, cq_ref, ck_ref, o_ref,
                 s_scr, p_scr, m_scr, a_scr, cq_scr, acc_scr, *, tq, tk, q_off, npairs):
    pidx = pl.program_id(2)
    qi = qi_tab[pidx]
    kj = kj_tab[pidx]
    live = live_tab[(pl.program_id(0) * HEAD_PAIRS + pl.program_id(1)) * npairs + pidx]
    q_lo = q_off + qi * tq
    k_lo = kj * tk
    cols = [(c, min(c + 128, tk)) for c in range(0, tk, 128)]
    lane = lax.broadcasted_iota(jnp.int32, (1, 2 * FOX_HEAD_DIM), 1)

    @pl.when(kj == 0)
    def _():
        m_scr[...] = jnp.full_like(m_scr, NEG_INF)
        acc_scr[...] = jnp.zeros_like(acc_scr)
        cq = cq_ref[0, 0] * LOG2E
        for hh in range(2):
            cq_scr[hh] = jnp.broadcast_to(cq[:, hh:hh + 1], (tq, 128))

    q = q_ref[0]
    kb = k_ref[0].astype(BF16)
    v = v_ref[0]
    ck = ck_ref[0, 0] * LOG2E

    def softmax_rows(hh, masked):
        ckb = [jnp.broadcast_to(ck[hh:hh + 1, lo:hi], (ATTN_ROWS, hi - lo)) for lo, hi in cols]
        for r in range(0, tq, ATTN_ROWS):
            t = [s_scr[hh, r:r + ATTN_ROWS, lo:hi] - ckb[c] for c, (lo, hi) in enumerate(cols)]
            if masked:
                qpos = q_lo + r + lax.broadcasted_iota(jnp.int32, (ATTN_ROWS, 128), 0)
                kpos = k_lo + lax.broadcasted_iota(jnp.int32, (ATTN_ROWS, 128), 1)
                t = [jnp.where((kpos + lo <= qpos)[:, :hi - lo], t[c], NEG_INF)
                     for c, (lo, hi) in enumerate(cols)]
            full = [t[c] for c, (lo, hi) in enumerate(cols) if hi - lo == 128]
            tmax = full[0]
            for tc in full[1:]:
                tmax = jnp.maximum(tmax, tc)
            tmax = jnp.max(tmax, axis=1, keepdims=True)
            if len(full) < len(cols):
                tmax = jnp.maximum(tmax, jnp.max(t[-1], axis=1, keepdims=True))
            tmax = jnp.broadcast_to(tmax, (ATTN_ROWS, 128))
            cq2 = cq_scr[hh, r:r + ATTN_ROWS, :]
            m_old = m_scr[hh, r:r + ATTN_ROWS, :]
            m_new = jnp.maximum(m_old, tmax + cq2)
            m_scr[hh, r:r + ATTN_ROWS, :] = m_new
            a_scr[hh, r:r + ATTN_ROWS, :] = jnp.exp2(m_old - m_new)
            mt = m_new - cq2
            for c, (lo, hi) in enumerate(cols):
                p_scr[hh, r:r + ATTN_ROWS, lo:hi] = jnp.exp2(t[c] - mt[:, :hi - lo]).astype(BF16)

    def step(masked):
        in_heads = [(lane >= hh * FOX_HEAD_DIM) & (lane < (hh + 1) * FOX_HEAD_DIM) for hh in range(2)]
        for hh in range(2):
            qm = jnp.where(in_heads[hh], q, jnp.zeros_like(q))
            s_scr[hh] = lax.dot_general(qm, kb, (((1,), (1,)), ((), ())),
                                        preferred_element_type=F32)
        for hh in range(2):
            in_head = in_heads[hh]
            softmax_rows(hh, masked)
            vh = jnp.where(in_head, v, 1.0).astype(BF16)
            pv = jnp.dot(p_scr[hh], vh, preferred_element_type=F32)
            acc_scr[hh] = a_scr[hh] * acc_scr[hh] + pv

    needs_mask = k_lo + tk - 1 > q_lo

    @pl.when(needs_mask)
    def _():
        step(True)

    @pl.when(jnp.logical_not(needs_mask) & (live != 0))
    def _():
        step(False)

    @pl.when(kj == (q_lo + tq - 1) // tk)
    def _():
        o = []
        for hh in range(2):
            acc = acc_scr[hh]
            o.append(acc * (1.0 / pltpu.roll(acc, FOX_HEAD_DIM, axis=1)))
        o_ref[0] = jnp.where(lane < FOX_HEAD_DIM, o[0], o[1]).astype(BF16)


def _live_tiles(qk_stats, c_all, nb, nq, t):
    qmax = jnp.sqrt(qk_stats[:, 0, :FOX_HEADS]).reshape(nb, nq, FOX_HEADS) * NORM_MARGIN
    kmax = jnp.sqrt(qk_stats[:, 1, :FOX_HEADS]).reshape(nb, nq, FOX_HEADS) * NORM_MARGIN
    c_first = c_all[:, 0::t, :]
    c_last = c_all[:, t - 1::t, :]
    bound = (qmax[:, :, None, :] * (kmax[:, None, :, :] + kmax[:, :, None, :])
             + c_first[:, :, None, :] - c_last[:, None, :, :])
    dead = jnp.all((bound < -DEAD_LOGIT).reshape(nb, nq, nq, HEAD_PAIRS, 2), axis=-1)
    below = (jnp.arange(nq)[None, :] < jnp.arange(nq)[:, None])[None, :, :, None]
    return jnp.transpose(jnp.logical_not(dead & below), (0, 3, 1, 2))


def _attention(q, k_all, v_all, cq, ck, q_off, tq, tk, live=None):
    nb, lq, _ = q.shape
    nq = lq // tq
    nk = k_all.shape[1] // tk
    pairs = [(qi, kj) for qi in range(nq) for kj in range((q_off + qi * tq + tq - 1) // tk + 1)]
    npairs = len(pairs)
    qi_tab = jnp.asarray([p[0] for p in pairs], jnp.int32)
    kj_tab = jnp.asarray([p[1] for p in pairs], jnp.int32)
    if live is None:
        live = jnp.ones((nb, HEAD_PAIRS, nq, nk), jnp.bool_)
    kidx = jnp.where(live, jnp.arange(nk, dtype=jnp.int32), nk)
    kfetch = lax.cummin(kidx, axis=3, reverse=True)
    live_tab = live[:, :, qi_tab, kj_tab].astype(jnp.int32).reshape(-1)
    kfetch_tab = kfetch[:, :, qi_tab, kj_tab].reshape(-1)
    flat = lambda b, hp, i: (b * HEAD_PAIRS + hp) * npairs + i
    qmap = lambda b, hp, i, qt, kt, kf, lv: (b, qt[i], hp)
    kmap = lambda b, hp, i, qt, kt, kf, lv: (b, kf[flat(b, hp, i)], hp)
    return pl.pallas_call(
        functools.partial(_attn_kernel, tq=tq, tk=tk, q_off=q_off, npairs=npairs),
        grid_spec=pltpu.PrefetchScalarGridSpec(
            num_scalar_prefetch=4,
            grid=(nb, HEAD_PAIRS, npairs),
            in_specs=[
                pl.BlockSpec((1, tq, 128), qmap),
                pl.BlockSpec((1, tk, 128), kmap),
                pl.BlockSpec((1, tk, 128), kmap),
                pl.BlockSpec((1, 1, tq, 2), lambda b, hp, i, qt, kt, kf, lv: (b, hp, qt[i], 0)),
                pl.BlockSpec((1, 1, 2, tk),
                             lambda b, hp, i, qt, kt, kf, lv: (b, hp, 0, kf[flat(b, hp, i)])),
            ],
            out_specs=pl.BlockSpec((1, tq, 128), qmap),
            scratch_shapes=[pltpu.VMEM((2, tq, tk), F32), pltpu.VMEM((2, tq, tk), BF16),
                            pltpu.VMEM((2, tq, 128), F32), pltpu.VMEM((2, tq, 128), F32),
                            pltpu.VMEM((2, tq, 128), F32), pltpu.VMEM((2, tq, 128), F32)],
        ),
        out_shape=jax.ShapeDtypeStruct((nb, lq, W_MIX), BF16),
        compiler_params=_params("parallel", "parallel", "arbitrary"),
        name="fox_attention",
    )(qi_tab, kj_tab, kfetch_tab, live_tab, q, k_all, v_all, cq, ck)


def _merge_kernel(x_ref, g_ref, ya_ref, yb_ref, yc_ref, wg_ref, wb_ref, wo_ref, o_ref):
    x = x_ref[...]
    hb = _rms_bf16(x, g_ref[...])
    merged = None
    for b, y_ref in enumerate((ya_ref, yb_ref, yc_ref)):
        gate = _sigmoid(jnp.dot(hb, wg_ref[:, b * D_MODEL:(b + 1) * D_MODEL],
                                preferred_element_type=F32))
        y = _load_lane_blocks(y_ref, slice(None)) if b == 1 else y_ref[...]
        term = gate * jnp.dot(y.astype(BF16), wb_ref[b], preferred_element_type=F32)
        merged = term if merged is None else merged + term
    o_ref[...] = x + jnp.dot(merged.astype(BF16), wo_ref[...], preferred_element_type=F32)


def _merge(x2, g, ya, yb, yc, w_gate, w_branch, w_out, tm):
    n = x2.shape[0]
    row = lambda w: pl.BlockSpec((tm, w), lambda i: (i, 0))
    const = lambda shape: pl.BlockSpec(shape, lambda i: (0,) * len(shape))
    return pl.pallas_call(
        _merge_kernel,
        grid=(n // tm,),
        in_specs=[row(D_MODEL), const((1, D_MODEL)), row(W_MIX),
                  pl.BlockSpec((LANE_BLOCKS, tm, 128), lambda i: (0, i, 0)), row(W_MIX),
                  const((D_MODEL, 3 * D_MODEL)), const((3, W_MIX, D_MODEL)),
                  const((D_MODEL, D_MODEL))],
        out_specs=row(D_MODEL),
        out_shape=jax.ShapeDtypeStruct((n, D_MODEL), F32),
        compiler_params=_params("parallel"),
        name="branch_merge",
    )(x2, g, ya, yb, yc, w_gate, w_branch, w_out)


FFN_HALO = 16


FFN_SUB = 256


def _ffn_kernel(x_ref, xh_ref, g_ref, hist_ref, wug_ref, wuv_ref, cwg_ref, cwv_ref,
                cbg_ref, cbv_ref, wd_ref, o_ref, cn_ref, h_scr, hh_scr, acc_scr, act_scr,
                *, tm, tn, nj):
    t = pl.program_id(1)
    j = pl.program_id(2)

    @pl.when(j == 0)
    def _():
        h_scr[...] = _rms_bf16(x_ref[...], g_ref[...])
        hh_scr[...] = _rms_bf16(xh_ref[...], g_ref[...])

    first = t == 0
    for lo in range(0, tn, FFN_SUB):
        hi = min(lo + FFN_SUB, tn)
        conv = []
        for hf, (wu_ref, cw_ref, cb_ref) in enumerate(((wug_ref, cwg_ref, cbg_ref),
                                                       (wuv_ref, cwv_ref, cbv_ref))):
            w = wu_ref[:, lo:hi]
            up = jnp.dot(h_scr[...], w, preferred_element_type=F32)
            uph = jnp.dot(hh_scr[...], w, preferred_element_type=F32)
            prev = jnp.where(first, hist_ref[0, hf, :, lo:hi], uph[FFN_HALO - 8:, :])
            full = jnp.concatenate([prev, up], axis=0)
            cn_ref[0, hf, j, :, lo:hi] = up[tm - 2:, :]
            cw = cw_ref[:, lo:hi]
            conv.append(cb_ref[:, lo:hi] + full[6:6 + tm, :] * cw[0:1, :]
                        + full[7:7 + tm, :] * cw[1:2, :] + up * cw[2:3, :])
        gate, val = conv
        act_scr[:, lo:hi] = (gate * _sigmoid(gate) * val).astype(BF16)
    down = jnp.dot(act_scr[...], wd_ref[...], preferred_element_type=F32)

    if nj == 1:
        o_ref[...] = x_ref[...] + down
        return

    @pl.when(j == 0)
    def _():
        acc_scr[...] = down

    @pl.when((j > 0) & (j < nj - 1))
    def _():
        acc_scr[...] += down

    @pl.when(j == nj - 1)
    def _():
        o_ref[...] = x_ref[...] + acc_scr[...] + down


def _ffn(x2, g, hist, wug, wuv, cwg, cwv, cbg, cbv, wd, nb, L, tm, tn):
    n = x2.shape[0]
    nt, nj = L // tm, D_FF // tn
    hpt = tm // FFN_HALO
    row = lambda b, t, j: (b * nt + t, 0)
    col = lambda b, t, j: (0, j)
    return pl.pallas_call(
        functools.partial(_ffn_kernel, tm=tm, tn=tn, nj=nj),
        grid=(nb, nt, nj),
        in_specs=[
            pl.BlockSpec((tm, D_MODEL), row),
            pl.BlockSpec((FFN_HALO, D_MODEL), lambda b, t, j: (jnp.maximum((b * nt + t) * hpt - 1, 0), 0)),
            pl.BlockSpec((1, D_MODEL), lambda b, t, j: (0, 0)),
            pl.BlockSpec((1, 2, 8, tn), lambda b, t, j: (b, 0, 0, j)),
            pl.BlockSpec((D_MODEL, tn), col),
            pl.BlockSpec((D_MODEL, tn), col),
            pl.BlockSpec((3, tn), col),
            pl.BlockSpec((3, tn), col),
            pl.BlockSpec((1, tn), col),
            pl.BlockSpec((1, tn), col),
            pl.BlockSpec((tn, D_MODEL), lambda b, t, j: (j, 0)),
        ],
        out_specs=[pl.BlockSpec((tm, D_MODEL), row),
                   pl.BlockSpec((1, 2, nj, 2, tn), lambda b, t, j: (b, 0, 0, 0, 0))],
        out_shape=[jax.ShapeDtypeStruct((n, D_MODEL), F32),
                   jax.ShapeDtypeStruct((nb, 2, nj, 2, tn), F32)],
        scratch_shapes=[pltpu.VMEM((tm, D_MODEL), BF16), pltpu.VMEM((FFN_HALO, D_MODEL), BF16),
                        pltpu.VMEM((tm, D_MODEL), F32), pltpu.VMEM((tm, tn), BF16)],
        compiler_params=_params("arbitrary", "arbitrary", "arbitrary"),
        name="conv_ffn",
    )(x2, x2, g, hist, wug, wuv, cwg, cwv, cbg, cbv, wd)


def _final_norm_kernel(x_ref, g_ref, o_ref):
    x = x_ref[...]
    ms = jnp.mean(x * x, axis=-1, keepdims=True)
    o_ref[...] = x * lax.rsqrt(ms + EPS) * g_ref[...]


def _final_norm(x2, g, tm):
    n = x2.shape[0]
    row = pl.BlockSpec((tm, D_MODEL), lambda i: (i, 0))
    return pl.pallas_call(
        _final_norm_kernel,
        grid=(n // tm,),
        in_specs=[row, pl.BlockSpec((1, D_MODEL), lambda i: (0, 0))],
        out_specs=row,
        out_shape=jax.ShapeDtypeStruct((n, D_MODEL), F32),
        compiler_params=_params("parallel"),
        name="final_norm",
    )(x2, g)


def _layer(x2, nb, L, pos0, pool_hist, s0, k_past, v_past, logf_past, conv_hist, p,
           layer, depth, kbuf, vbuf):
    n = nb * L
    tm = _tile(L, 512)
    u_pool, u_ssm, q, kbuf, vbuf, kb, vb, logf, qk_stats = _inproj(
        x2, p["g_mix"], p["w_main"], p["w_f"], p["b_f"], tm, layer, depth, kbuf, vbuf)

    y_a, pool_new = _pool(u_pool.reshape(nb, L, W_MIX), pool_hist, p["w_pool"], p["pool_scale"],
                          pos0, tm)

    nc = L // SSM_CHUNK
    sp = p["ssm"]
    tc = _tile(nb * nc, 128)
    vr, vi = _ssm_state(u_ssm, sp, tc)
    sr, si, sl_re, sl_im = _ssm_carry(vr.reshape(nb, nc, SSM_ALL_ST), vi.reshape(nb, nc, SSM_ALL_ST),
                                      s0[0], s0[1], sp)
    y_b = _ssm_out(u_ssm, sr.reshape(nb * nc, SSM_ALL_ST), si.reshape(nb * nc, SSM_ALL_ST), sp,
                   p["d_skip"], p["w_glu"], p["b_glu"], tc)

    q3 = q.reshape(nb, L, W_MIX)
    logf3 = logf.reshape(nb, L, FOX_HEADS)
    if k_past is None:
        k_all, v_all, logf_all = kb.reshape(nb, L, W_MIX), vb.reshape(nb, L, W_MIX), logf3
    else:
        k_all = jnp.concatenate([k_past, kbuf[layer].reshape(nb, L, W_MIX)], axis=1)
        v_all = jnp.concatenate([v_past, vbuf[layer].reshape(nb, L, W_MIX)], axis=1)
        logf_all = jnp.concatenate([logf_past, logf3], axis=1)
    lk = k_all.shape[1]
    c_all = _cumsum(logf_all, _tile(lk, 512))
    cq = jnp.transpose(c_all[:, lk - L:].reshape(nb, L, HEAD_PAIRS, 2), (0, 2, 1, 3))
    ck = jnp.transpose(c_all.reshape(nb, lk, HEAD_PAIRS, 2), (0, 2, 3, 1))
    tk = 512 if lk % 512 == 0 else lk
    live = _live_tiles(qk_stats, c_all, nb, L // tm, tm) if (k_past is None and tk == tm) else None
    y_c = _attention(q3, k_all, v_all, cq, ck, pos0, tm, tk, live)

    x2 = _merge(x2, p["g_mix"], y_a.reshape(n, W_MIX), y_b, y_c.reshape(n, W_MIX),
                p["w_gate"], p["w_branch"], p["w_out"], tm)
    x2, conv_new = _ffn(x2, p["g_ffn"], conv_hist, p["wug"], p["wuv"], p["cwg"], p["cwv"],
                        p["cbg"], p["cbv"], p["w_down"], nb, L, tm, D_FF // 2)
    states = (pool_new,
              sl_re.reshape(nb, SSM_GROUPS, SSM_STATE), sl_im.reshape(nb, SSM_GROUPS, SSM_STATE),
              logf3,
              jnp.transpose(conv_new, (0, 3, 1, 2, 4)).reshape(nb, 2, 2 * D_FF))
    return x2, states, kbuf, vbuf


def kernel(x_prompt, x_sample, cache_pool, state_ssm_re, state_ssm_im, cache_k, cache_v, cache_logf, cache_ffn_conv, norm_mix_g, w_in, b_f, w_pool, pool_scale, lam_re, lam_im, log_dt, b_re, b_im, c_re, c_im, d_skip, w_glu, b_glu, w_branch, w_out, norm_ffn_g, w_up, conv_w, conv_b, w_down, norm_final_g):
    bp, lp, _ = x_prompt.shape
    bs, ls, _ = x_sample.shape
    depth = w_in.shape[0]
    past = cache_k.shape[2]
    col_f = 5 * W_MIX
    col_gate = col_f + FOX_HEADS

    hp = x_prompt.reshape(bp * lp, D_MODEL)
    hs = x_sample.reshape(bs * ls, D_MODEL)
    zero_pool = jnp.zeros((bp, POOL_HIST, W_MIX), F32)
    zero_s = (jnp.zeros((bp, 1, SSM_ALL_ST), F32),) * 2
    zero_conv = jnp.zeros((bp, 2, 8, D_FF), F32)

    st_p, st_s = [], []
    kp = vp = ks = vs = None
    for l in range(depth):
        p = dict(
            ssm=_ssm_blockdiag(lam_re[l], lam_im[l], log_dt[l], b_re[l], b_im[l], c_re[l], c_im[l]),
            g_mix=norm_mix_g[l][None], g_ffn=norm_ffn_g[l][None],
            w_main=w_in[l][:, :col_f].astype(BF16),
            w_f=jnp.pad(w_in[l][:, col_f:col_gate], ((0, 0), (0, 128 - FOX_HEADS))).astype(BF16),
            b_f=b_f[l][None],
            w_gate=w_in[l][:, col_gate:].astype(BF16),
            w_pool=w_pool[l].astype(BF16), pool_scale=pool_scale[l][None],
            d_skip=d_skip[l][None], w_glu=w_glu[l].astype(BF16), b_glu=b_glu[l][None],
            w_branch=w_branch[l].astype(BF16), w_out=w_out[l].astype(BF16),
            wug=w_up[l][:, :D_FF].astype(BF16), wuv=w_up[l][:, D_FF:].astype(BF16),
            cwg=conv_w[l][:, :D_FF], cwv=conv_w[l][:, D_FF:],
            cbg=conv_b[l][None, :D_FF], cbv=conv_b[l][None, D_FF:],
            w_down=w_down[l].astype(BF16),
        )
        hp, sp, kp, vp = _layer(hp, bp, lp, 0, zero_pool, zero_s, None, None, None, zero_conv, p,
                                l, depth, kp, vp)
        s0 = (state_ssm_re[l].reshape(bs, 1, SSM_ALL_ST), state_ssm_im[l].reshape(bs, 1, SSM_ALL_ST))
        conv_hist = jnp.pad(jnp.transpose(cache_ffn_conv[l].reshape(bs, 2, 2, D_FF), (0, 2, 1, 3)),
                            ((0, 0), (0, 0), (6, 0), (0, 0)))
        hs, ss, ks, vs = _layer(hs, bs, ls, past, cache_pool[l], s0,
                                cache_k[l].reshape(bs, past, W_MIX), cache_v[l].reshape(bs, past, W_MIX),
                                cache_logf[l], conv_hist, p, l, depth, ks, vs)
        st_p.append(sp)
        st_s.append(ss)

    y_prompt = _final_norm(hp, norm_final_g[None], _tile(bp * lp, 512)).reshape(bp, lp, D_MODEL)
    y_sample = _final_norm(hs, norm_final_g[None], _tile(bs * ls, 512)).reshape(bs, ls, D_MODEL)
    stack = lambda st, i: jnp.stack([s[i] for s in st], axis=0)
    heads_p = (depth, bp, lp, FOX_HEADS, FOX_HEAD_DIM)
    heads_s = (depth, bs, ls, FOX_HEADS, FOX_HEAD_DIM)
    return (y_prompt, y_sample,
            stack(st_p, 0), stack(st_s, 0), stack(st_p, 1), stack(st_s, 1),
            stack(st_p, 2), stack(st_s, 2),
            kp.reshape(heads_p), ks.reshape(heads_s), vp.reshape(heads_p), vs.reshape(heads_s),
            stack(st_p, 3), stack(st_s, 3), stack(st_p, 4), stack(st_s, 4))
```

```python
import functools
import math

import jax
import jax.numpy as jnp
import numpy as np
from jax import lax
from jax.experimental import pallas as pl
from jax.experimental.pallas import tpu as pltpu

F32 = jnp.float32
BF16 = jnp.bfloat16
HIGHEST = lax.Precision.HIGHEST

D_MODEL = 1024
W_MIX = 512
POOL_WINDOWS = (2, 4, 8, 16)
POOL_GROUP_W = 128
POOL_HIST = 15
SSM_GROUPS = 32
SSM_STATE = 64
SSM_CHUNK = 16
FOX_HEADS = 8
FOX_HEAD_DIM = 64
HEAD_PAIRS = FOX_HEADS // 2
D_FF = 2816
EPS = 1e-6
NEG_INF = -1e30
VMEM_LIMIT = 56 * 1024 * 1024


def _params(*sem):
    return pltpu.CompilerParams(dimension_semantics=sem, vmem_limit_bytes=VMEM_LIMIT)


def _tile(n, target, mult=8):
    if n <= target:
        return n
    for t in range(target, 0, -1):
        if n % t == 0 and t % mult == 0:
            return t
    return n


def _rms_bf16(x, g):
    ms = jnp.mean(x * x, axis=-1, keepdims=True)
    return (x * lax.rsqrt(ms + EPS) * g).astype(BF16)


def _sigmoid(x):
    return 1.0 / (1.0 + jnp.exp(-x))


LANE_BLOCKS = W_MIX // 128


def _load_lane_blocks(ref, rows, lo=0, hi=LANE_BLOCKS):
    return jnp.concatenate([ref[j, rows, :] for j in range(lo, hi)], axis=1)


def _store_lane_blocks(ref, rows, val):
    for j in range(LANE_BLOCKS):
        ref[j, rows, :] = val[:, j * 128:(j + 1) * 128]


def _inproj_kernel(*refs):
    x_ref, g_ref, w_ref, wf_ref, bf_ref, sel_ref = refs[:6]
    up_ref, us_ref, q_ref, k_ref, v_ref, kb_ref, vb_ref, lf_ref, st_ref = refs[-9:]
    hb = _rms_bf16(x_ref[...], g_ref[...])

    def proj(idx):
        return jnp.dot(hb, w_ref[:, idx * W_MIX:(idx + 1) * W_MIX], preferred_element_type=F32)

    def head_norm2_max(a):
        return jnp.max(jnp.dot((a * a).astype(BF16), sel_ref[...], preferred_element_type=F32),
                       axis=0, keepdims=True)

    up_ref[...] = proj(0)
    _store_lane_blocks(us_ref, slice(None), proj(1))
    qn = proj(2) * (FOX_HEAD_DIM ** -0.5)
    q_ref[...] = (qn * LOG2E).astype(BF16)
    k = proj(3)
    k_ref[0] = k
    kb_ref[...] = k.astype(BF16)
    row = lax.broadcasted_iota(jnp.int32, (8, 128), 0)
    st_ref[0] = jnp.where(row == 0, head_norm2_max(qn), jnp.where(row == 1, head_norm2_max(k), 0.0))
    v = proj(4)
    v_ref[0] = v
    vb_ref[...] = v.astype(BF16)
    zf = jnp.dot(hb, wf_ref[...], preferred_element_type=F32)
    a = zf[:, :FOX_HEADS] + bf_ref[...]
    lf_ref[...] = jnp.minimum(a, 0.0) - jnp.log(1.0 + jnp.exp(-jnp.abs(a)))


def _inproj(x2, g, w_main, w_f, b_f, tm, layer, depth, kbuf, vbuf):
    n = x2.shape[0]
    row = lambda w: pl.BlockSpec((tm, w), lambda i: (i, 0))
    const = lambda shape: pl.BlockSpec(shape, lambda i: (0,) * len(shape))
    stacked = pl.BlockSpec((1, tm, W_MIX), lambda i: (layer, i, 0))
    f32 = jax.ShapeDtypeStruct((n, W_MIX), F32)
    b16 = jax.ShapeDtypeStruct((n, W_MIX), BF16)
    big = jax.ShapeDtypeStruct((depth, n, W_MIX), F32)
    head_sel = (jnp.arange(W_MIX)[:, None] // FOX_HEAD_DIM == jnp.arange(128)[None, :]).astype(BF16)
    in_specs = [row(D_MODEL), const((1, D_MODEL)), const((D_MODEL, 5 * W_MIX)),
                const((D_MODEL, 128)), const((1, FOX_HEADS)), const((W_MIX, 128))]
    args = [x2, g, w_main, w_f, b_f, head_sel]
    aliases = {}
    if kbuf is not None:
        in_specs += [pl.BlockSpec(memory_space=pl.ANY)] * 2
        args += [kbuf, vbuf]
        aliases = {6: 3, 7: 4}
    return pl.pallas_call(
        _inproj_kernel,
        grid=(n // tm,),
        in_specs=in_specs,
        out_specs=[row(W_MIX), pl.BlockSpec((LANE_BLOCKS, tm, 128), lambda i: (0, i, 0)), row(W_MIX),
                   stacked, stacked, row(W_MIX), row(W_MIX), row(FOX_HEADS),
                   pl.BlockSpec((1, 8, 128), lambda i: (i, 0, 0))],
        out_shape=[f32, jax.ShapeDtypeStruct((LANE_BLOCKS, n, 128), F32), b16, big, big, b16, b16,
                   jax.ShapeDtypeStruct((n, FOX_HEADS), F32),
                   jax.ShapeDtypeStruct((n // tm, 8, 128), F32)],
        input_output_aliases=aliases,
        compiler_params=_params("parallel"),
        name="in_proj",
    )(*args)


def _pool_kernel(u_ref, hist_ref, w_ref, sc_ref, y_ref, pn_ref, buf, *, tm, pos0):
    t = pl.program_id(1)

    @pl.when(t == 0)
    def _():
        buf[0:1, :] = jnp.zeros((1, W_MIX), F32)
        buf[1:16, :] = hist_ref[0]

    @pl.when(t > 0)
    def _():
        buf[0:16, :] = buf[tm:tm + 16, :]

    buf[16:16 + tm, :] = u_ref[0]
    pos = pos0 + t * tm + lax.broadcasted_iota(jnp.int32, (tm, 1), 0)
    for g, w in enumerate(POOL_WINDOWS):
        lo, hi = g * POOL_GROUP_W, (g + 1) * POOL_GROUP_W
        full = buf[:, lo:hi]
        s = full
        k = 1
        while k < w:
            s = s + pltpu.roll(s, k, axis=0)
            k *= 2
        cnt = jnp.minimum(pos + 1, w).astype(F32)
        d = s[16:] * (1.0 / cnt) - full[16:]
        y = jnp.dot(d.astype(BF16), w_ref[g], preferred_element_type=F32) * sc_ref[:, lo:hi]
        y_ref[0, :, lo:hi] = y.astype(BF16)
    pn_ref[0] = buf[tm + 1:tm + 16, :]


def _pool(u, hist, w_pool, scale, pos0, tm):
    nb, L, _ = u.shape
    return pl.pallas_call(
        functools.partial(_pool_kernel, tm=tm, pos0=pos0),
        grid=(nb, L // tm),
        in_specs=[
            pl.BlockSpec((1, tm, W_MIX), lambda b, t: (b, t, 0)),
            pl.BlockSpec((1, POOL_HIST, W_MIX), lambda b, t: (b, 0, 0)),
            pl.BlockSpec((4, POOL_GROUP_W, POOL_GROUP_W), lambda b, t: (0, 0, 0)),
            pl.BlockSpec((1, W_MIX), lambda b, t: (0, 0)),
        ],
        out_specs=[
            pl.BlockSpec((1, tm, W_MIX), lambda b, t: (b, t, 0)),
            pl.BlockSpec((1, POOL_HIST, W_MIX), lambda b, t: (b, 0, 0)),
        ],
        out_shape=[jax.ShapeDtypeStruct((nb, L, W_MIX), BF16),
                   jax.ShapeDtypeStruct((nb, POOL_HIST, W_MIX), F32)],
        scratch_shapes=[pltpu.VMEM((tm + 16, W_MIX), F32)],
        compiler_params=_params("parallel", "arbitrary"),
        name="pool_mixer",
    )(u, hist, w_pool, scale)


SSM_HALF_CH = 256
SSM_HALF_ST = 1024
SSM_ALL_ST = 2048


def _ssm_blockdiag(lam_re, lam_im, log_dt, b_re, b_im, c_re, c_im):
    dt = jnp.exp(log_dt)[:, None]
    ar, ai = lam_re * dt, lam_im * dt

    def power(n):
        mag = jnp.exp(n * ar)
        return mag * jnp.cos(n * ai), mag * jnp.sin(n * ai)

    lbr, lbi = power(1.0)
    a16r, a16i = power(float(SSM_CHUNK))
    den = lam_re * lam_re + lam_im * lam_im
    fr = ((lbr - 1.0) * lam_re + lbi * lam_im) / den
    fi = (lbi * lam_re - (lbr - 1.0) * lam_im) / den
    bbr = fr[..., None] * b_re - fi[..., None] * b_im
    bbi = fr[..., None] * b_im + fi[..., None] * b_re
    eye = jnp.eye(SSM_GROUPS, dtype=F32)

    def bd_in(m):
        return jnp.einsum('gpc,gh->gchp', m, eye).reshape(W_MIX, SSM_ALL_ST)

    def bd_out(m):
        return jnp.einsum('gcp,gh->gphc', m, eye).reshape(SSM_ALL_ST, W_MIX)

    def halves(m, rw, cw):
        return jnp.stack([m[h * rw:(h + 1) * rw, h * cw:(h + 1) * cw] for h in range(2)])

    bdb = jnp.concatenate([halves(bd_in(bbr), SSM_HALF_CH, SSM_HALF_ST),
                           halves(bd_in(bbi), SSM_HALF_CH, SSM_HALF_ST)], axis=2).astype(BF16)
    bdc = jnp.concatenate([halves(bd_out(c_re), SSM_HALF_ST, SSM_HALF_CH),
                           halves(bd_out(-c_im), SSM_HALF_ST, SSM_HALF_CH)], axis=1).astype(BF16)
    flat = lambda m: m.reshape(1, SSM_ALL_ST)
    return dict(bdb=bdb, bdc=bdc, lam_r=flat(lbr), lam_i=flat(lbi), a16r=flat(a16r), a16i=flat(a16i))


def _chunk_rows(ref, t):
    return pl.ds(t, ref.shape[1] // SSM_CHUNK, stride=SSM_CHUNK)


def _ssm_step(u_ref, bdb_ref, lr, li, t, h, s):
    u_t = _load_lane_blocks(u_ref, _chunk_rows(u_ref, t), 2 * h, 2 * h + 2)
    bu = jnp.dot(u_t.astype(BF16), bdb_ref[h], preferred_element_type=F32)
    bur, bui = bu[:, :SSM_HALF_ST], bu[:, SSM_HALF_ST:]
    if s is None:
        return bur, bui
    sr, si = s
    return lr * sr - li * si + bur, lr * si + li * sr + bui


def _ssm_state_kernel(u_ref, bdb_ref, lr_ref, li_ref, vr_ref, vi_ref):
    for h in range(2):
        st = slice(h * SSM_HALF_ST, (h + 1) * SSM_HALF_ST)
        lr, li = lr_ref[:, st], li_ref[:, st]
        s = None
        for t in range(SSM_CHUNK):
            s = _ssm_step(u_ref, bdb_ref, lr, li, t, h, s)
        vr_ref[:, st] = s[0]
        vi_ref[:, st] = s[1]


def _ssm_state(u, sp, tc):
    r = u.shape[1] // SSM_CHUNK
    row = lambda w: pl.BlockSpec((tc, w), lambda i: (i, 0))
    const = lambda shape: pl.BlockSpec(shape, lambda i: (0,) * len(shape))
    out = jax.ShapeDtypeStruct((r, SSM_ALL_ST), F32)
    u2 = u
    return pl.pallas_call(
        _ssm_state_kernel,
        grid=(r // tc,),
        in_specs=[pl.BlockSpec((LANE_BLOCKS, tc * SSM_CHUNK, 128), lambda i: (0, i, 0)),
                  const((2, SSM_HALF_CH, 2 * SSM_HALF_ST)),
                  const((1, SSM_ALL_ST)), const((1, SSM_ALL_ST))],
        out_specs=[row(SSM_ALL_ST), row(SSM_ALL_ST)],
        out_shape=[out, out],
        compiler_params=_params("parallel"),
        name="ssm_state",
    )(u2, sp["bdb"], sp["lam_r"], sp["lam_i"])


def _ssm_carry_kernel(vr_ref, vi_ref, s0r_ref, s0i_ref, ar_ref, ai_ref,
                      sr_ref, si_ref, lr_ref, li_ref, *, nc):
    ar, ai = ar_ref[...], ai_ref[...]

    def body(c, s):
        sr, si = s
        sr_ref[0, pl.ds(c, 1), :] = sr
        si_ref[0, pl.ds(c, 1), :] = si
        return (ar * sr - ai * si + vr_ref[0, pl.ds(c, 1), :],
                ar * si + ai * sr + vi_ref[0, pl.ds(c, 1), :])

    sr, si = lax.fori_loop(0, nc, body, (s0r_ref[0], s0i_ref[0]))
    lr_ref[0] = sr
    li_ref[0] = si


def _ssm_carry(vr, vi, s0r, s0i, sp):
    nb, nc, _ = vr.shape
    seq = pl.BlockSpec((1, nc, SSM_ALL_ST), lambda b: (b, 0, 0))
    one = pl.BlockSpec((1, 1, SSM_ALL_ST), lambda b: (b, 0, 0))
    vec = pl.BlockSpec((1, SSM_ALL_ST), lambda b: (0, 0))
    return pl.pallas_call(
        functools.partial(_ssm_carry_kernel, nc=nc),
        grid=(nb,),
        in_specs=[seq, seq, one, one, vec, vec],
        out_specs=[seq, seq, one, one],
        out_shape=[jax.ShapeDtypeStruct((nb, nc, SSM_ALL_ST), F32)] * 2
                  + [jax.ShapeDtypeStruct((nb, 1, SSM_ALL_ST), F32)] * 2,
        compiler_params=_params("parallel"),
        name="ssm_carry",
    )(vr, vi, s0r, s0i, sp["a16r"], sp["a16i"])


def _ssm_out_kernel(u_ref, sr_ref, si_ref, bdb_ref, bdc_ref, lr_ref, li_ref, d_ref, w_ref, b_ref,
                    o_ref):
    lam, s = [], []
    for h in range(2):
        st = slice(h * SSM_HALF_ST, (h + 1) * SSM_HALF_ST)
        lam.append((lr_ref[:, st], li_ref[:, st]))
        s.append((sr_ref[:, st], si_ref[:, st]))
    for t in range(SSM_CHUNK):
        ys = []
        for h in range(2):
            s[h] = _ssm_step(u_ref, bdb_ref, lam[h][0], lam[h][1], t, h, s[h])
            ys.append(jnp.dot(s[h][0].astype(BF16), bdc_ref[h, :SSM_HALF_ST, :],
                              preferred_element_type=F32)
                      + jnp.dot(s[h][1].astype(BF16), bdc_ref[h, SSM_HALF_ST:, :],
                                preferred_element_type=F32))
        y = jnp.concatenate(ys, axis=1) + d_ref[...] * _load_lane_blocks(u_ref, _chunk_rows(u_ref, t))
        y = 0.5 * y * (1.0 + jnp.tanh(math.sqrt(2.0 / math.pi) * (y + 0.044715 * (y * y * y))))
        z = jnp.dot(y.astype(BF16), w_ref[...], preferred_element_type=F32) + b_ref[...]
        _store_lane_blocks(o_ref, _chunk_rows(o_ref, t), y * _sigmoid(z))


def _ssm_out(u2, sr, si, sp, d_skip, w_glu, b_glu, tc):
    r = u2.shape[1] // SSM_CHUNK
    row = lambda w: pl.BlockSpec((tc, w), lambda i: (i, 0))
    tok = pl.BlockSpec((LANE_BLOCKS, tc * SSM_CHUNK, 128), lambda i: (0, i, 0))
    const = lambda shape: pl.BlockSpec(shape, lambda i: (0,) * len(shape))
    return pl.pallas_call(
        _ssm_out_kernel,
        grid=(r // tc,),
        in_specs=[tok, row(SSM_ALL_ST), row(SSM_ALL_ST),
                  const((2, SSM_HALF_CH, 2 * SSM_HALF_ST)), const((2, 2 * SSM_HALF_ST, SSM_HALF_CH)),
                  const((1, SSM_ALL_ST)), const((1, SSM_ALL_ST)),
                  const((1, W_MIX)), const((W_MIX, W_MIX)), const((1, W_MIX))],
        out_specs=tok,
        out_shape=jax.ShapeDtypeStruct(u2.shape, F32),
        compiler_params=_params("parallel"),
        name="ssm_out",
    )(u2, sr, si, sp["bdb"], sp["bdc"], sp["lam_r"], sp["lam_i"], d_skip, w_glu, b_glu)


def _cumsum_kernel(lf_ref, c_ref, carry, *, tc):
    @pl.when(pl.program_id(1) == 0)
    def _():
        carry[...] = jnp.zeros_like(carry)

    r = lax.broadcasted_iota(jnp.int32, (tc, tc), 0)
    c = lax.broadcasted_iota(jnp.int32, (tc, tc), 1)
    tri = (r >= c).astype(F32)
    cs = jnp.dot(tri, lf_ref[0], precision=HIGHEST, preferred_element_type=F32) + carry[...]
    c_ref[0] = cs
    carry[...] = cs[tc - 1:tc, :]


def _cumsum(lf, tc):
    nb, lk, _ = lf.shape
    blk = pl.BlockSpec((1, tc, FOX_HEADS), lambda b, t: (b, t, 0))
    return pl.pallas_call(
        functools.partial(_cumsum_kernel, tc=tc),
        grid=(nb, lk // tc),
        in_specs=[blk],
        out_specs=blk,
        out_shape=jax.ShapeDtypeStruct((nb, lk, FOX_HEADS), F32),
        scratch_shapes=[pltpu.VMEM((1, FOX_HEADS), F32)],
        compiler_params=_params("parallel", "arbitrary"),
        name="logf_cumsum",
    )(lf)


ATTN_ROWS = 16
LOG2E = math.log2(math.e)
DEAD_LOGIT = 120.0
NORM_MARGIN = 1.02


def _attn_kernel(b_tab, hp_tab, qi_tab, kj_tab, first_tab, q_ref, k_ref, v_ref, cq_ref, ck_ref,
                 o_ref, s_scr, p_scr, m_scr, a_scr, cq_scr, acc_scr, *, tq, tk, q_off):
    step_id = pl.program_id(0)
    qi = qi_tab[step_id]
    kj = kj_tab[step_id]
    q_lo = q_off + qi * tq
    k_lo = kj * tk
    cols = [(c, min(c + 128, tk)) for c in range(0, tk, 128)]
    lane = lax.broadcasted_iota(jnp.int32, (1, 2 * FOX_HEAD_DIM), 1)

    @pl.when(first_tab[step_id] != 0)
    def _():
        m_scr[...] = jnp.full_like(m_scr, NEG_INF)
        acc_scr[...] = jnp.zeros_like(acc_scr)
        cq = cq_ref[0, 0] * LOG2E
        for hh in range(2):
            cq_scr[hh] = jnp.broadcast_to(cq[:, hh:hh + 1], (tq, 128))

    q = q_ref[0]
    kb = k_ref[0].astype(BF16)
    v = v_ref[0]
    ck = ck_ref[0, 0] * LOG2E

    def softmax_rows(hh, masked):
        ckb = [jnp.broadcast_to(ck[hh:hh + 1, lo:hi], (ATTN_ROWS, hi - lo)) for lo, hi in cols]
        for r in range(0, tq, ATTN_ROWS):
            t = [s_scr[hh, r:r + ATTN_ROWS, lo:hi] - ckb[c] for c, (lo, hi) in enumerate(cols)]
            if masked:
                qpos = q_lo + r + lax.broadcasted_iota(jnp.int32, (ATTN_ROWS, 128), 0)
                kpos = k_lo + lax.broadcasted_iota(jnp.int32, (ATTN_ROWS, 128), 1)
                t = [jnp.where((kpos + lo <= qpos)[:, :hi - lo], t[c], NEG_INF)
                     for c, (lo, hi) in enumerate(cols)]
            full = [t[c] for c, (lo, hi) in enumerate(cols) if hi - lo == 128]
            tmax = full[0]
            for tc in full[1:]:
                tmax = jnp.maximum(tmax, tc)
            tmax = jnp.max(tmax, axis=1, keepdims=True)
            if len(full) < len(cols):
                tmax = jnp.maximum(tmax, jnp.max(t[-1], axis=1, keepdims=True))
            tmax = jnp.broadcast_to(tmax, (ATTN_ROWS, 128))
            cq2 = cq_scr[hh, r:r + ATTN_ROWS, :]
            m_old = m_scr[hh, r:r + ATTN_ROWS, :]
            m_new = jnp.maximum(m_old, tmax + cq2)
            m_scr[hh, r:r + ATTN_ROWS, :] = m_new
            a_scr[hh, r:r + ATTN_ROWS, :] = jnp.exp2(m_old - m_new)
            mt = m_new - cq2
            for c, (lo, hi) in enumerate(cols):
                p_scr[hh, r:r + ATTN_ROWS, lo:hi] = jnp.exp2(t[c] - mt[:, :hi - lo]).astype(BF16)

    def step(masked):
        in_heads = [(lane >= hh * FOX_HEAD_DIM) & (lane < (hh + 1) * FOX_HEAD_DIM) for hh in range(2)]
        for hh in range(2):
            qm = jnp.where(in_heads[hh], q, jnp.zeros_like(q))
            s_scr[hh] = lax.dot_general(qm, kb, (((1,), (1,)), ((), ())),
                                        preferred_element_type=F32)
        for hh in range(2):
            in_head = in_heads[hh]
            softmax_rows(hh, masked)
            vh = jnp.where(in_head, v, 1.0).astype(BF16)
            pv = jnp.dot(p_scr[hh], vh, preferred_element_type=F32)
            acc_scr[hh] = a_scr[hh] * acc_scr[hh] + pv

    needs_mask = k_lo + tk - 1 > q_lo

    @pl.when(needs_mask)
    def _():
        step(True)

    @pl.when(jnp.logical_not(needs_mask))
    def _():
        step(False)

    @pl.when(kj == (q_lo + tq - 1) // tk)
    def _():
        o = []
        for hh in range(2):
            acc = acc_scr[hh]
            o.append(acc * (1.0 / pltpu.roll(acc, FOX_HEAD_DIM, axis=1)))
        o_ref[0] = jnp.where(lane < FOX_HEAD_DIM, o[0], o[1]).astype(BF16)


def _live_tiles(qk_stats, c_all, nb, nq, t):
    qmax = jnp.sqrt(qk_stats[:, 0, :FOX_HEADS]).reshape(nb, nq, FOX_HEADS) * NORM_MARGIN
    kmax = jnp.sqrt(qk_stats[:, 1, :FOX_HEADS]).reshape(nb, nq, FOX_HEADS) * NORM_MARGIN
    c_first = c_all[:, 0::t, :]
    c_last = c_all[:, t - 1::t, :]
    bound = (qmax[:, :, None, :] * (kmax[:, None, :, :] + kmax[:, :, None, :])
             + c_first[:, :, None, :] - c_last[:, None, :, :])
    dead = jnp.all((bound < -DEAD_LOGIT).reshape(nb, nq, nq, HEAD_PAIRS, 2), axis=-1)
    below = (jnp.arange(nq)[None, :] < jnp.arange(nq)[:, None])[None, :, :, None]
    return jnp.transpose(jnp.logical_not(dead & below), (0, 3, 1, 2))


def _attention(q, k_all, v_all, cq, ck, q_off, tq, tk, live=None):
    nb, lq, _ = q.shape
    nq = lq // tq
    nk = k_all.shape[1] // tk
    last = (q_off + np.arange(nq) * tq + tq - 1) // tk
    causal = np.arange(nk)[None, :] <= last[:, None]
    valid = jnp.broadcast_to(causal if live is None else live & causal, (nb, HEAD_PAIRS, nq, nk))
    first = valid & (jnp.cumsum(valid.astype(jnp.int32), axis=3) == 1)
    max_steps = nb * HEAD_PAIRS * int(causal.sum())
    (idx,) = jnp.nonzero(valid.reshape(-1), size=max_steps, fill_value=0)
    idx = idx.astype(jnp.int32)
    n_steps = jnp.sum(valid, dtype=jnp.int32)
    kj_tab = idx % nk
    qi_tab = (idx // nk) % nq
    hp_tab = (idx // (nk * nq)) % HEAD_PAIRS
    b_tab = idx // (nk * nq * HEAD_PAIRS)
    first_tab = first.reshape(-1)[idx].astype(jnp.int32)
    qmap = lambda i, bt, ht, qt, kt, ft: (bt[i], qt[i], ht[i])
    kmap = lambda i, bt, ht, qt, kt, ft: (bt[i], kt[i], ht[i])
    return pl.pallas_call(
        functools.partial(_attn_kernel, tq=tq, tk=tk, q_off=q_off),
        grid_spec=pltpu.PrefetchScalarGridSpec(
            num_scalar_prefetch=5,
            grid=(n_steps,),
            in_specs=[
                pl.BlockSpec((1, tq, 128), qmap),
                pl.BlockSpec((1, tk, 128), kmap),
                pl.BlockSpec((1, tk, 128), kmap),
                pl.BlockSpec((1, 1, tq, 2), lambda i, bt, ht, qt, kt, ft: (bt[i], ht[i], qt[i], 0)),
                pl.BlockSpec((1, 1, 2, tk), lambda i, bt, ht, qt, kt, ft: (bt[i], ht[i], 0, kt[i])),
            ],
            out_specs=pl.BlockSpec((1, tq, 128), qmap),
            scratch_shapes=[pltpu.VMEM((2, tq, tk), F32), pltpu.VMEM((2, tq, tk), BF16),
                            pltpu.VMEM((2, tq, 128), F32), pltpu.VMEM((2, tq, 128), F32),
                            pltpu.VMEM((2, tq, 128), F32), pltpu.VMEM((2, tq, 128), F32)],
        ),
        out_shape=jax.ShapeDtypeStruct((nb, lq, W_MIX), BF16),
        compiler_params=_params("arbitrary"),
        name="fox_attention",
    )(b_tab, hp_tab, qi_tab, kj_tab, first_tab, q, k_all, v_all, cq, ck)


def _merge_kernel(x_ref, g_ref, ya_ref, yb_ref, yc_ref, wg_ref, wb_ref, wo_ref, o_ref):
    x = x_ref[...]
    hb = _rms_bf16(x, g_ref[...])
    merged = None
    for b, y_ref in enumerate((ya_ref, yb_ref, yc_ref)):
        gate = _sigmoid(jnp.dot(hb, wg_ref[:, b * D_MODEL:(b + 1) * D_MODEL],
                                preferred_element_type=F32))
        y = _load_lane_blocks(y_ref, slice(None)) if b == 1 else y_ref[...]
        term = gate * jnp.dot(y.astype(BF16), wb_ref[b], preferred_element_type=F32)
        merged = term if merged is None else merged + term
    o_ref[...] = x + jnp.dot(merged.astype(BF16), wo_ref[...], preferred_element_type=F32)


def _merge(x2, g, ya, yb, yc, w_gate, w_branch, w_out, tm):
    n = x2.shape[0]
    row = lambda w: pl.BlockSpec((tm, w), lambda i: (i, 0))
    const = lambda shape: pl.BlockSpec(shape, lambda i: (0,) * len(shape))
    return pl.pallas_call(
        _merge_kernel,
        grid=(n // tm,),
        in_specs=[row(D_MODEL), const((1, D_MODEL)), row(W_MIX),
                  pl.BlockSpec((LANE_BLOCKS, tm, 128), lambda i: (0, i, 0)), row(W_MIX),
                  const((D_MODEL, 3 * D_MODEL)), const((3, W_MIX, D_MODEL)),
                  const((D_MODEL, D_MODEL))],
        out_specs=row(D_MODEL),
        out_shape=jax.ShapeDtypeStruct((n, D_MODEL), F32),
        compiler_params=_params("parallel"),
        name="branch_merge",
    )(x2, g, ya, yb, yc, w_gate, w_branch, w_out)


FFN_HALO = 16


FFN_SUB = 256


def _ffn_kernel(x_ref, xh_ref, g_ref, hist_ref, wug_ref, wuv_ref, cwg_ref, cwv_ref,
                cbg_ref, cbv_ref, wd_ref, o_ref, cn_ref, h_scr, hh_scr, acc_scr, act_scr,
                *, tm, tn, nj):
    t = pl.program_id(1)
    j = pl.program_id(2)

    @pl.when(j == 0)
    def _():
        h_scr[...] = _rms_bf16(x_ref[...], g_ref[...])
        hh_scr[...] = _rms_bf16(xh_ref[...], g_ref[...])

    first = t == 0
    for lo in range(0, tn, FFN_SUB):
        hi = min(lo + FFN_SUB, tn)
        conv = []
        for hf, (wu_ref, cw_ref, cb_ref) in enumerate(((wug_ref, cwg_ref, cbg_ref),
                                                       (wuv_ref, cwv_ref, cbv_ref))):
            w = wu_ref[:, lo:hi]
            up = jnp.dot(h_scr[...], w, preferred_element_type=F32)
            uph = jnp.dot(hh_scr[...], w, preferred_element_type=F32)
            prev = jnp.where(first, hist_ref[0, hf, :, lo:hi], uph[FFN_HALO - 8:, :])
            full = jnp.concatenate([prev, up], axis=0)
            cn_ref[0, hf, j, :, lo:hi] = up[tm - 2:, :]
            cw = cw_ref[:, lo:hi]
            conv.append(cb_ref[:, lo:hi] + full[6:6 + tm, :] * cw[0:1, :]
                        + full[7:7 + tm, :] * cw[1:2, :] + up * cw[2:3, :])
        gate, val = conv
        act_scr[:, lo:hi] = (gate * _sigmoid(gate) * val).astype(BF16)
    down = jnp.dot(act_scr[...], wd_ref[...], preferred_element_type=F32)

    if nj == 1:
        o_ref[...] = x_ref[...] + down
        return

    @pl.when(j == 0)
    def _():
        acc_scr[...] = down

    @pl.when((j > 0) & (j < nj - 1))
    def _():
        acc_scr[...] += down

    @pl.when(j == nj - 1)
    def _():
        o_ref[...] = x_ref[...] + acc_scr[...] + down


def _ffn(x2, g, hist, wug, wuv, cwg, cwv, cbg, cbv, wd, nb, L, tm, tn):
    n = x2.shape[0]
    nt, nj = L // tm, D_FF // tn
    hpt = tm // FFN_HALO
    row = lambda b, t, j: (b * nt + t, 0)
    col = lambda b, t, j: (0, j)
    resident = dict(pipeline_mode=pl.Buffered(1)) if nj == 1 else {}
    return pl.pallas_call(
        functools.partial(_ffn_kernel, tm=tm, tn=tn, nj=nj),
        grid=(nb, nt, nj),
        in_specs=[
            pl.BlockSpec((tm, D_MODEL), row),
            pl.BlockSpec((FFN_HALO, D_MODEL), lambda b, t, j: (jnp.maximum((b * nt + t) * hpt - 1, 0), 0)),
            pl.BlockSpec((1, D_MODEL), lambda b, t, j: (0, 0)),
            pl.BlockSpec((1, 2, 8, tn), lambda b, t, j: (b, 0, 0, j)),
            pl.BlockSpec((D_MODEL, tn), col, **resident),
            pl.BlockSpec((D_MODEL, tn), col, **resident),
            pl.BlockSpec((3, tn), col),
            pl.BlockSpec((3, tn), col),
            pl.BlockSpec((1, tn), col),
            pl.BlockSpec((1, tn), col),
            pl.BlockSpec((tn, D_MODEL), lambda b, t, j: (j, 0), **resident),
        ],
        out_specs=[pl.BlockSpec((tm, D_MODEL), row),
                   pl.BlockSpec((1, 2, nj, 2, tn), lambda b, t, j: (b, 0, 0, 0, 0))],
        out_shape=[jax.ShapeDtypeStruct((n, D_MODEL), F32),
                   jax.ShapeDtypeStruct((nb, 2, nj, 2, tn), F32)],
        scratch_shapes=[pltpu.VMEM((tm, D_MODEL), BF16), pltpu.VMEM((FFN_HALO, D_MODEL), BF16),
                        pltpu.VMEM((tm, D_MODEL) if nj > 1 else (8, 128), F32),
                        pltpu.VMEM((tm, tn), BF16)],
        compiler_params=_params("arbitrary", "arbitrary", "arbitrary"),
        name="conv_ffn",
    )(x2, x2, g, hist, wug, wuv, cwg, cwv, cbg, cbv, wd)


def _final_norm_kernel(x_ref, g_ref, o_ref):
    x = x_ref[...]
    ms = jnp.mean(x * x, axis=-1, keepdims=True)
    o_ref[...] = x * lax.rsqrt(ms + EPS) * g_ref[...]


def _final_norm(x2, g, tm):
    n = x2.shape[0]
    row = pl.BlockSpec((tm, D_MODEL), lambda i: (i, 0))
    return pl.pallas_call(
        _final_norm_kernel,
        grid=(n // tm,),
        in_specs=[row, pl.BlockSpec((1, D_MODEL), lambda i: (0, 0))],
        out_specs=row,
        out_shape=jax.ShapeDtypeStruct((n, D_MODEL), F32),
        compiler_params=_params("parallel"),
        name="final_norm",
    )(x2, g)


def _layer(x2, nb, L, pos0, pool_hist, s0, k_past, v_past, logf_past, conv_hist, p,
           layer, depth, kbuf, vbuf):
    n = nb * L
    tm = _tile(L, 512)
    u_pool, u_ssm, q, kbuf, vbuf, kb, vb, logf, qk_stats = _inproj(
        x2, p["g_mix"], p["w_main"], p["w_f"], p["b_f"], tm, layer, depth, kbuf, vbuf)

    y_a, pool_new = _pool(u_pool.reshape(nb, L, W_MIX), pool_hist, p["w_pool"], p["pool_scale"],
                          pos0, tm)

    nc = L // SSM_CHUNK
    sp = p["ssm"]
    tc = _tile(nb * nc, 128)
    vr, vi = _ssm_state(u_ssm, sp, tc)
    sr, si, sl_re, sl_im = _ssm_carry(vr.reshape(nb, nc, SSM_ALL_ST), vi.reshape(nb, nc, SSM_ALL_ST),
                                      s0[0], s0[1], sp)
    y_b = _ssm_out(u_ssm, sr.reshape(nb * nc, SSM_ALL_ST), si.reshape(nb * nc, SSM_ALL_ST), sp,
                   p["d_skip"], p["w_glu"], p["b_glu"], tc)

    q3 = q.reshape(nb, L, W_MIX)
    logf3 = logf.reshape(nb, L, FOX_HEADS)
    if k_past is None:
        k_all, v_all, logf_all = kb.reshape(nb, L, W_MIX), vb.reshape(nb, L, W_MIX), logf3
    else:
        k_all = jnp.concatenate([k_past, kbuf[layer].reshape(nb, L, W_MIX)], axis=1)
        v_all = jnp.concatenate([v_past, vbuf[layer].reshape(nb, L, W_MIX)], axis=1)
        logf_all = jnp.concatenate([logf_past, logf3], axis=1)
    lk = k_all.shape[1]
    c_all = _cumsum(logf_all, _tile(lk, 512))
    cq = jnp.transpose(c_all[:, lk - L:].reshape(nb, L, HEAD_PAIRS, 2), (0, 2, 1, 3))
    ck = jnp.transpose(c_all.reshape(nb, lk, HEAD_PAIRS, 2), (0, 2, 3, 1))
    tk = 512 if lk % 512 == 0 else lk
    live = _live_tiles(qk_stats, c_all, nb, L // tm, tm) if (k_past is None and tk == tm) else None
    y_c = _attention(q3, k_all, v_all, cq, ck, pos0, tm, tk, live)

    x2 = _merge(x2, p["g_mix"], y_a.reshape(n, W_MIX), y_b, y_c.reshape(n, W_MIX),
                p["w_gate"], p["w_branch"], p["w_out"], tm)
    x2, conv_new = _ffn(x2, p["g_ffn"], conv_hist, p["wug"], p["wuv"], p["cwg"], p["cwv"],
                        p["cbg"], p["cbv"], p["w_down"], nb, L, tm, D_FF)
    states = (pool_new,
              sl_re.reshape(nb, SSM_GROUPS, SSM_STATE), sl_im.reshape(nb, SSM_GROUPS, SSM_STATE),
              logf3,
              jnp.transpose(conv_new, (0, 3, 1, 2, 4)).reshape(nb, 2, 2 * D_FF))
    return x2, states, kbuf, vbuf


def kernel(x_prompt, x_sample, cache_pool, state_ssm_re, state_ssm_im, cache_k, cache_v, cache_logf, cache_ffn_conv, norm_mix_g, w_in, b_f, w_pool, pool_scale, lam_re, lam_im, log_dt, b_re, b_im, c_re, c_im, d_skip, w_glu, b_glu, w_branch, w_out, norm_ffn_g, w_up, conv_w, conv_b, w_down, norm_final_g):
    bp, lp, _ = x_prompt.shape
    bs, ls, _ = x_sample.shape
    depth = w_in.shape[0]
    past = cache_k.shape[2]
    col_f = 5 * W_MIX
    col_gate = col_f + FOX_HEADS

    hp = x_prompt.reshape(bp * lp, D_MODEL)
    hs = x_sample.reshape(bs * ls, D_MODEL)
    zero_pool = jnp.zeros((bp, POOL_HIST, W_MIX), F32)
    zero_s = (jnp.zeros((bp, 1, SSM_ALL_ST), F32),) * 2
    zero_conv = jnp.zeros((bp, 2, 8, D_FF), F32)

    st_p, st_s = [], []
    kp = vp = ks = vs = None
    for l in range(depth):
        p = dict(
            ssm=_ssm_blockdiag(lam_re[l], lam_im[l], log_dt[l], b_re[l], b_im[l], c_re[l], c_im[l]),
            g_mix=norm_mix_g[l][None], g_ffn=norm_ffn_g[l][None],
            w_main=w_in[l][:, :col_f].astype(BF16),
            w_f=jnp.pad(w_in[l][:, col_f:col_gate], ((0, 0), (0, 128 - FOX_HEADS))).astype(BF16),
            b_f=b_f[l][None],
            w_gate=w_in[l][:, col_gate:].astype(BF16),
            w_pool=w_pool[l].astype(BF16), pool_scale=pool_scale[l][None],
            d_skip=d_skip[l][None], w_glu=w_glu[l].astype(BF16), b_glu=b_glu[l][None],
            w_branch=w_branch[l].astype(BF16), w_out=w_out[l].astype(BF16),
            wug=w_up[l][:, :D_FF].astype(BF16), wuv=w_up[l][:, D_FF:].astype(BF16),
            cwg=conv_w[l][:, :D_FF], cwv=conv_w[l][:, D_FF:],
            cbg=conv_b[l][None, :D_FF], cbv=conv_b[l][None, D_FF:],
            w_down=w_down[l].astype(BF16),
        )
        hp, sp, kp, vp = _layer(hp, bp, lp, 0, zero_pool, zero_s, None, None, None, zero_conv, p,
                                l, depth, kp, vp)
        s0 = (state_ssm_re[l].reshape(bs, 1, SSM_ALL_ST), state_ssm_im[l].reshape(bs, 1, SSM_ALL_ST))
        conv_hist = jnp.pad(jnp.transpose(cache_ffn_conv[l].reshape(bs, 2, 2, D_FF), (0, 2, 1, 3)),
                            ((0, 0), (0, 0), (6, 0), (0, 0)))
        hs, ss, ks, vs = _layer(hs, bs, ls, past, cache_pool[l], s0,
                                cache_k[l].reshape(bs, past, W_MIX), cache_v[l].reshape(bs, past, W_MIX),
                                cache_logf[l], conv_hist, p, l, depth, ks, vs)
        st_p.append(sp)
        st_s.append(ss)

    y_prompt = _final_norm(hp, norm_final_g[None], _tile(bp * lp, 512)).reshape(bp, lp, D_MODEL)
    y_sample = _final_norm(hs, norm_final_g[None], _tile(bs * ls, 512)).reshape(bs, ls, D_MODEL)
    stack = lambda st, i: jnp.stack([s[i] for s in st], axis=0)
    heads_p = (depth, bp, lp, FOX_HEADS, FOX_HEAD_DIM)
    heads_s = (depth, bs, ls, FOX_HEADS, FOX_HEAD_DIM)
    return (y_prompt, y_sample,
            stack(st_p, 0), stack(st_s, 0), stack(st_p, 1), stack(st_s, 1),
            stack(st_p, 2), stack(st_s, 2),
            kp.reshape(heads_p), ks.reshape(heads_s), vp.reshape(heads_p), vs.reshape(heads_s),
            stack(st_p, 3), stack(st_s, 3), stack(st_p, 4), stack(st_s, 4))
```

```python
import functools
import math

import jax
import jax.numpy as jnp
import numpy as np
from jax import lax
from jax.experimental import pallas as pl
from jax.experimental.pallas import tpu as pltpu

F32 = jnp.float32
BF16 = jnp.bfloat16

D_MODEL = 1024
W_MIX = 512
POOL_WINDOWS = (2, 4, 8, 16)
POOL_GROUP_W = 128
POOL_HIST = 15
SSM_GROUPS = 32
SSM_STATE = 64
SSM_CHUNK = 16
FOX_HEADS = 8
FOX_HEAD_DIM = 64
HEAD_PAIRS = FOX_HEADS // 2
D_FF = 2816
EPS = 1e-6
NEG_INF = -1e30
VMEM_LIMIT = 56 * 1024 * 1024


def _params(*sem):
    return pltpu.CompilerParams(dimension_semantics=sem, vmem_limit_bytes=VMEM_LIMIT)


def _tile(n, target, mult=8):
    if n <= target:
        return n
    for t in range(target, 0, -1):
        if n % t == 0 and t % mult == 0:
            return t
    return n


def _rms_bf16(x, g):
    ms = jnp.mean(x * x, axis=-1, keepdims=True)
    return (x * lax.rsqrt(ms + EPS) * g).astype(BF16)


def _sigmoid(x):
    return 1.0 / (1.0 + jnp.exp(-x))


LANE_BLOCKS = W_MIX // 128


def _load_lane_blocks(ref, rows, lo=0, hi=LANE_BLOCKS):
    return jnp.concatenate([ref[j, rows, :] for j in range(lo, hi)], axis=1)


def _store_lane_blocks(ref, rows, val):
    for j in range(LANE_BLOCKS):
        ref[j, rows, :] = val[:, j * 128:(j + 1) * 128]


def _inproj_kernel(*refs):
    x_ref, g_ref, w_ref, wf_ref, bf_ref, sel_ref = refs[:6]
    up_ref, us_ref, q_ref, k_ref, v_ref, kb_ref, vb_ref, lf_ref, st_ref = refs[-9:]
    hb = _rms_bf16(x_ref[...], g_ref[...])

    def proj(idx):
        return jnp.dot(hb, w_ref[:, idx * W_MIX:(idx + 1) * W_MIX], preferred_element_type=F32)

    def head_norm2_max(a):
        return jnp.max(jnp.dot((a * a).astype(BF16), sel_ref[...], preferred_element_type=F32),
                       axis=0, keepdims=True)

    up_ref[...] = proj(0)
    _store_lane_blocks(us_ref, slice(None), proj(1))
    qn = proj(2) * (FOX_HEAD_DIM ** -0.5)
    q_ref[...] = (qn * LOG2E).astype(BF16)
    k = proj(3)
    k_ref[0] = k
    kb_ref[...] = k.astype(BF16)
    row = lax.broadcasted_iota(jnp.int32, (8, 128), 0)
    st_ref[0] = jnp.where(row == 0, head_norm2_max(qn), jnp.where(row == 1, head_norm2_max(k), 0.0))
    v = proj(4)
    v_ref[0] = v
    vb_ref[...] = v.astype(BF16)
    zf = jnp.dot(hb, wf_ref[...], preferred_element_type=F32)
    a = zf[:, :FOX_HEADS] + bf_ref[...]
    lf_ref[...] = jnp.minimum(a, 0.0) - jnp.log(1.0 + jnp.exp(-jnp.abs(a)))


def _inproj(x2, g, w_main, w_f, b_f, tm, layer, depth, kbuf, vbuf):
    n = x2.shape[0]
    row = lambda w: pl.BlockSpec((tm, w), lambda i: (i, 0))
    const = lambda shape: pl.BlockSpec(shape, lambda i: (0,) * len(shape))
    stacked = pl.BlockSpec((1, tm, W_MIX), lambda i: (layer, i, 0))
    f32 = jax.ShapeDtypeStruct((n, W_MIX), F32)
    b16 = jax.ShapeDtypeStruct((n, W_MIX), BF16)
    big = jax.ShapeDtypeStruct((depth, n, W_MIX), F32)
    head_sel = (jnp.arange(W_MIX)[:, None] // FOX_HEAD_DIM == jnp.arange(128)[None, :]).astype(BF16)
    in_specs = [row(D_MODEL), const((1, D_MODEL)), const((D_MODEL, 5 * W_MIX)),
                const((D_MODEL, 128)), const((1, FOX_HEADS)), const((W_MIX, 128))]
    args = [x2, g, w_main, w_f, b_f, head_sel]
    aliases = {}
    if kbuf is not None:
        in_specs += [pl.BlockSpec(memory_space=pl.ANY)] * 2
        args += [kbuf, vbuf]
        aliases = {6: 3, 7: 4}
    return pl.pallas_call(
        _inproj_kernel,
        grid=(n // tm,),
        in_specs=in_specs,
        out_specs=[row(W_MIX), pl.BlockSpec((LANE_BLOCKS, tm, 128), lambda i: (0, i, 0)), row(W_MIX),
                   stacked, stacked, row(W_MIX), row(W_MIX), row(FOX_HEADS),
                   pl.BlockSpec((1, 8, 128), lambda i: (i, 0, 0))],
        out_shape=[f32, jax.ShapeDtypeStruct((LANE_BLOCKS, n, 128), F32), b16, big, big, b16, b16,
                   jax.ShapeDtypeStruct((n, FOX_HEADS), F32),
                   jax.ShapeDtypeStruct((n // tm, 8, 128), F32)],
        input_output_aliases=aliases,
        compiler_params=_params("parallel"),
        name="in_proj",
    )(*args)


def _pool_kernel(u_ref, hist_ref, w_ref, sc_ref, y_ref, pn_ref, buf, *, tm, pos0):
    t = pl.program_id(1)

    @pl.when(t == 0)
    def _():
        buf[0:1, :] = jnp.zeros((1, W_MIX), F32)
        buf[1:16, :] = hist_ref[0]

    @pl.when(t > 0)
    def _():
        buf[0:16, :] = buf[tm:tm + 16, :]

    buf[16:16 + tm, :] = u_ref[0]
    pos = pos0 + t * tm + lax.broadcasted_iota(jnp.int32, (tm, 1), 0)
    for g, w in enumerate(POOL_WINDOWS):
        lo, hi = g * POOL_GROUP_W, (g + 1) * POOL_GROUP_W
        full = buf[:, lo:hi]
        s = full
        k = 1
        while k < w:
            s = s + pltpu.roll(s, k, axis=0)
            k *= 2
        cnt = jnp.minimum(pos + 1, w).astype(F32)
        d = s[16:] * (1.0 / cnt) - full[16:]
        y = jnp.dot(d.astype(BF16), w_ref[g], preferred_element_type=F32) * sc_ref[:, lo:hi]
        y_ref[0, :, lo:hi] = y.astype(BF16)
    pn_ref[0] = buf[tm + 1:tm + 16, :]


def _pool(u, hist, w_pool, scale, pos0, tm):
    nb, L, _ = u.shape
    return pl.pallas_call(
        functools.partial(_pool_kernel, tm=tm, pos0=pos0),
        grid=(nb, L // tm),
        in_specs=[
            pl.BlockSpec((1, tm, W_MIX), lambda b, t: (b, t, 0)),
            pl.BlockSpec((1, POOL_HIST, W_MIX), lambda b, t: (b, 0, 0)),
            pl.BlockSpec((4, POOL_GROUP_W, POOL_GROUP_W), lambda b, t: (0, 0, 0)),
            pl.BlockSpec((1, W_MIX), lambda b, t: (0, 0)),
        ],
        out_specs=[
            pl.BlockSpec((1, tm, W_MIX), lambda b, t: (b, t, 0)),
            pl.BlockSpec((1, POOL_HIST, W_MIX), lambda b, t: (b, 0, 0)),
        ],
        out_shape=[jax.ShapeDtypeStruct((nb, L, W_MIX), BF16),
                   jax.ShapeDtypeStruct((nb, POOL_HIST, W_MIX), F32)],
        scratch_shapes=[pltpu.VMEM((tm + 16, W_MIX), F32)],
        compiler_params=_params("parallel", "arbitrary"),
        name="pool_mixer",
    )(u, hist, w_pool, scale)


SSM_HALF_CH = 256
SSM_HALF_ST = 1024
SSM_ALL_ST = 2048
SSM_BLOCK = 4


def _ssm_blockdiag(lam_re, lam_im, log_dt, b_re, b_im, c_re, c_im):
    dt = jnp.exp(log_dt)[:, None]
    ar, ai = lam_re * dt, lam_im * dt

    def power(n):
        mag = jnp.exp(n * ar)
        return mag * jnp.cos(n * ai), mag * jnp.sin(n * ai)

    lbr, lbi = power(1.0)
    a16r, a16i = power(float(SSM_CHUNK))
    den = lam_re * lam_re + lam_im * lam_im
    fr = ((lbr - 1.0) * lam_re + lbi * lam_im) / den
    fi = (lbi * lam_re - (lbr - 1.0) * lam_im) / den
    bbr = fr[..., None] * b_re - fi[..., None] * b_im
    bbi = fr[..., None] * b_im + fi[..., None] * b_re
    eye = jnp.eye(SSM_GROUPS, dtype=F32)

    def bd_in(m):
        return jnp.einsum('gpc,gh->gchp', m, eye).reshape(W_MIX, SSM_ALL_ST)

    def bd_out(m):
        return jnp.einsum('gcp,gh->gphc', m, eye).reshape(SSM_ALL_ST, W_MIX)

    def halves(m, rw, cw):
        return jnp.stack([m[h * rw:(h + 1) * rw, h * cw:(h + 1) * cw] for h in range(2)])

    bdb = jnp.concatenate([halves(bd_in(bbr), SSM_HALF_CH, SSM_HALF_ST),
                           halves(bd_in(bbi), SSM_HALF_CH, SSM_HALF_ST)], axis=2).astype(BF16)
    bdc = jnp.concatenate([halves(bd_out(c_re), SSM_HALF_ST, SSM_HALF_CH),
                           halves(bd_out(-c_im), SSM_HALF_ST, SSM_HALF_CH)], axis=1).astype(BF16)
    blk_r, blk_i = [], []
    for i in range(SSM_BLOCK):
        pr, pi = power(float(SSM_BLOCK - 1 - i))
        blk_r.append(halves(bd_in(pr[..., None] * bbr - pi[..., None] * bbi), SSM_HALF_CH, SSM_HALF_ST))
        blk_i.append(halves(bd_in(pr[..., None] * bbi + pi[..., None] * bbr), SSM_HALF_CH, SSM_HALF_ST))
    bd4 = jnp.concatenate([jnp.concatenate(blk_r, axis=1), jnp.concatenate(blk_i, axis=1)],
                          axis=2).astype(BF16)
    a4r, a4i = power(float(SSM_BLOCK))
    flat = lambda m: m.reshape(1, SSM_ALL_ST)
    return dict(bdb=bdb, bdc=bdc, bd4=bd4, lam_r=flat(lbr), lam_i=flat(lbi),
                a4r=flat(a4r), a4i=flat(a4i), a16r=flat(a16r), a16i=flat(a16i))


def _chunk_rows(ref, t):
    return pl.ds(t, ref.shape[1] // SSM_CHUNK, stride=SSM_CHUNK)


def _ssm_step(u_ref, bdb_ref, lr, li, t, h, s):
    u_t = _load_lane_blocks(u_ref, _chunk_rows(u_ref, t), 2 * h, 2 * h + 2)
    bu = jnp.dot(u_t.astype(BF16), bdb_ref[h], preferred_element_type=F32)
    bur, bui = bu[:, :SSM_HALF_ST], bu[:, SSM_HALF_ST:]
    if s is None:
        return bur, bui
    sr, si = s
    return lr * sr - li * si + bur, lr * si + li * sr + bui


def _ssm_state_kernel(u_ref, bd4_ref, lr_ref, li_ref, vr_ref, vi_ref):
    for h in range(2):
        st = slice(h * SSM_HALF_ST, (h + 1) * SSM_HALF_ST)
        lr, li = lr_ref[:, st], li_ref[:, st]
        s = None
        for t in range(0, SSM_CHUNK, SSM_BLOCK):
            u4 = jnp.concatenate([_load_lane_blocks(u_ref, _chunk_rows(u_ref, t + i), 2 * h, 2 * h + 2)
                                  for i in range(SSM_BLOCK)], axis=1)
            bu = jnp.dot(u4.astype(BF16), bd4_ref[h], preferred_element_type=F32)
            bur, bui = bu[:, :SSM_HALF_ST], bu[:, SSM_HALF_ST:]
            s = (bur, bui) if s is None else (lr * s[0] - li * s[1] + bur, lr * s[1] + li * s[0] + bui)
        vr_ref[:, st] = s[0]
        vi_ref[:, st] = s[1]


def _ssm_state(u, sp, tc):
    r = u.shape[1] // SSM_CHUNK
    row = lambda w: pl.BlockSpec((tc, w), lambda i: (i, 0))
    const = lambda shape: pl.BlockSpec(shape, lambda i: (0,) * len(shape))
    out = jax.ShapeDtypeStruct((r, SSM_ALL_ST), F32)
    return pl.pallas_call(
        _ssm_state_kernel,
        grid=(r // tc,),
        in_specs=[pl.BlockSpec((LANE_BLOCKS, tc * SSM_CHUNK, 128), lambda i: (0, i, 0)),
                  const((2, SSM_BLOCK * SSM_HALF_CH, 2 * SSM_HALF_ST)),
                  const((1, SSM_ALL_ST)), const((1, SSM_ALL_ST))],
        out_specs=[row(SSM_ALL_ST), row(SSM_ALL_ST)],
        out_shape=[out, out],
        compiler_params=_params("parallel"),
        name="ssm_state",
    )(u, sp["bd4"], sp["a4r"], sp["a4i"])


def _ssm_carry_kernel(vr_ref, vi_ref, s0r_ref, s0i_ref, ar_ref, ai_ref,
                      sr_ref, si_ref, lr_ref, li_ref, *, nc):
    ar, ai = ar_ref[...], ai_ref[...]

    def body(c, s):
        sr, si = s
        sr_ref[0, pl.ds(c, 1), :] = sr
        si_ref[0, pl.ds(c, 1), :] = si
        return (ar * sr - ai * si + vr_ref[0, pl.ds(c, 1), :],
                ar * si + ai * sr + vi_ref[0, pl.ds(c, 1), :])

    sr, si = lax.fori_loop(0, nc, body, (s0r_ref[0], s0i_ref[0]))
    lr_ref[0] = sr
    li_ref[0] = si


def _ssm_carry(vr, vi, s0r, s0i, sp):
    nb, nc, _ = vr.shape
    seq = pl.BlockSpec((1, nc, SSM_ALL_ST), lambda b: (b, 0, 0))
    one = pl.BlockSpec((1, 1, SSM_ALL_ST), lambda b: (b, 0, 0))
    vec = pl.BlockSpec((1, SSM_ALL_ST), lambda b: (0, 0))
    return pl.pallas_call(
        functools.partial(_ssm_carry_kernel, nc=nc),
        grid=(nb,),
        in_specs=[seq, seq, one, one, vec, vec],
        out_specs=[seq, seq, one, one],
        out_shape=[jax.ShapeDtypeStruct((nb, nc, SSM_ALL_ST), F32)] * 2
                  + [jax.ShapeDtypeStruct((nb, 1, SSM_ALL_ST), F32)] * 2,
        compiler_params=_params("parallel"),
        name="ssm_carry",
    )(vr, vi, s0r, s0i, sp["a16r"], sp["a16i"])


def _ssm_out_kernel(u_ref, sr_ref, si_ref, bdb_ref, bdc_ref, lr_ref, li_ref, d_ref, w_ref, b_ref,
                    o_ref):
    lam, s = [], []
    for h in range(2):
        st = slice(h * SSM_HALF_ST, (h + 1) * SSM_HALF_ST)
        lam.append((lr_ref[:, st], li_ref[:, st]))
        s.append((sr_ref[:, st], si_ref[:, st]))
    for t in range(SSM_CHUNK):
        ys = []
        for h in range(2):
            s[h] = _ssm_step(u_ref, bdb_ref, lam[h][0], lam[h][1], t, h, s[h])
            ys.append(jnp.dot(s[h][0].astype(BF16), bdc_ref[h, :SSM_HALF_ST, :],
                              preferred_element_type=F32)
                      + jnp.dot(s[h][1].astype(BF16), bdc_ref[h, SSM_HALF_ST:, :],
                                preferred_element_type=F32))
        y = jnp.concatenate(ys, axis=1) + d_ref[...] * _load_lane_blocks(u_ref, _chunk_rows(u_ref, t))
        y = 0.5 * y * (1.0 + jnp.tanh(math.sqrt(2.0 / math.pi) * (y + 0.044715 * (y * y * y))))
        z = jnp.dot(y.astype(BF16), w_ref[...], preferred_element_type=F32) + b_ref[...]
        _store_lane_blocks(o_ref, _chunk_rows(o_ref, t), y * _sigmoid(z))


def _ssm_out(u2, sr, si, sp, d_skip, w_glu, b_glu, tc):
    r = u2.shape[1] // SSM_CHUNK
    row = lambda w: pl.BlockSpec((tc, w), lambda i: (i, 0))
    tok = pl.BlockSpec((LANE_BLOCKS, tc * SSM_CHUNK, 128), lambda i: (0, i, 0))
    const = lambda shape: pl.BlockSpec(shape, lambda i: (0,) * len(shape))
    return pl.pallas_call(
        _ssm_out_kernel,
        grid=(r // tc,),
        in_specs=[tok, row(SSM_ALL_ST), row(SSM_ALL_ST),
                  const((2, SSM_HALF_CH, 2 * SSM_HALF_ST)), const((2, 2 * SSM_HALF_ST, SSM_HALF_CH)),
                  const((1, SSM_ALL_ST)), const((1, SSM_ALL_ST)),
                  const((1, W_MIX)), const((W_MIX, W_MIX)), const((1, W_MIX))],
        out_specs=tok,
        out_shape=jax.ShapeDtypeStruct(u2.shape, F32),
        compiler_params=_params("parallel"),
        name="ssm_out",
    )(u2, sr, si, sp["bdb"], sp["bdc"], sp["lam_r"], sp["lam_i"], d_skip, w_glu, b_glu)


def _cumsum_kernel(lf_ref, c_ref, carry, *, tc):
    @pl.when(pl.program_id(1) == 0)
    def _():
        carry[...] = jnp.zeros_like(carry)

    r = lax.broadcasted_iota(jnp.int32, (tc, tc), 0)
    c = lax.broadcasted_iota(jnp.int32, (tc, tc), 1)
    tri = jnp.where(r >= c, 1.0, 0.0).astype(BF16)
    lf = lf_ref[0]
    hi = lf.astype(BF16)
    rest = lf - hi.astype(F32)
    mid = rest.astype(BF16)
    lo = (rest - mid.astype(F32)).astype(BF16)
    cs = sum(jnp.dot(tri, part, preferred_element_type=F32) for part in (lo, mid, hi)) + carry[...]
    c_ref[0] = cs
    carry[...] = cs[tc - 1:tc, :]


def _cumsum(lf, tc):
    nb, lk, _ = lf.shape
    blk = pl.BlockSpec((1, tc, FOX_HEADS), lambda b, t: (b, t, 0))
    return pl.pallas_call(
        functools.partial(_cumsum_kernel, tc=tc),
        grid=(nb, lk // tc),
        in_specs=[blk],
        out_specs=blk,
        out_shape=jax.ShapeDtypeStruct((nb, lk, FOX_HEADS), F32),
        scratch_shapes=[pltpu.VMEM((1, FOX_HEADS), F32)],
        compiler_params=_params("parallel", "arbitrary"),
        name="logf_cumsum",
    )(lf)


ATTN_ROWS = 16
LOG2E = math.log2(math.e)
DEAD_LOGIT = 120.0
NORM_MARGIN = 1.02


def _attn_kernel(b_tab, hp_tab, qi_tab, kj_tab, first_tab, q_ref, k_ref, v_ref, cq_ref, ck_ref,
                 o_ref, s_scr, p_scr, m_scr, a_scr, cq_scr, acc_scr, *, tq, tk, q_off):
    step_id = pl.program_id(0)
    qi = qi_tab[step_id]
    kj = kj_tab[step_id]
    q_lo = q_off + qi * tq
    k_lo = kj * tk
    cols = [(c, min(c + 128, tk)) for c in range(0, tk, 128)]
    lane = lax.broadcasted_iota(jnp.int32, (1, 2 * FOX_HEAD_DIM), 1)

    @pl.when(first_tab[step_id] != 0)
    def _():
        m_scr[...] = jnp.full_like(m_scr, NEG_INF)
        acc_scr[...] = jnp.zeros_like(acc_scr)
        cq = cq_ref[0, 0] * LOG2E
        for hh in range(2):
            cq_scr[hh] = jnp.broadcast_to(cq[:, hh:hh + 1], (tq, 128))

    q = q_ref[0]
    kb = k_ref[0].astype(BF16)
    v = v_ref[0]
    ck = ck_ref[0, 0] * LOG2E

    def softmax_rows(hh, masked):
        ckb = [jnp.broadcast_to(ck[hh:hh + 1, lo:hi], (ATTN_ROWS, hi - lo)) for lo, hi in cols]
        for r in range(0, tq, ATTN_ROWS):
            t = [s_scr[hh, r:r + ATTN_ROWS, lo:hi] - ckb[c] for c, (lo, hi) in enumerate(cols)]
            if masked:
                qpos = q_lo + r + lax.broadcasted_iota(jnp.int32, (ATTN_ROWS, 128), 0)
                kpos = k_lo + lax.broadcasted_iota(jnp.int32, (ATTN_ROWS, 128), 1)
                t = [jnp.where((kpos + lo <= qpos)[:, :hi - lo], t[c], NEG_INF)
                     for c, (lo, hi) in enumerate(cols)]
            full = [t[c] for c, (lo, hi) in enumerate(cols) if hi - lo == 128]
            tmax = full[0]
            for tc in full[1:]:
                tmax = jnp.maximum(tmax, tc)
            tmax = jnp.max(tmax, axis=1, keepdims=True)
            if len(full) < len(cols):
                tmax = jnp.maximum(tmax, jnp.max(t[-1], axis=1, keepdims=True))
            tmax = jnp.broadcast_to(tmax, (ATTN_ROWS, 128))
            cq2 = cq_scr[hh, r:r + ATTN_ROWS, :]
            m_old = m_scr[hh, r:r + ATTN_ROWS, :]
            m_new = jnp.maximum(m_old, tmax + cq2)
            m_scr[hh, r:r + ATTN_ROWS, :] = m_new
            a_scr[hh, r:r + ATTN_ROWS, :] = jnp.exp2(m_old - m_new)
            mt = m_new - cq2
            for c, (lo, hi) in enumerate(cols):
                p_scr[hh, r:r + ATTN_ROWS, lo:hi] = jnp.exp2(t[c] - mt[:, :hi - lo]).astype(BF16)

    def step(masked):
        in_heads = [(lane >= hh * FOX_HEAD_DIM) & (lane < (hh + 1) * FOX_HEAD_DIM) for hh in range(2)]
        for hh in range(2):
            qm = jnp.where(in_heads[hh], q, jnp.zeros_like(q))
            s_scr[hh] = lax.dot_general(qm, kb, (((1,), (1,)), ((), ())),
                                        preferred_element_type=F32)
        for hh in range(2):
            in_head = in_heads[hh]
            softmax_rows(hh, masked)
            vh = jnp.where(in_head, v, 1.0).astype(BF16)
            pv = jnp.dot(p_scr[hh], vh, preferred_element_type=F32)
            acc_scr[hh] = a_scr[hh] * acc_scr[hh] + pv

    needs_mask = k_lo + tk - 1 > q_lo

    @pl.when(needs_mask)
    def _():
        step(True)

    @pl.when(jnp.logical_not(needs_mask))
    def _():
        step(False)

    @pl.when(kj == (q_lo + tq - 1) // tk)
    def _():
        o = []
        for hh in range(2):
            acc = acc_scr[hh]
            o.append(acc * (1.0 / pltpu.roll(acc, FOX_HEAD_DIM, axis=1)))
        o_ref[0] = jnp.where(lane < FOX_HEAD_DIM, o[0], o[1]).astype(BF16)


def _live_tiles(qk_stats, c_all, nb, nq, t):
    qmax = jnp.sqrt(qk_stats[:, 0, :FOX_HEADS]).reshape(nb, nq, FOX_HEADS) * NORM_MARGIN
    kmax = jnp.sqrt(qk_stats[:, 1, :FOX_HEADS]).reshape(nb, nq, FOX_HEADS) * NORM_MARGIN
    c_first = c_all[:, 0::t, :]
    c_last = c_all[:, t - 1::t, :]
    bound = (qmax[:, :, None, :] * (kmax[:, None, :, :] + kmax[:, :, None, :])
             + c_first[:, :, None, :] - c_last[:, None, :, :])
    dead = jnp.all((bound < -DEAD_LOGIT).reshape(nb, nq, nq, HEAD_PAIRS, 2), axis=-1)
    below = (jnp.arange(nq)[None, :] < jnp.arange(nq)[:, None])[None, :, :, None]
    return jnp.transpose(jnp.logical_not(dead & below), (0, 3, 1, 2))


def _attention(q, k_all, v_all, cq, ck, q_off, tq, tk, live=None):
    nb, lq, _ = q.shape
    nq = lq // tq
    nk = k_all.shape[1] // tk
    last = (q_off + np.arange(nq) * tq + tq - 1) // tk
    causal = np.arange(nk)[None, :] <= last[:, None]
    valid = jnp.broadcast_to(causal if live is None else live & causal, (nb, HEAD_PAIRS, nq, nk))
    first = valid & (jnp.cumsum(valid.astype(jnp.int32), axis=3) == 1)
    max_steps = nb * HEAD_PAIRS * int(causal.sum())
    (idx,) = jnp.nonzero(valid.reshape(-1), size=max_steps, fill_value=0)
    idx = idx.astype(jnp.int32)
    n_steps = jnp.sum(valid, dtype=jnp.int32)
    kj_tab = idx % nk
    qi_tab = (idx // nk) % nq
    hp_tab = (idx // (nk * nq)) % HEAD_PAIRS
    b_tab = idx // (nk * nq * HEAD_PAIRS)
    first_tab = first.reshape(-1)[idx].astype(jnp.int32)
    qmap = lambda i, bt, ht, qt, kt, ft: (bt[i], qt[i], ht[i])
    kmap = lambda i, bt, ht, qt, kt, ft: (bt[i], kt[i], ht[i])
    return pl.pallas_call(
        functools.partial(_attn_kernel, tq=tq, tk=tk, q_off=q_off),
        grid_spec=pltpu.PrefetchScalarGridSpec(
            num_scalar_prefetch=5,
            grid=(n_steps,),
            in_specs=[
                pl.BlockSpec((1, tq, 128), qmap),
                pl.BlockSpec((1, tk, 128), kmap),
                pl.BlockSpec((1, tk, 128), kmap),
                pl.BlockSpec((1, 1, tq, 2), lambda i, bt, ht, qt, kt, ft: (bt[i], ht[i], qt[i], 0)),
                pl.BlockSpec((1, 1, 2, tk), lambda i, bt, ht, qt, kt, ft: (bt[i], ht[i], 0, kt[i])),
            ],
            out_specs=pl.BlockSpec((1, tq, 128), qmap),
            scratch_shapes=[pltpu.VMEM((2, tq, tk), F32), pltpu.VMEM((2, tq, tk), BF16),
                            pltpu.VMEM((2, tq, 128), F32), pltpu.VMEM((2, tq, 128), F32),
                            pltpu.VMEM((2, tq, 128), F32), pltpu.VMEM((2, tq, 128), F32)],
        ),
        out_shape=jax.ShapeDtypeStruct((nb, lq, W_MIX), BF16),
        compiler_params=_params("arbitrary"),
        name="fox_attention",
    )(b_tab, hp_tab, qi_tab, kj_tab, first_tab, q, k_all, v_all, cq, ck)


def _merge_kernel(x_ref, g_ref, ya_ref, yb_ref, yc_ref, wg_ref, wb_ref, wo_ref, o_ref):
    x = x_ref[...]
    hb = _rms_bf16(x, g_ref[...])
    merged = None
    for b, y_ref in enumerate((ya_ref, yb_ref, yc_ref)):
        gate = _sigmoid(jnp.dot(hb, wg_ref[:, b * D_MODEL:(b + 1) * D_MODEL],
                                preferred_element_type=F32))
        y = _load_lane_blocks(y_ref, slice(None)) if b == 1 else y_ref[...]
        term = gate * jnp.dot(y.astype(BF16), wb_ref[b], preferred_element_type=F32)
        merged = term if merged is None else merged + term
    o_ref[...] = x + jnp.dot(merged.astype(BF16), wo_ref[...], preferred_element_type=F32)


def _merge(x2, g, ya, yb, yc, w_gate, w_branch, w_out, tm):
    n = x2.shape[0]
    row = lambda w: pl.BlockSpec((tm, w), lambda i: (i, 0))
    const = lambda shape: pl.BlockSpec(shape, lambda i: (0,) * len(shape))
    return pl.pallas_call(
        _merge_kernel,
        grid=(n // tm,),
        in_specs=[row(D_MODEL), const((1, D_MODEL)), row(W_MIX),
                  pl.BlockSpec((LANE_BLOCKS, tm, 128), lambda i: (0, i, 0)), row(W_MIX),
                  const((D_MODEL, 3 * D_MODEL)), const((3, W_MIX, D_MODEL)),
                  const((D_MODEL, D_MODEL))],
        out_specs=row(D_MODEL),
        out_shape=jax.ShapeDtypeStruct((n, D_MODEL), F32),
        compiler_params=_params("parallel"),
        name="branch_merge",
    )(x2, g, ya, yb, yc, w_gate, w_branch, w_out)


FFN_HALO = 16


FFN_SUB = 256


def _ffn_kernel(x_ref, xh_ref, g_ref, hist_ref, wug_ref, wuv_ref, cwg_ref, cwv_ref,
                cbg_ref, cbv_ref, wd_ref, o_ref, cn_ref, h_scr, hh_scr, acc_scr, act_scr,
                *, tm, tn, nj):
    t = pl.program_id(1)
    j = pl.program_id(2)

    @pl.when(j == 0)
    def _():
        h_scr[...] = _rms_bf16(x_ref[...], g_ref[...])
        hh_scr[...] = _rms_bf16(xh_ref[...], g_ref[...])

    first = t == 0
    for lo in range(0, tn, FFN_SUB):
        hi = min(lo + FFN_SUB, tn)
        conv = []
        for hf, (wu_ref, cw_ref, cb_ref) in enumerate(((wug_ref, cwg_ref, cbg_ref),
                                                       (wuv_ref, cwv_ref, cbv_ref))):
            w = wu_ref[:, lo:hi]
            up = jnp.dot(h_scr[...], w, preferred_element_type=F32)
            uph = jnp.dot(hh_scr[...], w, preferred_element_type=F32)
            prev = jnp.where(first, hist_ref[0, hf, :, lo:hi], uph[FFN_HALO - 8:, :])
            full = jnp.concatenate([prev, up], axis=0)
            cn_ref[0, hf, j, :, lo:hi] = up[tm - 2:, :]
            cw = cw_ref[:, lo:hi]
            conv.append(cb_ref[:, lo:hi] + full[6:6 + tm, :] * cw[0:1, :]
                        + full[7:7 + tm, :] * cw[1:2, :] + up * cw[2:3, :])
        gate, val = conv
        act_scr[:, lo:hi] = (gate * _sigmoid(gate) * val).astype(BF16)
    down = jnp.dot(act_scr[...], wd_ref[...], preferred_element_type=F32)

    if nj == 1:
        o_ref[...] = x_ref[...] + down
        return

    @pl.when(j == 0)
    def _():
        acc_scr[...] = down

    @pl.when((j > 0) & (j < nj - 1))
    def _():
        acc_scr[...] += down

    @pl.when(j == nj - 1)
    def _():
        o_ref[...] = x_ref[...] + acc_scr[...] + down


def _ffn(x2, g, hist, wug, wuv, cwg, cwv, cbg, cbv, wd, nb, L, tm, tn):
    n = x2.shape[0]
    nt, nj = L // tm, D_FF // tn
    hpt = tm // FFN_HALO
    row = lambda b, t, j: (b * nt + t, 0)
    col = lambda b, t, j: (0, j)
    resident = dict(pipeline_mode=pl.Buffered(1)) if nj == 1 else {}
    return pl.pallas_call(
        functools.partial(_ffn_kernel, tm=tm, tn=tn, nj=nj),
        grid=(nb, nt, nj),
        in_specs=[
            pl.BlockSpec((tm, D_MODEL), row),
            pl.BlockSpec((FFN_HALO, D_MODEL), lambda b, t, j: (jnp.maximum((b * nt + t) * hpt - 1, 0), 0)),
            pl.BlockSpec((1, D_MODEL), lambda b, t, j: (0, 0)),
            pl.BlockSpec((1, 2, 8, tn), lambda b, t, j: (b, 0, 0, j)),
            pl.BlockSpec((D_MODEL, tn), col, **resident),
            pl.BlockSpec((D_MODEL, tn), col, **resident),
            pl.BlockSpec((3, tn), col),
            pl.BlockSpec((3, tn), col),
            pl.BlockSpec((1, tn), col),
            pl.BlockSpec((1, tn), col),
            pl.BlockSpec((tn, D_MODEL), lambda b, t, j: (j, 0), **resident),
        ],
        out_specs=[pl.BlockSpec((tm, D_MODEL), row),
                   pl.BlockSpec((1, 2, nj, 2, tn), lambda b, t, j: (b, 0, 0, 0, 0))],
        out_shape=[jax.ShapeDtypeStruct((n, D_MODEL), F32),
                   jax.ShapeDtypeStruct((nb, 2, nj, 2, tn), F32)],
        scratch_shapes=[pltpu.VMEM((tm, D_MODEL), BF16), pltpu.VMEM((FFN_HALO, D_MODEL), BF16),
                        pltpu.VMEM((tm, D_MODEL) if nj > 1 else (8, 128), F32),
                        pltpu.VMEM((tm, tn), BF16)],
        compiler_params=_params("arbitrary", "arbitrary", "arbitrary"),
        name="conv_ffn",
    )(x2, x2, g, hist, wug, wuv, cwg, cwv, cbg, cbv, wd)


def _final_norm_kernel(x_ref, g_ref, o_ref):
    x = x_ref[...]
    ms = jnp.mean(x * x, axis=-1, keepdims=True)
    o_ref[...] = x * lax.rsqrt(ms + EPS) * g_ref[...]


def _final_norm(x2, g, tm):
    n = x2.shape[0]
    row = pl.BlockSpec((tm, D_MODEL), lambda i: (i, 0))
    return pl.pallas_call(
        _final_norm_kernel,
        grid=(n // tm,),
        in_specs=[row, pl.BlockSpec((1, D_MODEL), lambda i: (0, 0))],
        out_specs=row,
        out_shape=jax.ShapeDtypeStruct((n, D_MODEL), F32),
        compiler_params=_params("parallel"),
        name="final_norm",
    )(x2, g)


def _layer(x2, nb, L, pos0, pool_hist, s0, k_past, v_past, logf_past, conv_hist, p,
           layer, depth, kbuf, vbuf):
    n = nb * L
    tm = _tile(L, 512)
    u_pool, u_ssm, q, kbuf, vbuf, kb, vb, logf, qk_stats = _inproj(
        x2, p["g_mix"], p["w_main"], p["w_f"], p["b_f"], tm, layer, depth, kbuf, vbuf)

    y_a, pool_new = _pool(u_pool.reshape(nb, L, W_MIX), pool_hist, p["w_pool"], p["pool_scale"],
                          pos0, tm)

    nc = L // SSM_CHUNK
    sp = p["ssm"]
    tc = _tile(nb * nc, 128)
    vr, vi = _ssm_state(u_ssm, sp, tc)
    sr, si, sl_re, sl_im = _ssm_carry(vr.reshape(nb, nc, SSM_ALL_ST), vi.reshape(nb, nc, SSM_ALL_ST),
                                      s0[0], s0[1], sp)
    y_b = _ssm_out(u_ssm, sr.reshape(nb * nc, SSM_ALL_ST), si.reshape(nb * nc, SSM_ALL_ST), sp,
                   p["d_skip"], p["w_glu"], p["b_glu"], tc)

    q3 = q.reshape(nb, L, W_MIX)
    logf3 = logf.reshape(nb, L, FOX_HEADS)
    if k_past is None:
        k_all, v_all, logf_all = kb.reshape(nb, L, W_MIX), vb.reshape(nb, L, W_MIX), logf3
    else:
        k_all = jnp.concatenate([k_past, kbuf[layer].reshape(nb, L, W_MIX)], axis=1)
        v_all = jnp.concatenate([v_past, vbuf[layer].reshape(nb, L, W_MIX)], axis=1)
        logf_all = jnp.concatenate([logf_past, logf3], axis=1)
    lk = k_all.shape[1]
    c_all = _cumsum(logf_all, _tile(lk, 512))
    cq = jnp.transpose(c_all[:, lk - L:].reshape(nb, L, HEAD_PAIRS, 2), (0, 2, 1, 3))
    ck = jnp.transpose(c_all.reshape(nb, lk, HEAD_PAIRS, 2), (0, 2, 3, 1))
    tk = 512 if lk % 512 == 0 else lk
    live = _live_tiles(qk_stats, c_all, nb, L // tm, tm) if (k_past is None and tk == tm) else None
    y_c = _attention(q3, k_all, v_all, cq, ck, pos0, tm, tk, live)

    x2 = _merge(x2, p["g_mix"], y_a.reshape(n, W_MIX), y_b, y_c.reshape(n, W_MIX),
                p["w_gate"], p["w_branch"], p["w_out"], tm)
    x2, conv_new = _ffn(x2, p["g_ffn"], conv_hist, p["wug"], p["wuv"], p["cwg"], p["cwv"],
                        p["cbg"], p["cbv"], p["w_down"], nb, L, tm, D_FF)
    states = (pool_new,
              sl_re.reshape(nb, SSM_GROUPS, SSM_STATE), sl_im.reshape(nb, SSM_GROUPS, SSM_STATE),
              logf3,
              jnp.transpose(conv_new, (0, 3, 1, 2, 4)).reshape(nb, 2, 2 * D_FF))
    return x2, states, kbuf, vbuf


def kernel(x_prompt, x_sample, cache_pool, state_ssm_re, state_ssm_im, cache_k, cache_v, cache_logf, cache_ffn_conv, norm_mix_g, w_in, b_f, w_pool, pool_scale, lam_re, lam_im, log_dt, b_re, b_im, c_re, c_im, d_skip, w_glu, b_glu, w_branch, w_out, norm_ffn_g, w_up, conv_w, conv_b, w_down, norm_final_g):
    bp, lp, _ = x_prompt.shape
    bs, ls, _ = x_sample.shape
    depth = w_in.shape[0]
    past = cache_k.shape[2]
    col_f = 5 * W_MIX
    col_gate = col_f + FOX_HEADS

    hp = x_prompt.reshape(bp * lp, D_MODEL)
    hs = x_sample.reshape(bs * ls, D_MODEL)
    zero_pool = jnp.zeros((bp, POOL_HIST, W_MIX), F32)
    zero_s = (jnp.zeros((bp, 1, SSM_ALL_ST), F32),) * 2
    zero_conv = jnp.zeros((bp, 2, 8, D_FF), F32)

    st_p, st_s = [], []
    kp = vp = ks = vs = None
    for l in range(depth):
        p = dict(
            ssm=_ssm_blockdiag(lam_re[l], lam_im[l], log_dt[l], b_re[l], b_im[l], c_re[l], c_im[l]),
            g_mix=norm_mix_g[l][None], g_ffn=norm_ffn_g[l][None],
            w_main=w_in[l][:, :col_f].astype(BF16),
            w_f=jnp.pad(w_in[l][:, col_f:col_gate], ((0, 0), (0, 128 - FOX_HEADS))).astype(BF16),
            b_f=b_f[l][None],
            w_gate=w_in[l][:, col_gate:].astype(BF16),
            w_pool=w_pool[l].astype(BF16), pool_scale=pool_scale[l][None],
            d_skip=d_skip[l][None], w_glu=w_glu[l].astype(BF16), b_glu=b_glu[l][None],
            w_branch=w_branch[l].astype(BF16), w_out=w_out[l].astype(BF16),
            wug=w_up[l][:, :D_FF].astype(BF16), wuv=w_up[l][:, D_FF:].astype(BF16),
            cwg=conv_w[l][:, :D_FF], cwv=conv_w[l][:, D_FF:],
            cbg=conv_b[l][None, :D_FF], cbv=conv_b[l][None, D_FF:],
            w_down=w_down[l].astype(BF16),
        )
        hp, sp, kp, vp = _layer(hp, bp, lp, 0, zero_pool, zero_s, None, None, None, zero_conv, p,
                                l, depth, kp, vp)
        s0 = (state_ssm_re[l].reshape(bs, 1, SSM_ALL_ST), state_ssm_im[l].reshape(bs, 1, SSM_ALL_ST))
        conv_hist = jnp.pad(jnp.transpose(cache_ffn_conv[l].reshape(bs, 2, 2, D_FF), (0, 2, 1, 3)),
                            ((0, 0), (0, 0), (6, 0), (0, 0)))
        hs, ss, ks, vs = _layer(hs, bs, ls, past, cache_pool[l], s0,
                                cache_k[l].reshape(bs, past, W_MIX), cache_v[l].reshape(bs, past, W_MIX),
                                cache_logf[l], conv_hist, p, l, depth, ks, vs)
        st_p.append(sp)
        st_s.append(ss)

    y_prompt = _final_norm(hp, norm_final_g[None], _tile(bp * lp, 512)).reshape(bp, lp, D_MODEL)
    y_sample = _final_norm(hs, norm_final_g[None], _tile(bs * ls, 512)).reshape(bs, ls, D_MODEL)
    stack = lambda st, i: jnp.stack([s[i] for s in st], axis=0)
    heads_p = (depth, bp, lp, FOX_HEADS, FOX_HEAD_DIM)
    heads_s = (depth, bs, ls, FOX_HEADS, FOX_HEAD_DIM)
    return (y_prompt, y_sample,
            stack(st_p, 0), stack(st_s, 0), stack(st_p, 1), stack(st_s, 1),
            stack(st_p, 2), stack(st_s, 2),
            kp.reshape(heads_p), ks.reshape(heads_s), vp.reshape(heads_p), vs.reshape(heads_s),
            stack(st_p, 3), stack(st_s, 3), stack(st_p, 4), stack(st_s, 4))
```

```python
import functools
import math

import jax
import jax.numpy as jnp
import numpy as np
from jax import lax
from jax.experimental import pallas as pl
from jax.experimental.pallas import tpu as pltpu

F32 = jnp.float32
BF16 = jnp.bfloat16

D_MODEL = 1024
W_MIX = 512
POOL_WINDOWS = (2, 4, 8, 16)
POOL_GROUP_W = 128
POOL_HIST = 15
SSM_GROUPS = 32
SSM_STATE = 64
SSM_CHUNK = 16
FOX_HEADS = 8
FOX_HEAD_DIM = 64
HEAD_PAIRS = FOX_HEADS // 2
D_FF = 2816
EPS = 1e-6
NEG_INF = -1e30
VMEM_LIMIT = 56 * 1024 * 1024


def _params(*sem):
    return pltpu.CompilerParams(dimension_semantics=sem, vmem_limit_bytes=VMEM_LIMIT)


def _tile(n, target, mult=8):
    if n <= target:
        return n
    for t in range(target, 0, -1):
        if n % t == 0 and t % mult == 0:
            return t
    return n


def _rms_bf16(x, g):
    ms = jnp.mean(x * x, axis=-1, keepdims=True)
    return (x * lax.rsqrt(ms + EPS) * g).astype(BF16)


def _sigmoid(x):
    return 1.0 / (1.0 + jnp.exp(-x))


LANE_BLOCKS = W_MIX // 128
LOG2E = math.log2(math.e)


def _load_lane_blocks(ref, rows, lo=0, hi=LANE_BLOCKS):
    return jnp.concatenate([ref[j, rows, :] for j in range(lo, hi)], axis=1)


def _store_lane_blocks(ref, rows, val):
    for j in range(LANE_BLOCKS):
        ref[j, rows, :] = val[:, j * 128:(j + 1) * 128]


def _inproj_kernel(*refs):
    x_ref, g_ref, w_ref, wf_ref, bf_ref, sel_ref = refs[:6]
    up_ref, us_ref, q_ref, k_ref, v_ref, kb_ref, vb_ref, lf_ref, st_ref = refs[-9:]
    hb = _rms_bf16(x_ref[...], g_ref[...])

    def proj(idx):
        return jnp.dot(hb, w_ref[:, idx * W_MIX:(idx + 1) * W_MIX], preferred_element_type=F32)

    def head_norm2_max(a):
        return jnp.max(jnp.dot((a * a).astype(BF16), sel_ref[...], preferred_element_type=F32),
                       axis=0, keepdims=True)

    up_ref[...] = proj(0)
    _store_lane_blocks(us_ref, slice(None), proj(1))
    qn = proj(2) * (FOX_HEAD_DIM ** -0.5)
    q_ref[...] = (qn * LOG2E).astype(BF16)
    k = proj(3)
    k_ref[0] = k
    kb_ref[...] = k.astype(BF16)
    row = lax.broadcasted_iota(jnp.int32, (8, 128), 0)
    st_ref[0] = jnp.where(row == 0, head_norm2_max(qn), jnp.where(row == 1, head_norm2_max(k), 0.0))
    v = proj(4)
    v_ref[0] = v
    vb_ref[...] = v.astype(BF16)
    zf = jnp.dot(hb, wf_ref[...], preferred_element_type=F32)
    a = zf[:, :FOX_HEADS] + bf_ref[...]
    lf_ref[...] = jnp.minimum(a, 0.0) - jnp.log(1.0 + jnp.exp(-jnp.abs(a)))


def _inproj(x2, g, w_main, w_f, b_f, tm, layer, depth, kbuf, vbuf):
    n = x2.shape[0]
    row = lambda w: pl.BlockSpec((tm, w), lambda i: (i, 0))
    const = lambda shape: pl.BlockSpec(shape, lambda i: (0,) * len(shape))
    stacked = pl.BlockSpec((1, tm, W_MIX), lambda i: (layer, i, 0))
    f32 = jax.ShapeDtypeStruct((n, W_MIX), F32)
    b16 = jax.ShapeDtypeStruct((n, W_MIX), BF16)
    big = jax.ShapeDtypeStruct((depth, n, W_MIX), F32)
    head_sel = (jnp.arange(W_MIX)[:, None] // FOX_HEAD_DIM == jnp.arange(128)[None, :]).astype(BF16)
    in_specs = [row(D_MODEL), const((1, D_MODEL)), const((D_MODEL, 5 * W_MIX)),
                const((D_MODEL, 128)), const((1, FOX_HEADS)), const((W_MIX, 128))]
    args = [x2, g, w_main, w_f, b_f, head_sel]
    aliases = {}
    if kbuf is not None:
        in_specs += [pl.BlockSpec(memory_space=pl.ANY)] * 2
        args += [kbuf, vbuf]
        aliases = {6: 3, 7: 4}
    return pl.pallas_call(
        _inproj_kernel,
        grid=(n // tm,),
        in_specs=in_specs,
        out_specs=[row(W_MIX), pl.BlockSpec((LANE_BLOCKS, tm, 128), lambda i: (0, i, 0)), row(W_MIX),
                   stacked, stacked, row(W_MIX), row(W_MIX), row(FOX_HEADS),
                   pl.BlockSpec((1, 8, 128), lambda i: (i, 0, 0))],
        out_shape=[f32, jax.ShapeDtypeStruct((LANE_BLOCKS, n, 128), F32), b16, big, big, b16, b16,
                   jax.ShapeDtypeStruct((n, FOX_HEADS), F32),
                   jax.ShapeDtypeStruct((n // tm, 8, 128), F32)],
        input_output_aliases=aliases,
        compiler_params=_params("parallel"),
        name="in_proj",
    )(*args)


def _pool_kernel(u_ref, hist_ref, w_ref, sc_ref, y_ref, pn_ref, buf, *, tm, pos0):
    t = pl.program_id(1)

    @pl.when(t == 0)
    def _():
        buf[0:1, :] = jnp.zeros((1, W_MIX), F32)
        buf[1:16, :] = hist_ref[0]

    @pl.when(t > 0)
    def _():
        buf[0:16, :] = buf[tm:tm + 16, :]

    buf[16:16 + tm, :] = u_ref[0]
    pos = pos0 + t * tm + lax.broadcasted_iota(jnp.int32, (tm, 1), 0)
    for g, w in enumerate(POOL_WINDOWS):
        lo, hi = g * POOL_GROUP_W, (g + 1) * POOL_GROUP_W
        full = buf[:, lo:hi]
        s = full
        k = 1
        while k < w:
            s = s + pltpu.roll(s, k, axis=0)
            k *= 2
        cnt = jnp.minimum(pos + 1, w).astype(F32)
        d = s[16:] * (1.0 / cnt) - full[16:]
        y = jnp.dot(d.astype(BF16), w_ref[g], preferred_element_type=F32) * sc_ref[:, lo:hi]
        y_ref[0, :, lo:hi] = y.astype(BF16)
    pn_ref[0] = buf[tm + 1:tm + 16, :]


def _pool(u, hist, w_pool, scale, pos0, tm):
    nb, L, _ = u.shape
    return pl.pallas_call(
        functools.partial(_pool_kernel, tm=tm, pos0=pos0),
        grid=(nb, L // tm),
        in_specs=[
            pl.BlockSpec((1, tm, W_MIX), lambda b, t: (b, t, 0)),
            pl.BlockSpec((1, POOL_HIST, W_MIX), lambda b, t: (b, 0, 0)),
            pl.BlockSpec((4, POOL_GROUP_W, POOL_GROUP_W), lambda b, t: (0, 0, 0)),
            pl.BlockSpec((1, W_MIX), lambda b, t: (0, 0)),
        ],
        out_specs=[
            pl.BlockSpec((1, tm, W_MIX), lambda b, t: (b, t, 0)),
            pl.BlockSpec((1, POOL_HIST, W_MIX), lambda b, t: (b, 0, 0)),
        ],
        out_shape=[jax.ShapeDtypeStruct((nb, L, W_MIX), BF16),
                   jax.ShapeDtypeStruct((nb, POOL_HIST, W_MIX), F32)],
        scratch_shapes=[pltpu.VMEM((tm + 16, W_MIX), F32)],
        compiler_params=_params("parallel", "arbitrary"),
        name="pool_mixer",
    )(u, hist, w_pool, scale)


SSM_HALF_CH = 256
SSM_HALF_ST = 1024
SSM_ALL_ST = 2048
SSM_BLOCK = 4


def _ssm_blockdiag(lam_re, lam_im, log_dt, b_re, b_im, c_re, c_im):
    dt = jnp.exp(log_dt)[:, None]
    ar, ai = lam_re * dt, lam_im * dt

    def power(n):
        mag = jnp.exp(n * ar)
        return mag * jnp.cos(n * ai), mag * jnp.sin(n * ai)

    lbr, lbi = power(1.0)
    a16r, a16i = power(float(SSM_CHUNK))
    den = lam_re * lam_re + lam_im * lam_im
    fr = ((lbr - 1.0) * lam_re + lbi * lam_im) / den
    fi = (lbi * lam_re - (lbr - 1.0) * lam_im) / den
    bbr = fr[..., None] * b_re - fi[..., None] * b_im
    bbi = fr[..., None] * b_im + fi[..., None] * b_re
    ch = W_MIX // SSM_GROUPS

    def blockdiag(m, r, c):
        tiled = jnp.tile(m.reshape(SSM_GROUPS * r, c), (1, SSM_GROUPS))
        same = (jnp.arange(SSM_GROUPS * r)[:, None] // r) == (jnp.arange(SSM_GROUPS * c)[None, :] // c)
        return jnp.where(same, tiled, 0.0)

    def bd_in(m):
        return blockdiag(jnp.swapaxes(m, 1, 2), ch, SSM_STATE)

    def bd_out(m):
        return blockdiag(jnp.swapaxes(m, 1, 2), SSM_STATE, ch)

    def halves(m, rw, cw):
        return jnp.stack([m[h * rw:(h + 1) * rw, h * cw:(h + 1) * cw] for h in range(2)])

    bdb = jnp.concatenate([halves(bd_in(bbr), SSM_HALF_CH, SSM_HALF_ST),
                           halves(bd_in(bbi), SSM_HALF_CH, SSM_HALF_ST)], axis=2).astype(BF16)
    bdc = jnp.concatenate([halves(bd_out(c_re), SSM_HALF_ST, SSM_HALF_CH),
                           halves(bd_out(-c_im), SSM_HALF_ST, SSM_HALF_CH)], axis=1).astype(BF16)
    blk_r, blk_i = [], []
    for i in range(SSM_BLOCK):
        pr, pi = power(float(SSM_BLOCK - 1 - i))
        blk_r.append(halves(bd_in(pr[..., None] * bbr - pi[..., None] * bbi), SSM_HALF_CH, SSM_HALF_ST))
        blk_i.append(halves(bd_in(pr[..., None] * bbi + pi[..., None] * bbr), SSM_HALF_CH, SSM_HALF_ST))
    bd4 = jnp.concatenate([jnp.concatenate(blk_r, axis=1), jnp.concatenate(blk_i, axis=1)],
                          axis=2).astype(BF16)
    a4r, a4i = power(float(SSM_BLOCK))
    flat = lambda m: m.reshape(1, SSM_ALL_ST)
    return dict(bdb=bdb, bdc=bdc, bd4=bd4, lam_r=flat(lbr), lam_i=flat(lbi),
                a4r=flat(a4r), a4i=flat(a4i), a16r=flat(a16r), a16i=flat(a16i))


def _chunk_rows(ref, t):
    return pl.ds(t, ref.shape[1] // SSM_CHUNK, stride=SSM_CHUNK)


def _ssm_step(u_ref, bdb_ref, lr, li, t, h, s):
    u_t = _load_lane_blocks(u_ref, _chunk_rows(u_ref, t), 2 * h, 2 * h + 2)
    bu = jnp.dot(u_t.astype(BF16), bdb_ref[h], preferred_element_type=F32)
    bur, bui = bu[:, :SSM_HALF_ST], bu[:, SSM_HALF_ST:]
    sr, si = s
    return lr * sr - li * si + bur, lr * si + li * sr + bui


def _ssm_state_kernel(u_ref, bd4_ref, lr_ref, li_ref, vr_ref, vi_ref):
    for h in range(2):
        st = slice(h * SSM_HALF_ST, (h + 1) * SSM_HALF_ST)
        lr, li = lr_ref[:, st], li_ref[:, st]
        s = None
        for t in range(0, SSM_CHUNK, SSM_BLOCK):
            u4 = jnp.concatenate([_load_lane_blocks(u_ref, _chunk_rows(u_ref, t + i), 2 * h, 2 * h + 2)
                                  for i in range(SSM_BLOCK)], axis=1)
            bu = jnp.dot(u4.astype(BF16), bd4_ref[h], preferred_element_type=F32)
            bur, bui = bu[:, :SSM_HALF_ST], bu[:, SSM_HALF_ST:]
            s = (bur, bui) if s is None else (lr * s[0] - li * s[1] + bur, lr * s[1] + li * s[0] + bui)
        vr_ref[:, st] = s[0]
        vi_ref[:, st] = s[1]


def _ssm_state(u, sp, tc):
    r = u.shape[1] // SSM_CHUNK
    row = lambda w: pl.BlockSpec((tc, w), lambda i: (i, 0))
    const = lambda shape: pl.BlockSpec(shape, lambda i: (0,) * len(shape))
    out = jax.ShapeDtypeStruct((r, SSM_ALL_ST), F32)
    return pl.pallas_call(
        _ssm_state_kernel,
        grid=(r // tc,),
        in_specs=[pl.BlockSpec((LANE_BLOCKS, tc * SSM_CHUNK, 128), lambda i: (0, i, 0)),
                  const((2, SSM_BLOCK * SSM_HALF_CH, 2 * SSM_HALF_ST)),
                  const((1, SSM_ALL_ST)), const((1, SSM_ALL_ST))],
        out_specs=[row(SSM_ALL_ST), row(SSM_ALL_ST)],
        out_shape=[out, out],
        compiler_params=_params("parallel"),
        name="ssm_state",
    )(u, sp["bd4"], sp["a4r"], sp["a4i"])


def _ssm_carry_kernel(vr_ref, vi_ref, s0r_ref, s0i_ref, ar_ref, ai_ref,
                      sr_ref, si_ref, lr_ref, li_ref, *, nc):
    ar, ai = ar_ref[...], ai_ref[...]

    def body(c, s):
        sr, si = s
        sr_ref[0, pl.ds(c, 1), :] = sr
        si_ref[0, pl.ds(c, 1), :] = si
        return (ar * sr - ai * si + vr_ref[0, pl.ds(c, 1), :],
                ar * si + ai * sr + vi_ref[0, pl.ds(c, 1), :])

    sr, si = lax.fori_loop(0, nc, body, (s0r_ref[0], s0i_ref[0]))
    lr_ref[0] = sr
    li_ref[0] = si


def _ssm_carry(vr, vi, s0r, s0i, sp):
    nb, nc, _ = vr.shape
    seq = pl.BlockSpec((1, nc, SSM_ALL_ST), lambda b: (b, 0, 0))
    one = pl.BlockSpec((1, 1, SSM_ALL_ST), lambda b: (b, 0, 0))
    vec = pl.BlockSpec((1, SSM_ALL_ST), lambda b: (0, 0))
    return pl.pallas_call(
        functools.partial(_ssm_carry_kernel, nc=nc),
        grid=(nb,),
        in_specs=[seq, seq, one, one, vec, vec],
        out_specs=[seq, seq, one, one],
        out_shape=[jax.ShapeDtypeStruct((nb, nc, SSM_ALL_ST), F32)] * 2
                  + [jax.ShapeDtypeStruct((nb, 1, SSM_ALL_ST), F32)] * 2,
        compiler_params=_params("parallel"),
        name="ssm_carry",
    )(vr, vi, s0r, s0i, sp["a16r"], sp["a16i"])


def _ssm_out_kernel(u_ref, sr_ref, si_ref, bdb_ref, bdc_ref, lr_ref, li_ref, d_ref, w_ref, b_ref,
                    o_ref):
    lam, s = [], []
    for h in range(2):
        st = slice(h * SSM_HALF_ST, (h + 1) * SSM_HALF_ST)
        lam.append((lr_ref[:, st], li_ref[:, st]))
        s.append((sr_ref[:, st], si_ref[:, st]))
    for t in range(SSM_CHUNK):
        ys = []
        for h in range(2):
            s[h] = _ssm_step(u_ref, bdb_ref, lam[h][0], lam[h][1], t, h, s[h])
            ys.append(jnp.dot(s[h][0].astype(BF16), bdc_ref[h, :SSM_HALF_ST, :],
                              preferred_element_type=F32)
                      + jnp.dot(s[h][1].astype(BF16), bdc_ref[h, SSM_HALF_ST:, :],
                                preferred_element_type=F32))
        y = jnp.concatenate(ys, axis=1) + d_ref[...] * _load_lane_blocks(u_ref, _chunk_rows(u_ref, t))
        y = 0.5 * y * (1.0 + jnp.tanh(math.sqrt(2.0 / math.pi) * (y + 0.044715 * (y * y * y))))
        z = jnp.dot(y.astype(BF16), w_ref[...], preferred_element_type=F32) + b_ref[...]
        _store_lane_blocks(o_ref, _chunk_rows(o_ref, t), y * _sigmoid(z))


def _ssm_out(u, sr, si, sp, d_skip, w_glu, b_glu, tc):
    r = u.shape[1] // SSM_CHUNK
    row = lambda w: pl.BlockSpec((tc, w), lambda i: (i, 0))
    tok = pl.BlockSpec((LANE_BLOCKS, tc * SSM_CHUNK, 128), lambda i: (0, i, 0))
    const = lambda shape: pl.BlockSpec(shape, lambda i: (0,) * len(shape))
    return pl.pallas_call(
        _ssm_out_kernel,
        grid=(r // tc,),
        in_specs=[tok, row(SSM_ALL_ST), row(SSM_ALL_ST),
                  const((2, SSM_HALF_CH, 2 * SSM_HALF_ST)), const((2, 2 * SSM_HALF_ST, SSM_HALF_CH)),
                  const((1, SSM_ALL_ST)), const((1, SSM_ALL_ST)),
                  const((1, W_MIX)), const((W_MIX, W_MIX)), const((1, W_MIX))],
        out_specs=tok,
        out_shape=jax.ShapeDtypeStruct(u.shape, F32),
        compiler_params=_params("parallel"),
        name="ssm_out",
    )(u, sr, si, sp["bdb"], sp["bdc"], sp["lam_r"], sp["lam_i"], d_skip, w_glu, b_glu)


def _cumsum_kernel(lf_ref, c_ref, carry, *, tc):
    @pl.when(pl.program_id(1) == 0)
    def _():
        carry[...] = jnp.zeros_like(carry)

    r = lax.broadcasted_iota(jnp.int32, (tc, tc), 0)
    c = lax.broadcasted_iota(jnp.int32, (tc, tc), 1)
    tri = jnp.where(r >= c, 1.0, 0.0).astype(BF16)
    lf = lf_ref[0]
    hi = lf.astype(BF16)
    rest = lf - hi.astype(F32)
    mid = rest.astype(BF16)
    lo = (rest - mid.astype(F32)).astype(BF16)
    cs = sum(jnp.dot(tri, part, preferred_element_type=F32) for part in (lo, mid, hi)) + carry[...]
    c_ref[0] = cs
    carry[...] = cs[tc - 1:tc, :]


def _cumsum(lf, tc):
    nb, lk, _ = lf.shape
    blk = pl.BlockSpec((1, tc, FOX_HEADS), lambda b, t: (b, t, 0))
    return pl.pallas_call(
        functools.partial(_cumsum_kernel, tc=tc),
        grid=(nb, lk // tc),
        in_specs=[blk],
        out_specs=blk,
        out_shape=jax.ShapeDtypeStruct((nb, lk, FOX_HEADS), F32),
        scratch_shapes=[pltpu.VMEM((1, FOX_HEADS), F32)],
        compiler_params=_params("parallel", "arbitrary"),
        name="logf_cumsum",
    )(lf)


ATTN_ROWS = 16
DEAD_LOGIT = 120.0
NORM_MARGIN = 1.02


def _attn_kernel(*refs, tq, tk, q_off, new_len, n_past_tiles):
    b_tab, hp_tab, qi_tab, kj_tab, first_tab = refs[:5]
    if new_len:
        q_ref, k_ref, v_ref, cq_ref, ck_ref, kn_ref, vn_ref, ckn_ref, o_ref = refs[5:14]
    else:
        q_ref, k_ref, v_ref, cq_ref, ck_ref, o_ref = refs[5:11]
    s_scr, p_scr, m_scr, a_scr, cq_scr, acc_scr = refs[-6:]
    step_id = pl.program_id(0)
    qi = qi_tab[step_id]
    kj = kj_tab[step_id]
    q_lo = q_off + qi * tq
    k_lo = kj * tk
    lane = lax.broadcasted_iota(jnp.int32, (1, 2 * FOX_HEAD_DIM), 1)
    in_heads = [(lane >= hh * FOX_HEAD_DIM) & (lane < (hh + 1) * FOX_HEAD_DIM) for hh in range(2)]

    @pl.when(first_tab[step_id] != 0)
    def _():
        m_scr[...] = jnp.full_like(m_scr, NEG_INF)
        acc_scr[...] = jnp.zeros_like(acc_scr)
        cq = cq_ref[0, 0] * LOG2E
        for hh in range(2):
            cq_scr[hh] = jnp.broadcast_to(cq[:, hh:hh + 1], (tq, 128))

    q = q_ref[0]

    def softmax_rows(hh, masked, ck, cols):
        ckb = [jnp.broadcast_to(ck[hh:hh + 1, lo:hi], (ATTN_ROWS, hi - lo)) for lo, hi in cols]
        for r in range(0, tq, ATTN_ROWS):
            t = [s_scr[hh, r:r + ATTN_ROWS, lo:hi] - ckb[c] for c, (lo, hi) in enumerate(cols)]
            if masked:
                qpos = q_lo + r + lax.broadcasted_iota(jnp.int32, (ATTN_ROWS, 128), 0)
                kpos = k_lo + lax.broadcasted_iota(jnp.int32, (ATTN_ROWS, 128), 1)
                t = [jnp.where((kpos + lo <= qpos)[:, :hi - lo], t[c], NEG_INF)
                     for c, (lo, hi) in enumerate(cols)]
            tmax = None
            for c, (lo, hi) in enumerate(cols):
                if hi - lo == 128:
                    tmax = t[c] if tmax is None else jnp.maximum(tmax, t[c])
            if tmax is not None:
                tmax = jnp.max(tmax, axis=1, keepdims=True)
            if cols[-1][1] - cols[-1][0] < 128:
                ragged = jnp.max(t[-1], axis=1, keepdims=True)
                tmax = ragged if tmax is None else jnp.maximum(tmax, ragged)
            tmax = jnp.broadcast_to(tmax, (ATTN_ROWS, 128))
            cq2 = cq_scr[hh, r:r + ATTN_ROWS, :]
            m_old = m_scr[hh, r:r + ATTN_ROWS, :]
            m_new = jnp.maximum(m_old, tmax + cq2)
            m_scr[hh, r:r + ATTN_ROWS, :] = m_new
            a_scr[hh, r:r + ATTN_ROWS, :] = jnp.exp2(m_old - m_new)
            mt = m_new - cq2
            for c, (lo, hi) in enumerate(cols):
                p_scr[hh, r:r + ATTN_ROWS, lo:hi] = jnp.exp2(t[c] - mt[:, :hi - lo]).astype(BF16)

    def step(masked, kr, vr, ckr, width):
        cols = [(c, min(c + 128, width)) for c in range(0, width, 128)]
        kb = kr[0].astype(BF16)
        v = vr[0]
        ck = ckr[0, 0] * LOG2E
        for hh in range(2):
            qm = jnp.where(in_heads[hh], q, jnp.zeros_like(q))
            s_scr[hh, :, :width] = lax.dot_general(qm, kb, (((1,), (1,)), ((), ())),
                                                   preferred_element_type=F32)
        for hh in range(2):
            softmax_rows(hh, masked, ck, cols)
            vh = jnp.where(in_heads[hh], v, 1.0).astype(BF16)
            pv = jnp.dot(p_scr[hh, :, :width], vh, preferred_element_type=F32)
            acc_scr[hh] = a_scr[hh] * acc_scr[hh] + pv

    def finish():
        o = []
        for hh in range(2):
            acc = acc_scr[hh]
            o.append(acc * (1.0 / pltpu.roll(acc, FOX_HEAD_DIM, axis=1)))
        o_ref[0] = jnp.where(lane < FOX_HEAD_DIM, o[0], o[1]).astype(BF16)

    is_new = (kj == n_past_tiles) if new_len else False
    needs_mask = k_lo + tk - 1 > q_lo

    @pl.when(jnp.logical_not(is_new) & needs_mask)
    def _():
        step(True, k_ref, v_ref, ck_ref, tk)

    @pl.when(jnp.logical_not(is_new) & jnp.logical_not(needs_mask))
    def _():
        step(False, k_ref, v_ref, ck_ref, tk)

    if new_len:
        @pl.when(is_new)
        def _():
            step(True, kn_ref, vn_ref, ckn_ref, new_len)
            finish()
    else:
        pl.when(kj == (q_lo + tq - 1) // tk)(finish)


def _live_tiles(qk_stats, c_all, nb, nq, t):
    qmax = jnp.sqrt(qk_stats[:, 0, :FOX_HEADS]).reshape(nb, nq, FOX_HEADS) * NORM_MARGIN
    kmax = jnp.sqrt(qk_stats[:, 1, :FOX_HEADS]).reshape(nb, nq, FOX_HEADS) * NORM_MARGIN
    c_first = c_all[:, 0::t, :]
    c_last = c_all[:, t - 1::t, :]
    bound = (qmax[:, :, None, :] * (kmax[:, None, :, :] + kmax[:, :, None, :])
             + c_first[:, :, None, :] - c_last[:, None, :, :])
    dead = jnp.all((bound < -DEAD_LOGIT).reshape(nb, nq, nq, HEAD_PAIRS, 2), axis=-1)
    below = (jnp.arange(nq)[None, :] < jnp.arange(nq)[:, None])[None, :, :, None]
    return jnp.transpose(jnp.logical_not(dead & below), (0, 3, 1, 2))


def _attention(q, k_all, v_all, cq, ck, q_off, tq, tk, live=None, new=None):
    nb, lq, _ = q.shape
    nq = lq // tq
    n_past_tiles = k_all.shape[1] // tk
    new_len = 0 if new is None else new[0].shape[1]
    assert new is None or (nq == 1 and q_off == n_past_tiles * tk and new_len == tq)
    nk = n_past_tiles + (1 if new_len else 0)
    last = np.full(nq, nk - 1) if new_len else (q_off + np.arange(nq) * tq + tq - 1) // tk
    causal = np.arange(nk)[None, :] <= last[:, None]
    valid = jnp.broadcast_to(causal if live is None else live & causal, (nb, HEAD_PAIRS, nq, nk))
    first = valid & (jnp.cumsum(valid.astype(jnp.int32), axis=3) == 1)
    max_steps = nb * HEAD_PAIRS * int(causal.sum())
    (idx,) = jnp.nonzero(valid.reshape(-1), size=max_steps, fill_value=0)
    idx = idx.astype(jnp.int32)
    n_steps = jnp.sum(valid, dtype=jnp.int32)
    kj_tab = idx % nk
    qi_tab = (idx // nk) % nq
    hp_tab = (idx // (nk * nq)) % HEAD_PAIRS
    b_tab = idx // (nk * nq * HEAD_PAIRS)
    first_tab = first.reshape(-1)[idx].astype(jnp.int32)
    past_tile = lambda kt, i: jnp.minimum(kt[i], n_past_tiles - 1)
    qmap = lambda i, bt, ht, qt, kt, ft: (bt[i], qt[i], ht[i])
    kmap = lambda i, bt, ht, qt, kt, ft: (bt[i], past_tile(kt, i), ht[i])
    in_specs = [
        pl.BlockSpec((1, tq, 128), qmap),
        pl.BlockSpec((1, tk, 128), kmap),
        pl.BlockSpec((1, tk, 128), kmap),
        pl.BlockSpec((1, 1, tq, 2), lambda i, bt, ht, qt, kt, ft: (bt[i], ht[i], qt[i], 0)),
        pl.BlockSpec((1, 1, 2, tk), lambda i, bt, ht, qt, kt, ft: (bt[i], ht[i], 0, past_tile(kt, i))),
    ]
    args = [q, k_all, v_all, cq, ck]
    if new_len:
        newmap = lambda i, bt, ht, qt, kt, ft: (bt[i], 0, ht[i])
        in_specs += [pl.BlockSpec((1, new_len, 128), newmap), pl.BlockSpec((1, new_len, 128), newmap),
                     pl.BlockSpec((1, 1, 2, new_len), lambda i, bt, ht, qt, kt, ft: (bt[i], ht[i], 0, 0))]
        args += list(new)
    width = max(tk, new_len)
    return pl.pallas_call(
        functools.partial(_attn_kernel, tq=tq, tk=tk, q_off=q_off, new_len=new_len,
                          n_past_tiles=n_past_tiles),
        grid_spec=pltpu.PrefetchScalarGridSpec(
            num_scalar_prefetch=5,
            grid=(n_steps,),
            in_specs=in_specs,
            out_specs=pl.BlockSpec((1, tq, 128), qmap),
            scratch_shapes=[pltpu.VMEM((2, tq, width), F32), pltpu.VMEM((2, tq, width), BF16),
                            pltpu.VMEM((2, tq, 128), F32), pltpu.VMEM((2, tq, 128), F32),
                            pltpu.VMEM((2, tq, 128), F32), pltpu.VMEM((2, tq, 128), F32)],
        ),
        out_shape=jax.ShapeDtypeStruct((nb, lq, W_MIX), BF16),
        compiler_params=_params("arbitrary"),
        name="fox_attention",
    )(b_tab, hp_tab, qi_tab, kj_tab, first_tab, *args)


def _merge_kernel(x_ref, g_ref, ya_ref, yb_ref, yc_ref, wg_ref, wb_ref, wo_ref, o_ref):
    x = x_ref[...]
    hb = _rms_bf16(x, g_ref[...])
    merged = None
    for b, y_ref in enumerate((ya_ref, yb_ref, yc_ref)):
        gate = _sigmoid(jnp.dot(hb, wg_ref[:, b * D_MODEL:(b + 1) * D_MODEL],
                                preferred_element_type=F32))
        y = _load_lane_blocks(y_ref, slice(None)) if b == 1 else y_ref[...]
        term = gate * jnp.dot(y.astype(BF16), wb_ref[b], preferred_element_type=F32)
        merged = term if merged is None else merged + term
    o_ref[...] = x + jnp.dot(merged.astype(BF16), wo_ref[...], preferred_element_type=F32)


def _merge(x2, g, ya, yb, yc, w_gate, w_branch, w_out, tm):
    n = x2.shape[0]
    row = lambda w: pl.BlockSpec((tm, w), lambda i: (i, 0))
    const = lambda shape: pl.BlockSpec(shape, lambda i: (0,) * len(shape))
    return pl.pallas_call(
        _merge_kernel,
        grid=(n // tm,),
        in_specs=[row(D_MODEL), const((1, D_MODEL)), row(W_MIX),
                  pl.BlockSpec((LANE_BLOCKS, tm, 128), lambda i: (0, i, 0)), row(W_MIX),
                  const((D_MODEL, 3 * D_MODEL)), const((3, W_MIX, D_MODEL)),
                  const((D_MODEL, D_MODEL))],
        out_specs=row(D_MODEL),
        out_shape=jax.ShapeDtypeStruct((n, D_MODEL), F32),
        compiler_params=_params("parallel"),
        name="branch_merge",
    )(x2, g, ya, yb, yc, w_gate, w_branch, w_out)


FFN_HALO = 16
FFN_SUB = 256


def _ffn_kernel(x_ref, xh_ref, g_ref, hist_ref, wug_ref, wuv_ref, cwg_ref, cwv_ref,
                cbg_ref, cbv_ref, wd_ref, o_ref, cn_ref, h_scr, hh_scr, acc_scr, act_scr,
                *, tm, tn, nj):
    t = pl.program_id(1)
    j = pl.program_id(2)

    @pl.when(j == 0)
    def _():
        h_scr[...] = _rms_bf16(x_ref[...], g_ref[...])
        hh_scr[...] = _rms_bf16(xh_ref[...], g_ref[...])

    first = t == 0
    for lo in range(0, tn, FFN_SUB):
        hi = min(lo + FFN_SUB, tn)
        conv = []
        for hf, (wu_ref, cw_ref, cb_ref) in enumerate(((wug_ref, cwg_ref, cbg_ref),
                                                       (wuv_ref, cwv_ref, cbv_ref))):
            w = wu_ref[:, lo:hi]
            up = jnp.dot(h_scr[...], w, preferred_element_type=F32)
            uph = jnp.dot(hh_scr[...], w, preferred_element_type=F32)
            prev = jnp.where(first, hist_ref[0, hf, :, lo:hi], uph[FFN_HALO - 8:, :])
            full = jnp.concatenate([prev, up], axis=0)
            cn_ref[0, hf, j, :, lo:hi] = up[tm - 2:, :]
            cw = cw_ref[:, lo:hi]
            conv.append(cb_ref[:, lo:hi] + full[6:6 + tm, :] * cw[0:1, :]
                        + full[7:7 + tm, :] * cw[1:2, :] + up * cw[2:3, :])
        gate, val = conv
        act_scr[:, lo:hi] = (gate * _sigmoid(gate) * val).astype(BF16)
    down = jnp.dot(act_scr[...], wd_ref[...], preferred_element_type=F32)

    if nj == 1:
        o_ref[...] = x_ref[...] + down
        return

    @pl.when(j == 0)
    def _():
        acc_scr[...] = down

    @pl.when((j > 0) & (j < nj - 1))
    def _():
        acc_scr[...] += down

    @pl.when(j == nj - 1)
    def _():
        o_ref[...] = x_ref[...] + acc_scr[...] + down


def _ffn(x2, g, hist, wug, wuv, cwg, cwv, cbg, cbv, wd, nb, L, tm, tn):
    n = x2.shape[0]
    nt, nj = L // tm, D_FF // tn
    hpt = tm // FFN_HALO
    row = lambda b, t, j: (b * nt + t, 0)
    col = lambda b, t, j: (0, j)
    resident = dict(pipeline_mode=pl.Buffered(1)) if nj == 1 else {}
    return pl.pallas_call(
        functools.partial(_ffn_kernel, tm=tm, tn=tn, nj=nj),
        grid=(nb, nt, nj),
        in_specs=[
            pl.BlockSpec((tm, D_MODEL), row),
            pl.BlockSpec((FFN_HALO, D_MODEL), lambda b, t, j: (jnp.maximum((b * nt + t) * hpt - 1, 0), 0)),
            pl.BlockSpec((1, D_MODEL), lambda b, t, j: (0, 0)),
            pl.BlockSpec((1, 2, 8, tn), lambda b, t, j: (b, 0, 0, j)),
            pl.BlockSpec((D_MODEL, tn), col, **resident),
            pl.BlockSpec((D_MODEL, tn), col, **resident),
            pl.BlockSpec((3, tn), col),
            pl.BlockSpec((3, tn), col),
            pl.BlockSpec((1, tn), col),
            pl.BlockSpec((1, tn), col),
            pl.BlockSpec((tn, D_MODEL), lambda b, t, j: (j, 0), **resident),
        ],
        out_specs=[pl.BlockSpec((tm, D_MODEL), row),
                   pl.BlockSpec((1, 2, nj, 2, tn), lambda b, t, j: (b, 0, 0, 0, 0))],
        out_shape=[jax.ShapeDtypeStruct((n, D_MODEL), F32),
                   jax.ShapeDtypeStruct((nb, 2, nj, 2, tn), F32)],
        scratch_shapes=[pltpu.VMEM((tm, D_MODEL), BF16), pltpu.VMEM((FFN_HALO, D_MODEL), BF16),
                        pltpu.VMEM((tm, D_MODEL) if nj > 1 else (8, 128), F32),
                        pltpu.VMEM((tm, tn), BF16)],
        compiler_params=_params("arbitrary", "arbitrary", "arbitrary"),
        name="conv_ffn",
    )(x2, x2, g, hist, wug, wuv, cwg, cwv, cbg, cbv, wd)


def _final_norm_kernel(x_ref, g_ref, o_ref):
    x = x_ref[...]
    ms = jnp.mean(x * x, axis=-1, keepdims=True)
    o_ref[...] = x * lax.rsqrt(ms + EPS) * g_ref[...]


def _final_norm(x2, g, tm):
    n = x2.shape[0]
    row = pl.BlockSpec((tm, D_MODEL), lambda i: (i, 0))
    return pl.pallas_call(
        _final_norm_kernel,
        grid=(n // tm,),
        in_specs=[row, pl.BlockSpec((1, D_MODEL), lambda i: (0, 0))],
        out_specs=row,
        out_shape=jax.ShapeDtypeStruct((n, D_MODEL), F32),
        compiler_params=_params("parallel"),
        name="final_norm",
    )(x2, g)


def _layer(x2, nb, L, pos0, pool_hist, s0, k_past, v_past, logf_past, conv_hist, p,
           layer, depth, kbuf, vbuf):
    n = nb * L
    tm = _tile(L, 512)
    u_pool, u_ssm, q, kbuf, vbuf, kb, vb, logf, qk_stats = _inproj(
        x2, p["g_mix"], p["w_main"], p["w_f"], p["b_f"], tm, layer, depth, kbuf, vbuf)

    y_a, pool_new = _pool(u_pool.reshape(nb, L, W_MIX), pool_hist, p["w_pool"], p["pool_scale"],
                          pos0, tm)

    nc = L // SSM_CHUNK
    sp = p["ssm"]
    tc = _tile(nb * nc, 128)
    vr, vi = _ssm_state(u_ssm, sp, tc)
    sr, si, sl_re, sl_im = _ssm_carry(vr.reshape(nb, nc, SSM_ALL_ST), vi.reshape(nb, nc, SSM_ALL_ST),
                                      s0[0], s0[1], sp)
    y_b = _ssm_out(u_ssm, sr.reshape(nb * nc, SSM_ALL_ST), si.reshape(nb * nc, SSM_ALL_ST), sp,
                   p["d_skip"], p["w_glu"], p["b_glu"], tc)

    q3, kb3, vb3 = (a.reshape(nb, L, W_MIX) for a in (q, kb, vb))
    logf3 = logf.reshape(nb, L, FOX_HEADS)
    logf_all = logf3 if k_past is None else jnp.concatenate([logf_past, logf3], axis=1)
    lk = logf_all.shape[1]
    c_all = _cumsum(logf_all, _tile(lk, 512))
    cq = jnp.transpose(c_all[:, lk - L:].reshape(nb, L, HEAD_PAIRS, 2), (0, 2, 1, 3))
    ck = jnp.transpose(c_all.reshape(nb, lk, HEAD_PAIRS, 2), (0, 2, 3, 1))
    if k_past is None:
        tk = _tile(L, 512)
        live = _live_tiles(qk_stats, c_all, nb, L // tm, tm) if tk == tm else None
        y_c = _attention(q3, kb3, vb3, cq, ck, pos0, tm, tk, live)
    else:
        past = k_past.shape[1]
        y_c = _attention(q3, k_past, v_past, cq, ck[..., :past], pos0, tm, _tile(past, 2048, 128),
                         new=(kb3, vb3, ck[..., past:]))

    x2 = _merge(x2, p["g_mix"], y_a.reshape(n, W_MIX), y_b, y_c.reshape(n, W_MIX),
                p["w_gate"], p["w_branch"], p["w_out"], tm)
    x2, conv_new = _ffn(x2, p["g_ffn"], conv_hist, p["wug"], p["wuv"], p["cwg"], p["cwv"],
                        p["cbg"], p["cbv"], p["w_down"], nb, L, tm, D_FF)
    states = (pool_new,
              sl_re.reshape(nb, SSM_GROUPS, SSM_STATE), sl_im.reshape(nb, SSM_GROUPS, SSM_STATE),
              logf3,
              jnp.transpose(conv_new, (0, 3, 1, 2, 4)).reshape(nb, 2, 2 * D_FF))
    return x2, states, kbuf, vbuf


def kernel(x_prompt, x_sample, cache_pool, state_ssm_re, state_ssm_im, cache_k, cache_v, cache_logf, cache_ffn_conv, norm_mix_g, w_in, b_f, w_pool, pool_scale, lam_re, lam_im, log_dt, b_re, b_im, c_re, c_im, d_skip, w_glu, b_glu, w_branch, w_out, norm_ffn_g, w_up, conv_w, conv_b, w_down, norm_final_g):
    bp, lp, _ = x_prompt.shape
    bs, ls, _ = x_sample.shape
    depth = w_in.shape[0]
    past = cache_k.shape[2]
    col_f = 5 * W_MIX
    col_gate = col_f + FOX_HEADS

    hp = x_prompt.reshape(bp * lp, D_MODEL)
    hs = x_sample.reshape(bs * ls, D_MODEL)
    zero_pool = jnp.zeros((bp, POOL_HIST, W_MIX), F32)
    zero_s = (jnp.zeros((bp, 1, SSM_ALL_ST), F32),) * 2
    zero_conv = jnp.zeros((bp, 2, 8, D_FF), F32)

    st_p, st_s = [], []
    kp = vp = ks = vs = None
    for l in range(depth):
        p = dict(
            ssm=_ssm_blockdiag(lam_re[l], lam_im[l], log_dt[l], b_re[l], b_im[l], c_re[l], c_im[l]),
            g_mix=norm_mix_g[l][None], g_ffn=norm_ffn_g[l][None],
            w_main=w_in[l][:, :col_f].astype(BF16),
            w_f=jnp.pad(w_in[l][:, col_f:col_gate], ((0, 0), (0, 128 - FOX_HEADS))).astype(BF16),
            b_f=b_f[l][None],
            w_gate=w_in[l][:, col_gate:].astype(BF16),
            w_pool=w_pool[l].astype(BF16), pool_scale=pool_scale[l][None],
            d_skip=d_skip[l][None], w_glu=w_glu[l].astype(BF16), b_glu=b_glu[l][None],
            w_branch=w_branch[l].astype(BF16), w_out=w_out[l].astype(BF16),
            wug=w_up[l][:, :D_FF].astype(BF16), wuv=w_up[l][:, D_FF:].astype(BF16),
            cwg=conv_w[l][:, :D_FF], cwv=conv_w[l][:, D_FF:],
            cbg=conv_b[l][None, :D_FF], cbv=conv_b[l][None, D_FF:],
            w_down=w_down[l].astype(BF16),
        )
        hp, sp, kp, vp = _layer(hp, bp, lp, 0, zero_pool, zero_s, None, None, None, zero_conv, p,
                                l, depth, kp, vp)
        s0 = (state_ssm_re[l].reshape(bs, 1, SSM_ALL_ST), state_ssm_im[l].reshape(bs, 1, SSM_ALL_ST))
        conv_hist = jnp.pad(jnp.transpose(cache_ffn_conv[l].reshape(bs, 2, 2, D_FF), (0, 2, 1, 3)),
                            ((0, 0), (0, 0), (6, 0), (0, 0)))
        hs, ss, ks, vs = _layer(hs, bs, ls, past, cache_pool[l], s0,
                                cache_k[l].reshape(bs, past, W_MIX), cache_v[l].reshape(bs, past, W_MIX),
                                cache_logf[l], conv_hist, p, l, depth, ks, vs)
        st_p.append(sp)
        st_s.append(ss)

    y_prompt = _final_norm(hp, norm_final_g[None], _tile(bp * lp, 512)).reshape(bp, lp, D_MODEL)
    y_sample = _final_norm(hs, norm_final_g[None], _tile(bs * ls, 512)).reshape(bs, ls, D_MODEL)
    stack = lambda st, i: jnp.stack([s[i] for s in st], axis=0)
    heads_p = (depth, bp, lp, FOX_HEADS, FOX_HEAD_DIM)
    heads_s = (depth, bs, ls, FOX_HEADS, FOX_HEAD_DIM)
    return (y_prompt, y_sample,
            stack(st_p, 0), stack(st_s, 0), stack(st_p, 1), stack(st_s, 1),
            stack(st_p, 2), stack(st_s, 2),
            kp.reshape(heads_p), ks.reshape(heads_s), vp.reshape(heads_p), vs.reshape(heads_s),
            stack(st_p, 3), stack(st_s, 3), stack(st_p, 4), stack(st_s, 4))
```

```python
import functools
import math

import jax
import jax.numpy as jnp
import numpy as np
from jax import lax
from jax.experimental import pallas as pl
from jax.experimental.pallas import tpu as pltpu

F32 = jnp.float32
BF16 = jnp.bfloat16

D_MODEL = 1024
W_MIX = 512
POOL_WINDOWS = (2, 4, 8, 16)
POOL_GROUP_W = 128
POOL_HIST = 15
SSM_GROUPS = 32
SSM_STATE = 64
SSM_CHUNK = 16
FOX_HEADS = 8
FOX_HEAD_DIM = 64
HEAD_PAIRS = FOX_HEADS // 2
D_FF = 2816
EPS = 1e-6
NEG_INF = -1e30
VMEM_LIMIT = 56 * 1024 * 1024


def _params(*sem):
    return pltpu.CompilerParams(dimension_semantics=sem, vmem_limit_bytes=VMEM_LIMIT)


def _tile(n, target, mult=8):
    if n <= target:
        return n
    for t in range(target, 0, -1):
        if n % t == 0 and t % mult == 0:
            return t
    return n


def _rms_bf16(x, g):
    ms = jnp.mean(x * x, axis=-1, keepdims=True)
    return (x * lax.rsqrt(ms + EPS) * g).astype(BF16)


def _sigmoid(x):
    return 1.0 / (1.0 + jnp.exp(-x))


LANE_BLOCKS = W_MIX // 128
LOG2E = math.log2(math.e)


def _load_lane_blocks(ref, rows, lo=0, hi=LANE_BLOCKS):
    return jnp.concatenate([ref[j, rows, :] for j in range(lo, hi)], axis=1)


def _store_lane_blocks(ref, rows, val):
    for j in range(LANE_BLOCKS):
        ref[j, rows, :] = val[:, j * 128:(j + 1) * 128]


def _inproj_kernel(*refs):
    x_ref, g_ref, w_ref, wf_ref, bf_ref, sel_ref = refs[:6]
    up_ref, us_ref, q_ref, k_ref, v_ref, kb_ref, vb_ref, lf_ref, st_ref = refs[-9:]
    hb = _rms_bf16(x_ref[...], g_ref[...])

    def proj(idx):
        return jnp.dot(hb, w_ref[:, idx * W_MIX:(idx + 1) * W_MIX], preferred_element_type=F32)

    def head_norm2_max(a):
        return jnp.max(jnp.dot((a * a).astype(BF16), sel_ref[...], preferred_element_type=F32),
                       axis=0, keepdims=True)

    up_ref[...] = proj(0)
    _store_lane_blocks(us_ref, slice(None), proj(1))
    qn = proj(2) * (FOX_HEAD_DIM ** -0.5)
    q_ref[...] = (qn * LOG2E).astype(BF16)
    k = proj(3)
    k_ref[0] = k
    kb_ref[...] = k.astype(BF16)
    row = lax.broadcasted_iota(jnp.int32, (8, 128), 0)
    st_ref[0] = jnp.where(row == 0, head_norm2_max(qn), jnp.where(row == 1, head_norm2_max(k), 0.0))
    v = proj(4)
    v_ref[0] = v
    vb_ref[...] = v.astype(BF16)
    zf = jnp.dot(hb, wf_ref[...], preferred_element_type=F32)
    a = zf[:, :FOX_HEADS] + bf_ref[...]
    lf_ref[...] = jnp.minimum(a, 0.0) - jnp.log(1.0 + jnp.exp(-jnp.abs(a)))


def _inproj(x2, g, w_main, w_f, b_f, tm, layer, depth, kbuf, vbuf):
    n = x2.shape[0]
    row = lambda w: pl.BlockSpec((tm, w), lambda i: (i, 0))
    const = lambda shape: pl.BlockSpec(shape, lambda i: (0,) * len(shape))
    stacked = pl.BlockSpec((1, tm, W_MIX), lambda i: (layer, i, 0))
    f32 = jax.ShapeDtypeStruct((n, W_MIX), F32)
    b16 = jax.ShapeDtypeStruct((n, W_MIX), BF16)
    big = jax.ShapeDtypeStruct((depth, n, W_MIX), F32)
    head_sel = (jnp.arange(W_MIX)[:, None] // FOX_HEAD_DIM == jnp.arange(128)[None, :]).astype(BF16)
    in_specs = [row(D_MODEL), const((1, D_MODEL)), const((D_MODEL, 5 * W_MIX)),
                const((D_MODEL, 128)), const((1, FOX_HEADS)), const((W_MIX, 128))]
    args = [x2, g, w_main, w_f, b_f, head_sel]
    aliases = {}
    if kbuf is not None:
        in_specs += [pl.BlockSpec(memory_space=pl.ANY)] * 2
        args += [kbuf, vbuf]
        aliases = {6: 3, 7: 4}
    return pl.pallas_call(
        _inproj_kernel,
        grid=(n // tm,),
        in_specs=in_specs,
        out_specs=[row(W_MIX), pl.BlockSpec((LANE_BLOCKS, tm, 128), lambda i: (0, i, 0)), row(W_MIX),
                   stacked, stacked, row(W_MIX), row(W_MIX), row(FOX_HEADS),
                   pl.BlockSpec((1, 8, 128), lambda i: (i, 0, 0))],
        out_shape=[f32, jax.ShapeDtypeStruct((LANE_BLOCKS, n, 128), F32), b16, big, big, b16, b16,
                   jax.ShapeDtypeStruct((n, FOX_HEADS), F32),
                   jax.ShapeDtypeStruct((n // tm, 8, 128), F32)],
        input_output_aliases=aliases,
        compiler_params=_params("parallel"),
        name="in_proj",
    )(*args)


def _pool_kernel(u_ref, hist_ref, w_ref, sc_ref, y_ref, pn_ref, buf, *, tm, pos0):
    t = pl.program_id(1)

    @pl.when(t == 0)
    def _():
        buf[0:1, :] = jnp.zeros((1, W_MIX), F32)
        buf[1:16, :] = hist_ref[0]

    @pl.when(t > 0)
    def _():
        buf[0:16, :] = buf[tm:tm + 16, :]

    buf[16:16 + tm, :] = u_ref[0]
    pos = pos0 + t * tm + lax.broadcasted_iota(jnp.int32, (tm, 1), 0)
    for g, w in enumerate(POOL_WINDOWS):
        lo, hi = g * POOL_GROUP_W, (g + 1) * POOL_GROUP_W
        full = buf[:, lo:hi]
        s = full
        k = 1
        while k < w:
            s = s + pltpu.roll(s, k, axis=0)
            k *= 2
        cnt = jnp.minimum(pos + 1, w).astype(F32)
        d = s[16:] * (1.0 / cnt) - full[16:]
        y = jnp.dot(d.astype(BF16), w_ref[g], preferred_element_type=F32) * sc_ref[:, lo:hi]
        y_ref[0, :, lo:hi] = y.astype(BF16)
    pn_ref[0] = buf[tm + 1:tm + 16, :]


def _pool(u, hist, w_pool, scale, pos0, tm):
    nb, L, _ = u.shape
    return pl.pallas_call(
        functools.partial(_pool_kernel, tm=tm, pos0=pos0),
        grid=(nb, L // tm),
        in_specs=[
            pl.BlockSpec((1, tm, W_MIX), lambda b, t: (b, t, 0)),
            pl.BlockSpec((1, POOL_HIST, W_MIX), lambda b, t: (b, 0, 0)),
            pl.BlockSpec((4, POOL_GROUP_W, POOL_GROUP_W), lambda b, t: (0, 0, 0)),
            pl.BlockSpec((1, W_MIX), lambda b, t: (0, 0)),
        ],
        out_specs=[
            pl.BlockSpec((1, tm, W_MIX), lambda b, t: (b, t, 0)),
            pl.BlockSpec((1, POOL_HIST, W_MIX), lambda b, t: (b, 0, 0)),
        ],
        out_shape=[jax.ShapeDtypeStruct((nb, L, W_MIX), BF16),
                   jax.ShapeDtypeStruct((nb, POOL_HIST, W_MIX), F32)],
        scratch_shapes=[pltpu.VMEM((tm + 16, W_MIX), F32)],
        compiler_params=_params("parallel", "arbitrary"),
        name="pool_mixer",
    )(u, hist, w_pool, scale)


SSM_HALF_CH = 256
SSM_HALF_ST = 1024
SSM_ALL_ST = 2048
SSM_BLOCK = 4


def _ssm_blockdiag(lam_re, lam_im, log_dt, b_re, b_im, c_re, c_im):
    dt = jnp.exp(log_dt)[:, None]
    ar, ai = lam_re * dt, lam_im * dt

    def power(n):
        mag = jnp.exp(n * ar)
        return mag * jnp.cos(n * ai), mag * jnp.sin(n * ai)

    lbr, lbi = power(1.0)
    a16r, a16i = power(float(SSM_CHUNK))
    den = lam_re * lam_re + lam_im * lam_im
    fr = ((lbr - 1.0) * lam_re + lbi * lam_im) / den
    fi = (lbi * lam_re - (lbr - 1.0) * lam_im) / den
    bbr = fr[..., None] * b_re - fi[..., None] * b_im
    bbi = fr[..., None] * b_im + fi[..., None] * b_re
    ch = W_MIX // SSM_GROUPS

    def blockdiag(m, r, c):
        tiled = jnp.tile(m.reshape(SSM_GROUPS * r, c), (1, SSM_GROUPS))
        same = (jnp.arange(SSM_GROUPS * r)[:, None] // r) == (jnp.arange(SSM_GROUPS * c)[None, :] // c)
        return jnp.where(same, tiled, 0.0)

    def bd_in(m):
        return blockdiag(jnp.swapaxes(m, 1, 2), ch, SSM_STATE)

    def bd_out(m):
        return blockdiag(jnp.swapaxes(m, 1, 2), SSM_STATE, ch)

    def halves(m, rw, cw):
        return jnp.stack([m[h * rw:(h + 1) * rw, h * cw:(h + 1) * cw] for h in range(2)])

    bdb = jnp.concatenate([halves(bd_in(bbr), SSM_HALF_CH, SSM_HALF_ST),
                           halves(bd_in(bbi), SSM_HALF_CH, SSM_HALF_ST)], axis=2).astype(BF16)
    bdc = jnp.concatenate([halves(bd_out(c_re), SSM_HALF_ST, SSM_HALF_CH),
                           halves(bd_out(-c_im), SSM_HALF_ST, SSM_HALF_CH)], axis=1).astype(BF16)
    blk_r, blk_i = [], []
    for i in range(SSM_BLOCK):
        pr, pi = power(float(SSM_BLOCK - 1 - i))
        blk_r.append(halves(bd_in(pr[..., None] * bbr - pi[..., None] * bbi), SSM_HALF_CH, SSM_HALF_ST))
        blk_i.append(halves(bd_in(pr[..., None] * bbi + pi[..., None] * bbr), SSM_HALF_CH, SSM_HALF_ST))
    bd4 = jnp.concatenate([jnp.concatenate(blk_r, axis=1), jnp.concatenate(blk_i, axis=1)],
                          axis=2).astype(BF16)
    a4r, a4i = power(float(SSM_BLOCK))
    flat = lambda m: m.reshape(1, SSM_ALL_ST)
    return dict(bdb=bdb, bdc=bdc, bd4=bd4, lam_r=flat(lbr), lam_i=flat(lbi),
                a4r=flat(a4r), a4i=flat(a4i), a16r=flat(a16r), a16i=flat(a16i))


def _chunk_rows(ref, t):
    return pl.ds(t, ref.shape[1] // SSM_CHUNK, stride=SSM_CHUNK)


def _ssm_step(u_ref, bdb_ref, lr, li, t, h, s):
    u_t = _load_lane_blocks(u_ref, _chunk_rows(u_ref, t), 2 * h, 2 * h + 2)
    bu = jnp.dot(u_t.astype(BF16), bdb_ref[h], preferred_element_type=F32)
    bur, bui = bu[:, :SSM_HALF_ST], bu[:, SSM_HALF_ST:]
    sr, si = s
    return lr * sr - li * si + bur, lr * si + li * sr + bui


def _ssm_state_kernel(u_ref, bd4_ref, lr_ref, li_ref, vr_ref, vi_ref):
    for h in range(2):
        st = slice(h * SSM_HALF_ST, (h + 1) * SSM_HALF_ST)
        lr, li = lr_ref[:, st], li_ref[:, st]
        s = None
        for t in range(0, SSM_CHUNK, SSM_BLOCK):
            u4 = jnp.concatenate([_load_lane_blocks(u_ref, _chunk_rows(u_ref, t + i), 2 * h, 2 * h + 2)
                                  for i in range(SSM_BLOCK)], axis=1)
            bu = jnp.dot(u4.astype(BF16), bd4_ref[h], preferred_element_type=F32)
            bur, bui = bu[:, :SSM_HALF_ST], bu[:, SSM_HALF_ST:]
            s = (bur, bui) if s is None else (lr * s[0] - li * s[1] + bur, lr * s[1] + li * s[0] + bui)
        vr_ref[:, st] = s[0]
        vi_ref[:, st] = s[1]


def _ssm_state(u, sp, tc):
    r = u.shape[1] // SSM_CHUNK
    row = lambda w: pl.BlockSpec((tc, w), lambda i: (i, 0))
    const = lambda shape: pl.BlockSpec(shape, lambda i: (0,) * len(shape))
    out = jax.ShapeDtypeStruct((r, SSM_ALL_ST), F32)
    return pl.pallas_call(
        _ssm_state_kernel,
        grid=(r // tc,),
        in_specs=[pl.BlockSpec((LANE_BLOCKS, tc * SSM_CHUNK, 128), lambda i: (0, i, 0)),
                  const((2, SSM_BLOCK * SSM_HALF_CH, 2 * SSM_HALF_ST)),
                  const((1, SSM_ALL_ST)), const((1, SSM_ALL_ST))],
        out_specs=[row(SSM_ALL_ST), row(SSM_ALL_ST)],
        out_shape=[out, out],
        compiler_params=_params("parallel"),
        name="ssm_state",
    )(u, sp["bd4"], sp["a4r"], sp["a4i"])


def _ssm_carry_kernel(vr_ref, vi_ref, s0r_ref, s0i_ref, ar_ref, ai_ref,
                      sr_ref, si_ref, lr_ref, li_ref, *, nc):
    ar, ai = ar_ref[...], ai_ref[...]

    def body(c, s):
        sr, si = s
        sr_ref[0, pl.ds(c, 1), :] = sr
        si_ref[0, pl.ds(c, 1), :] = si
        return (ar * sr - ai * si + vr_ref[0, pl.ds(c, 1), :],
                ar * si + ai * sr + vi_ref[0, pl.ds(c, 1), :])

    sr, si = lax.fori_loop(0, nc, body, (s0r_ref[0], s0i_ref[0]))
    lr_ref[0] = sr
    li_ref[0] = si


def _ssm_carry(vr, vi, s0r, s0i, sp):
    nb, nc, _ = vr.shape
    seq = pl.BlockSpec((1, nc, SSM_ALL_ST), lambda b: (b, 0, 0))
    one = pl.BlockSpec((1, 1, SSM_ALL_ST), lambda b: (b, 0, 0))
    vec = pl.BlockSpec((1, SSM_ALL_ST), lambda b: (0, 0))
    return pl.pallas_call(
        functools.partial(_ssm_carry_kernel, nc=nc),
        grid=(nb,),
        in_specs=[seq, seq, one, one, vec, vec],
        out_specs=[seq, seq, one, one],
        out_shape=[jax.ShapeDtypeStruct((nb, nc, SSM_ALL_ST), F32)] * 2
                  + [jax.ShapeDtypeStruct((nb, 1, SSM_ALL_ST), F32)] * 2,
        compiler_params=_params("parallel"),
        name="ssm_carry",
    )(vr, vi, s0r, s0i, sp["a16r"], sp["a16i"])


def _ssm_out_kernel(u_ref, sr_ref, si_ref, bdb_ref, bdc_ref, lr_ref, li_ref, d_ref, w_ref, b_ref,
                    o_ref):
    lam, s = [], []
    for h in range(2):
        st = slice(h * SSM_HALF_ST, (h + 1) * SSM_HALF_ST)
        lam.append((lr_ref[:, st], li_ref[:, st]))
        s.append((sr_ref[:, st], si_ref[:, st]))
    for t in range(SSM_CHUNK):
        ys = []
        for h in range(2):
            s[h] = _ssm_step(u_ref, bdb_ref, lam[h][0], lam[h][1], t, h, s[h])
            ys.append(jnp.dot(s[h][0].astype(BF16), bdc_ref[h, :SSM_HALF_ST, :],
                              preferred_element_type=F32)
                      + jnp.dot(s[h][1].astype(BF16), bdc_ref[h, SSM_HALF_ST:, :],
                                preferred_element_type=F32))
        y = jnp.concatenate(ys, axis=1) + d_ref[...] * _load_lane_blocks(u_ref, _chunk_rows(u_ref, t))
        y = 0.5 * y * (1.0 + jnp.tanh(math.sqrt(2.0 / math.pi) * (y + 0.044715 * (y * y * y))))
        z = jnp.dot(y.astype(BF16), w_ref[...], preferred_element_type=F32) + b_ref[...]
        _store_lane_blocks(o_ref, _chunk_rows(o_ref, t), y * _sigmoid(z))


def _ssm_out(u, sr, si, sp, d_skip, w_glu, b_glu, tc):
    r = u.shape[1] // SSM_CHUNK
    row = lambda w: pl.BlockSpec((tc, w), lambda i: (i, 0))
    tok = pl.BlockSpec((LANE_BLOCKS, tc * SSM_CHUNK, 128), lambda i: (0, i, 0))
    const = lambda shape: pl.BlockSpec(shape, lambda i: (0,) * len(shape))
    return pl.pallas_call(
        _ssm_out_kernel,
        grid=(r // tc,),
        in_specs=[tok, row(SSM_ALL_ST), row(SSM_ALL_ST),
                  const((2, SSM_HALF_CH, 2 * SSM_HALF_ST)), const((2, 2 * SSM_HALF_ST, SSM_HALF_CH)),
                  const((1, SSM_ALL_ST)), const((1, SSM_ALL_ST)),
                  const((1, W_MIX)), const((W_MIX, W_MIX)), const((1, W_MIX))],
        out_specs=tok,
        out_shape=jax.ShapeDtypeStruct(u.shape, F32),
        compiler_params=_params("parallel"),
        name="ssm_out",
    )(u, sr, si, sp["bdb"], sp["bdc"], sp["lam_r"], sp["lam_i"], d_skip, w_glu, b_glu)


def _cumsum_kernel(lf_ref, c_ref, carry, *, tc):
    @pl.when(pl.program_id(1) == 0)
    def _():
        carry[...] = jnp.zeros_like(carry)

    r = lax.broadcasted_iota(jnp.int32, (tc, tc), 0)
    c = lax.broadcasted_iota(jnp.int32, (tc, tc), 1)
    tri = jnp.where(r >= c, 1.0, 0.0).astype(BF16)
    lf = lf_ref[0]
    hi = lf.astype(BF16)
    rest = lf - hi.astype(F32)
    mid = rest.astype(BF16)
    lo = (rest - mid.astype(F32)).astype(BF16)
    cs = sum(jnp.dot(tri, part, preferred_element_type=F32) for part in (lo, mid, hi)) + carry[...]
    c_ref[0] = cs
    carry[...] = cs[tc - 1:tc, :]


def _cumsum(lf, tc):
    nb, lk, _ = lf.shape
    blk = pl.BlockSpec((1, tc, FOX_HEADS), lambda b, t: (b, t, 0))
    return pl.pallas_call(
        functools.partial(_cumsum_kernel, tc=tc),
        grid=(nb, lk // tc),
        in_specs=[blk],
        out_specs=blk,
        out_shape=jax.ShapeDtypeStruct((nb, lk, FOX_HEADS), F32),
        scratch_shapes=[pltpu.VMEM((1, FOX_HEADS), F32)],
        compiler_params=_params("parallel", "arbitrary"),
        name="logf_cumsum",
    )(lf)


ATTN_ROWS = 16
DEAD_LOGIT = 120.0
NORM_MARGIN = 1.02


def _attn_kernel(b_tab, hp_tab, qi_tab, kj_tab, first_tab, q_ref, k_ref, v_ref, cq_ref, ck_ref,
                 o_ref, s_scr, p_scr, m_scr, a_scr, cq_scr, acc_scr, *, tq, tk, q_off):
    step_id = pl.program_id(0)
    qi = qi_tab[step_id]
    kj = kj_tab[step_id]
    q_lo = q_off + qi * tq
    k_lo = kj * tk
    lane = lax.broadcasted_iota(jnp.int32, (1, 2 * FOX_HEAD_DIM), 1)
    in_heads = [(lane >= hh * FOX_HEAD_DIM) & (lane < (hh + 1) * FOX_HEAD_DIM) for hh in range(2)]

    @pl.when(first_tab[step_id] != 0)
    def _():
        m_scr[...] = jnp.full_like(m_scr, NEG_INF)
        acc_scr[...] = jnp.zeros_like(acc_scr)
        cq = cq_ref[0, 0] * LOG2E
        for hh in range(2):
            cq_scr[hh] = jnp.broadcast_to(cq[:, hh:hh + 1], (tq, 128))

    q = q_ref[0]

    def softmax_rows(hh, masked, ck, cols):
        ckb = [jnp.broadcast_to(ck[hh:hh + 1, lo:hi], (ATTN_ROWS, hi - lo)) for lo, hi in cols]
        for r in range(0, tq, ATTN_ROWS):
            t = [s_scr[hh, r:r + ATTN_ROWS, lo:hi] - ckb[c] for c, (lo, hi) in enumerate(cols)]
            if masked:
                qpos = q_lo + r + lax.broadcasted_iota(jnp.int32, (ATTN_ROWS, 128), 0)
                kpos = k_lo + lax.broadcasted_iota(jnp.int32, (ATTN_ROWS, 128), 1)
                t = [jnp.where((kpos + lo <= qpos)[:, :hi - lo], t[c], NEG_INF)
                     for c, (lo, hi) in enumerate(cols)]
            tmax = None
            for c, (lo, hi) in enumerate(cols):
                if hi - lo == 128:
                    tmax = t[c] if tmax is None else jnp.maximum(tmax, t[c])
            if tmax is not None:
                tmax = jnp.max(tmax, axis=1, keepdims=True)
            if cols[-1][1] - cols[-1][0] < 128:
                ragged = jnp.max(t[-1], axis=1, keepdims=True)
                tmax = ragged if tmax is None else jnp.maximum(tmax, ragged)
            tmax = jnp.broadcast_to(tmax, (ATTN_ROWS, 128))
            cq2 = cq_scr[hh, r:r + ATTN_ROWS, :]
            m_old = m_scr[hh, r:r + ATTN_ROWS, :]
            m_new = jnp.maximum(m_old, tmax + cq2)
            m_scr[hh, r:r + ATTN_ROWS, :] = m_new
            a_scr[hh, r:r + ATTN_ROWS, :] = jnp.exp2(m_old - m_new)
            mt = m_new - cq2
            for c, (lo, hi) in enumerate(cols):
                p_scr[hh, r:r + ATTN_ROWS, lo:hi] = jnp.exp2(t[c] - mt[:, :hi - lo]).astype(BF16)

    def step(masked):
        cols = [(c, min(c + 128, tk)) for c in range(0, tk, 128)]
        kb = k_ref[0].astype(BF16)
        v = v_ref[0]
        ck = ck_ref[0, 0] * LOG2E
        for hh in range(2):
            qm = jnp.where(in_heads[hh], q, jnp.zeros_like(q))
            s_scr[hh] = lax.dot_general(qm, kb, (((1,), (1,)), ((), ())),
                                        preferred_element_type=F32)
        for hh in range(2):
            softmax_rows(hh, masked, ck, cols)
            vh = jnp.where(in_heads[hh], v, 1.0).astype(BF16)
            pv = jnp.dot(p_scr[hh], vh, preferred_element_type=F32)
            acc_scr[hh] = a_scr[hh] * acc_scr[hh] + pv

    needs_mask = k_lo + tk - 1 > q_lo

    @pl.when(needs_mask)
    def _():
        step(True)

    @pl.when(jnp.logical_not(needs_mask))
    def _():
        step(False)

    @pl.when(kj == (q_lo + tq - 1) // tk)
    def _():
        o = []
        for hh in range(2):
            acc = acc_scr[hh]
            o.append(acc * (1.0 / pltpu.roll(acc, FOX_HEAD_DIM, axis=1)))
        o_ref[0] = jnp.where(lane < FOX_HEAD_DIM, o[0], o[1]).astype(BF16)


def _live_tiles(qk_stats, c_all, nb, nq, t):
    qmax = jnp.sqrt(qk_stats[:, 0, :FOX_HEADS]).reshape(nb, nq, FOX_HEADS) * NORM_MARGIN
    kmax = jnp.sqrt(qk_stats[:, 1, :FOX_HEADS]).reshape(nb, nq, FOX_HEADS) * NORM_MARGIN
    c_first = c_all[:, 0::t, :]
    c_last = c_all[:, t - 1::t, :]
    bound = (qmax[:, :, None, :] * (kmax[:, None, :, :] + kmax[:, :, None, :])
             + c_first[:, :, None, :] - c_last[:, None, :, :])
    dead = jnp.all((bound < -DEAD_LOGIT).reshape(nb, nq, nq, HEAD_PAIRS, 2), axis=-1)
    below = (jnp.arange(nq)[None, :] < jnp.arange(nq)[:, None])[None, :, :, None]
    return jnp.transpose(jnp.logical_not(dead & below), (0, 3, 1, 2))


def _attention(q, k_all, v_all, cq, ck, q_off, tq, tk, live=None):
    nb, lq, _ = q.shape
    nq = lq // tq
    nk = k_all.shape[1] // tk
    last = (q_off + np.arange(nq) * tq + tq - 1) // tk
    causal = np.arange(nk)[None, :] <= last[:, None]
    valid = jnp.broadcast_to(causal if live is None else live & causal, (nb, HEAD_PAIRS, nq, nk))
    first = valid & (jnp.cumsum(valid.astype(jnp.int32), axis=3) == 1)
    max_steps = nb * HEAD_PAIRS * int(causal.sum())
    (idx,) = jnp.nonzero(valid.reshape(-1), size=max_steps, fill_value=0)
    idx = idx.astype(jnp.int32)
    n_steps = jnp.sum(valid, dtype=jnp.int32)
    kj_tab = idx % nk
    qi_tab = (idx // nk) % nq
    hp_tab = (idx // (nk * nq)) % HEAD_PAIRS
    b_tab = idx // (nk * nq * HEAD_PAIRS)
    first_tab = first.reshape(-1)[idx].astype(jnp.int32)
    qmap = lambda i, bt, ht, qt, kt, ft: (bt[i], qt[i], ht[i])
    kmap = lambda i, bt, ht, qt, kt, ft: (bt[i], kt[i], ht[i])
    in_specs = [
        pl.BlockSpec((1, tq, 128), qmap),
        pl.BlockSpec((1, tk, 128), kmap),
        pl.BlockSpec((1, tk, 128), kmap),
        pl.BlockSpec((1, 1, tq, 2), lambda i, bt, ht, qt, kt, ft: (bt[i], ht[i], qt[i], 0)),
        pl.BlockSpec((1, 1, 2, tk), lambda i, bt, ht, qt, kt, ft: (bt[i], ht[i], 0, kt[i])),
    ]
    return pl.pallas_call(
        functools.partial(_attn_kernel, tq=tq, tk=tk, q_off=q_off),
        grid_spec=pltpu.PrefetchScalarGridSpec(
            num_scalar_prefetch=5,
            grid=(n_steps,),
            in_specs=in_specs,
            out_specs=pl.BlockSpec((1, tq, 128), qmap),
            scratch_shapes=[pltpu.VMEM((2, tq, tk), F32), pltpu.VMEM((2, tq, tk), BF16),
                            pltpu.VMEM((2, tq, 128), F32), pltpu.VMEM((2, tq, 128), F32),
                            pltpu.VMEM((2, tq, 128), F32), pltpu.VMEM((2, tq, 128), F32)],
        ),
        out_shape=jax.ShapeDtypeStruct((nb, lq, W_MIX), BF16),
        compiler_params=_params("arbitrary"),
        name="fox_attention",
    )(b_tab, hp_tab, qi_tab, kj_tab, first_tab, q, k_all, v_all, cq, ck)


def _merge_kernel(x_ref, g_ref, ya_ref, yb_ref, yc_ref, wg_ref, wb_ref, wo_ref, o_ref):
    x = x_ref[...]
    hb = _rms_bf16(x, g_ref[...])
    merged = None
    for b, y_ref in enumerate((ya_ref, yb_ref, yc_ref)):
        gate = _sigmoid(jnp.dot(hb, wg_ref[:, b * D_MODEL:(b + 1) * D_MODEL],
                                preferred_element_type=F32))
        y = _load_lane_blocks(y_ref, slice(None)) if b == 1 else y_ref[...]
        term = gate * jnp.dot(y.astype(BF16), wb_ref[b], preferred_element_type=F32)
        merged = term if merged is None else merged + term
    o_ref[...] = x + jnp.dot(merged.astype(BF16), wo_ref[...], preferred_element_type=F32)


def _merge(x2, g, ya, yb, yc, w_gate, w_branch, w_out, tm):
    n = x2.shape[0]
    row = lambda w: pl.BlockSpec((tm, w), lambda i: (i, 0))
    const = lambda shape: pl.BlockSpec(shape, lambda i: (0,) * len(shape))
    return pl.pallas_call(
        _merge_kernel,
        grid=(n // tm,),
        in_specs=[row(D_MODEL), const((1, D_MODEL)), row(W_MIX),
                  pl.BlockSpec((LANE_BLOCKS, tm, 128), lambda i: (0, i, 0)), row(W_MIX),
                  const((D_MODEL, 3 * D_MODEL)), const((3, W_MIX, D_MODEL)),
                  const((D_MODEL, D_MODEL))],
        out_specs=row(D_MODEL),
        out_shape=jax.ShapeDtypeStruct((n, D_MODEL), F32),
        compiler_params=_params("parallel"),
        name="branch_merge",
    )(x2, g, ya, yb, yc, w_gate, w_branch, w_out)


FFN_HALO = 16
FFN_SUB = 256


def _ffn_kernel(x_ref, xh_ref, g_ref, hist_ref, wug_ref, wuv_ref, cwg_ref, cwv_ref,
                cbg_ref, cbv_ref, wd_ref, o_ref, cn_ref, h_scr, hh_scr, acc_scr, act_scr,
                *, tm, tn, nj):
    t = pl.program_id(1)
    j = pl.program_id(2)

    @pl.when(j == 0)
    def _():
        h_scr[...] = _rms_bf16(x_ref[...], g_ref[...])
        hh_scr[...] = _rms_bf16(xh_ref[...], g_ref[...])

    first = t == 0
    for lo in range(0, tn, FFN_SUB):
        hi = min(lo + FFN_SUB, tn)
        conv = []
        for hf, (wu_ref, cw_ref, cb_ref) in enumerate(((wug_ref, cwg_ref, cbg_ref),
                                                       (wuv_ref, cwv_ref, cbv_ref))):
            w = wu_ref[:, lo:hi]
            up = jnp.dot(h_scr[...], w, preferred_element_type=F32)
            uph = jnp.dot(hh_scr[...], w, preferred_element_type=F32)
            prev = jnp.where(first, hist_ref[0, hf, :, lo:hi], uph[FFN_HALO - 8:, :])
            full = jnp.concatenate([prev, up], axis=0)
            cn_ref[0, hf, j, :, lo:hi] = up[tm - 2:, :]
            cw = cw_ref[:, lo:hi]
            conv.append(cb_ref[:, lo:hi] + full[6:6 + tm, :] * cw[0:1, :]
                        + full[7:7 + tm, :] * cw[1:2, :] + up * cw[2:3, :])
        gate, val = conv
        act_scr[:, lo:hi] = (gate * _sigmoid(gate) * val).astype(BF16)
    down = jnp.dot(act_scr[...], wd_ref[...], preferred_element_type=F32)

    if nj == 1:
        o_ref[...] = x_ref[...] + down
        return

    @pl.when(j == 0)
    def _():
        acc_scr[...] = down

    @pl.when((j > 0) & (j < nj - 1))
    def _():
        acc_scr[...] += down

    @pl.when(j == nj - 1)
    def _():
        o_ref[...] = x_ref[...] + acc_scr[...] + down


def _ffn(x2, g, hist, wug, wuv, cwg, cwv, cbg, cbv, wd, nb, L, tm, tn):
    n = x2.shape[0]
    nt, nj = L // tm, D_FF // tn
    hpt = tm // FFN_HALO
    row = lambda b, t, j: (b * nt + t, 0)
    col = lambda b, t, j: (0, j)
    resident = dict(pipeline_mode=pl.Buffered(1)) if nj == 1 else {}
    return pl.pallas_call(
        functools.partial(_ffn_kernel, tm=tm, tn=tn, nj=nj),
        grid=(nb, nt, nj),
        in_specs=[
            pl.BlockSpec((tm, D_MODEL), row),
            pl.BlockSpec((FFN_HALO, D_MODEL), lambda b, t, j: (jnp.maximum((b * nt + t) * hpt - 1, 0), 0)),
            pl.BlockSpec((1, D_MODEL), lambda b, t, j: (0, 0)),
            pl.BlockSpec((1, 2, 8, tn), lambda b, t, j: (b, 0, 0, j)),
            pl.BlockSpec((D_MODEL, tn), col, **resident),
            pl.BlockSpec((D_MODEL, tn), col, **resident),
            pl.BlockSpec((3, tn), col),
            pl.BlockSpec((3, tn), col),
            pl.BlockSpec((1, tn), col),
            pl.BlockSpec((1, tn), col),
            pl.BlockSpec((tn, D_MODEL), lambda b, t, j: (j, 0), **resident),
        ],
        out_specs=[pl.BlockSpec((tm, D_MODEL), row),
                   pl.BlockSpec((1, 2, nj, 2, tn), lambda b, t, j: (b, 0, 0, 0, 0))],
        out_shape=[jax.ShapeDtypeStruct((n, D_MODEL), F32),
                   jax.ShapeDtypeStruct((nb, 2, nj, 2, tn), F32)],
        scratch_shapes=[pltpu.VMEM((tm, D_MODEL), BF16), pltpu.VMEM((FFN_HALO, D_MODEL), BF16),
                        pltpu.VMEM((tm, D_MODEL) if nj > 1 else (8, 128), F32),
                        pltpu.VMEM((tm, tn), BF16)],
        compiler_params=_params("arbitrary", "arbitrary", "arbitrary"),
        name="conv_ffn",
    )(x2, x2, g, hist, wug, wuv, cwg, cwv, cbg, cbv, wd)


def _final_norm_kernel(x_ref, g_ref, o_ref):
    x = x_ref[...]
    ms = jnp.mean(x * x, axis=-1, keepdims=True)
    o_ref[...] = x * lax.rsqrt(ms + EPS) * g_ref[...]


def _final_norm(x2, g, tm):
    n = x2.shape[0]
    row = pl.BlockSpec((tm, D_MODEL), lambda i: (i, 0))
    return pl.pallas_call(
        _final_norm_kernel,
        grid=(n // tm,),
        in_specs=[row, pl.BlockSpec((1, D_MODEL), lambda i: (0, 0))],
        out_specs=row,
        out_shape=jax.ShapeDtypeStruct((n, D_MODEL), F32),
        compiler_params=_params("parallel"),
        name="final_norm",
    )(x2, g)


def _layer(x2, nb, L, pos0, pool_hist, s0, k_past, v_past, logf_past, conv_hist, p,
           layer, depth, kbuf, vbuf):
    n = nb * L
    tm = _tile(L, 512)
    u_pool, u_ssm, q, kbuf, vbuf, kb, vb, logf, qk_stats = _inproj(
        x2, p["g_mix"], p["w_main"], p["w_f"], p["b_f"], tm, layer, depth, kbuf, vbuf)

    y_a, pool_new = _pool(u_pool.reshape(nb, L, W_MIX), pool_hist, p["w_pool"], p["pool_scale"],
                          pos0, tm)

    nc = L // SSM_CHUNK
    sp = p["ssm"]
    tc = _tile(nb * nc, 128)
    vr, vi = _ssm_state(u_ssm, sp, tc)
    sr, si, sl_re, sl_im = _ssm_carry(vr.reshape(nb, nc, SSM_ALL_ST), vi.reshape(nb, nc, SSM_ALL_ST),
                                      s0[0], s0[1], sp)
    y_b = _ssm_out(u_ssm, sr.reshape(nb * nc, SSM_ALL_ST), si.reshape(nb * nc, SSM_ALL_ST), sp,
                   p["d_skip"], p["w_glu"], p["b_glu"], tc)

    q3, kb3, vb3 = (a.reshape(nb, L, W_MIX) for a in (q, kb, vb))
    logf3 = logf.reshape(nb, L, FOX_HEADS)
    logf_all = logf3 if k_past is None else jnp.concatenate([logf_past, logf3], axis=1)
    lk = logf_all.shape[1]
    c_all = _cumsum(logf_all, _tile(lk, 512))
    cq = jnp.transpose(c_all[:, lk - L:].reshape(nb, L, HEAD_PAIRS, 2), (0, 2, 1, 3))
    ck = jnp.transpose(c_all.reshape(nb, lk, HEAD_PAIRS, 2), (0, 2, 3, 1))
    if k_past is None:
        tk = _tile(L, 512)
        live = _live_tiles(qk_stats, c_all, nb, L // tm, tm) if tk == tm else None
        y_c = _attention(q3, kb3, vb3, cq, ck, pos0, tm, tk, live)
    else:
        k_all = jnp.concatenate([k_past, kbuf[layer].reshape(nb, L, W_MIX)], axis=1)
        v_all = jnp.concatenate([v_past, vbuf[layer].reshape(nb, L, W_MIX)], axis=1)
        y_c = _attention(q3, k_all, v_all, cq, ck, pos0, tm, lk)

    x2 = _merge(x2, p["g_mix"], y_a.reshape(n, W_MIX), y_b, y_c.reshape(n, W_MIX),
                p["w_gate"], p["w_branch"], p["w_out"], tm)
    x2, conv_new = _ffn(x2, p["g_ffn"], conv_hist, p["wug"], p["wuv"], p["cwg"], p["cwv"],
                        p["cbg"], p["cbv"], p["w_down"], nb, L, tm, D_FF)
    states = (pool_new,
              sl_re.reshape(nb, SSM_GROUPS, SSM_STATE), sl_im.reshape(nb, SSM_GROUPS, SSM_STATE),
              logf3,
              jnp.transpose(conv_new, (0, 3, 1, 2, 4)).reshape(nb, 2, 2 * D_FF))
    return x2, states, kbuf, vbuf


def kernel(x_prompt, x_sample, cache_pool, state_ssm_re, state_ssm_im, cache_k, cache_v, cache_logf, cache_ffn_conv, norm_mix_g, w_in, b_f, w_pool, pool_scale, lam_re, lam_im, log_dt, b_re, b_im, c_re, c_im, d_skip, w_glu, b_glu, w_branch, w_out, norm_ffn_g, w_up, conv_w, conv_b, w_down, norm_final_g):
    bp, lp, _ = x_prompt.shape
    bs, ls, _ = x_sample.shape
    depth = w_in.shape[0]
    past = cache_k.shape[2]
    col_f = 5 * W_MIX
    col_gate = col_f + FOX_HEADS

    hp = x_prompt.reshape(bp * lp, D_MODEL)
    hs = x_sample.reshape(bs * ls, D_MODEL)
    zero_pool = jnp.zeros((bp, POOL_HIST, W_MIX), F32)
    zero_s = (jnp.zeros((bp, 1, SSM_ALL_ST), F32),) * 2
    zero_conv = jnp.zeros((bp, 2, 8, D_FF), F32)

    st_p, st_s = [], []
    kp = vp = ks = vs = None
    for l in range(depth):
        p = dict(
            ssm=_ssm_blockdiag(lam_re[l], lam_im[l], log_dt[l], b_re[l], b_im[l], c_re[l], c_im[l]),
            g_mix=norm_mix_g[l][None], g_ffn=norm_ffn_g[l][None],
            w_main=w_in[l][:, :col_f].astype(BF16),
            w_f=jnp.pad(w_in[l][:, col_f:col_gate], ((0, 0), (0, 128 - FOX_HEADS))).astype(BF16),
            b_f=b_f[l][None],
            w_gate=w_in[l][:, col_gate:].astype(BF16),
            w_pool=w_pool[l].astype(BF16), pool_scale=pool_scale[l][None],
            d_skip=d_skip[l][None], w_glu=w_glu[l].astype(BF16), b_glu=b_glu[l][None],
            w_branch=w_branch[l].astype(BF16), w_out=w_out[l].astype(BF16),
            wug=w_up[l][:, :D_FF].astype(BF16), wuv=w_up[l][:, D_FF:].astype(BF16),
            cwg=conv_w[l][:, :D_FF], cwv=conv_w[l][:, D_FF:],
            cbg=conv_b[l][None, :D_FF], cbv=conv_b[l][None, D_FF:],
            w_down=w_down[l].astype(BF16),
        )
        hp, sp, kp, vp = _layer(hp, bp, lp, 0, zero_pool, zero_s, None, None, None, zero_conv, p,
                                l, depth, kp, vp)
        s0 = (state_ssm_re[l].reshape(bs, 1, SSM_ALL_ST), state_ssm_im[l].reshape(bs, 1, SSM_ALL_ST))
        conv_hist = jnp.pad(jnp.transpose(cache_ffn_conv[l].reshape(bs, 2, 2, D_FF), (0, 2, 1, 3)),
                            ((0, 0), (0, 0), (6, 0), (0, 0)))
        hs, ss, ks, vs = _layer(hs, bs, ls, past, cache_pool[l], s0,
                                cache_k[l].reshape(bs, past, W_MIX), cache_v[l].reshape(bs, past, W_MIX),
                                cache_logf[l], conv_hist, p, l, depth, ks, vs)
        st_p.append(sp)
        st_s.append(ss)

    y_prompt = _final_norm(hp, norm_final_g[None], _tile(bp * lp, 512)).reshape(bp, lp, D_MODEL)
    y_sample = _final_norm(hs, norm_final_g[None], _tile(bs * ls, 512)).reshape(bs, ls, D_MODEL)
    stack = lambda st, i: jnp.stack([s[i] for s in st], axis=0)
    heads_p = (depth, bp, lp, FOX_HEADS, FOX_HEAD_DIM)
    heads_s = (depth, bs, ls, FOX_HEADS, FOX_HEAD_DIM)
    return (y_prompt, y_sample,
            stack(st_p, 0), stack(st_s, 0), stack(st_p, 1), stack(st_s, 1),
            stack(st_p, 2), stack(st_s, 2),
            kp.reshape(heads_p), ks.reshape(heads_s), vp.reshape(heads_p), vs.reshape(heads_s),
            stack(st_p, 3), stack(st_s, 3), stack(st_p, 4), stack(st_s, 4))
```

```python
import functools
import math

import jax
import jax.numpy as jnp
import numpy as np
from jax import lax
from jax.experimental import pallas as pl
from jax.experimental.pallas import tpu as pltpu

F32 = jnp.float32
BF16 = jnp.bfloat16

D_MODEL = 1024
W_MIX = 512
POOL_WINDOWS = (2, 4, 8, 16)
POOL_GROUP_W = 128
POOL_HIST = 15
SSM_GROUPS = 32
SSM_STATE = 64
SSM_CHUNK = 16
FOX_HEADS = 8
FOX_HEAD_DIM = 64
HEAD_PAIRS = FOX_HEADS // 2
D_FF = 2816
EPS = 1e-6
NEG_INF = -1e30
VMEM_LIMIT = 56 * 1024 * 1024


def _params(*sem):
    return pltpu.CompilerParams(dimension_semantics=sem, vmem_limit_bytes=VMEM_LIMIT)


def _tile(n, target, mult=8):
    if n <= target:
        return n
    for t in range(target, 0, -1):
        if n % t == 0 and t % mult == 0:
            return t
    return n


def _rms_bf16(x, g):
    ms = jnp.mean(x * x, axis=-1, keepdims=True)
    return (x * lax.rsqrt(ms + EPS) * g).astype(BF16)


def _sigmoid(x):
    return 1.0 / (1.0 + jnp.exp(-x))


LANE_BLOCKS = W_MIX // 128
LOG2E = math.log2(math.e)


def _load_lane_blocks(ref, rows, lo=0, hi=LANE_BLOCKS):
    return jnp.concatenate([ref[j, rows, :] for j in range(lo, hi)], axis=1)


def _store_lane_blocks(ref, rows, val):
    for j in range(LANE_BLOCKS):
        ref[j, rows, :] = val[:, j * 128:(j + 1) * 128]


def _inproj_kernel(*refs):
    x_ref, g_ref, w_ref, wf_ref, bf_ref, sel_ref = refs[:6]
    up_ref, us_ref, q_ref, k_ref, v_ref, kb_ref, vb_ref, lf_ref, st_ref = refs[-9:]
    hb = _rms_bf16(x_ref[...], g_ref[...])

    def proj(idx):
        return jnp.dot(hb, w_ref[:, idx * W_MIX:(idx + 1) * W_MIX], preferred_element_type=F32)

    def head_norm2_max(a):
        return jnp.max(jnp.dot((a * a).astype(BF16), sel_ref[...], preferred_element_type=F32),
                       axis=0, keepdims=True)

    up_ref[...] = proj(0)
    _store_lane_blocks(us_ref, slice(None), proj(1))
    qn = proj(2) * (FOX_HEAD_DIM ** -0.5)
    q_ref[...] = (qn * LOG2E).astype(BF16)
    k = proj(3)
    k_ref[0] = k
    kb_ref[...] = k.astype(BF16)
    row = lax.broadcasted_iota(jnp.int32, (8, 128), 0)
    st_ref[0] = jnp.where(row == 0, head_norm2_max(qn), jnp.where(row == 1, head_norm2_max(k), 0.0))
    v = proj(4)
    v_ref[0] = v
    vb_ref[...] = v.astype(BF16)
    zf = jnp.dot(hb, wf_ref[...], preferred_element_type=F32)
    a = zf[:, :FOX_HEADS] + bf_ref[...]
    lf_ref[...] = jnp.minimum(a, 0.0) - jnp.log(1.0 + jnp.exp(-jnp.abs(a)))


def _inproj(x2, g, w_main, w_f, b_f, tm, layer, depth, kbuf, vbuf):
    n = x2.shape[0]
    row = lambda w: pl.BlockSpec((tm, w), lambda i: (i, 0))
    const = lambda shape: pl.BlockSpec(shape, lambda i: (0,) * len(shape))
    stacked = pl.BlockSpec((1, tm, W_MIX), lambda i: (layer, i, 0))
    f32 = jax.ShapeDtypeStruct((n, W_MIX), F32)
    b16 = jax.ShapeDtypeStruct((n, W_MIX), BF16)
    big = jax.ShapeDtypeStruct((depth, n, W_MIX), F32)
    head_sel = (jnp.arange(W_MIX)[:, None] // FOX_HEAD_DIM == jnp.arange(128)[None, :]).astype(BF16)
    in_specs = [row(D_MODEL), const((1, D_MODEL)), const((D_MODEL, 5 * W_MIX)),
                const((D_MODEL, 128)), const((1, FOX_HEADS)), const((W_MIX, 128))]
    args = [x2, g, w_main, w_f, b_f, head_sel]
    aliases = {}
    if kbuf is not None:
        in_specs += [pl.BlockSpec(memory_space=pl.ANY)] * 2
        args += [kbuf, vbuf]
        aliases = {6: 3, 7: 4}
    return pl.pallas_call(
        _inproj_kernel,
        grid=(n // tm,),
        in_specs=in_specs,
        out_specs=[row(W_MIX), pl.BlockSpec((LANE_BLOCKS, tm, 128), lambda i: (0, i, 0)), row(W_MIX),
                   stacked, stacked, row(W_MIX), row(W_MIX), row(FOX_HEADS),
                   pl.BlockSpec((1, 8, 128), lambda i: (i, 0, 0))],
        out_shape=[f32, jax.ShapeDtypeStruct((LANE_BLOCKS, n, 128), F32), b16, big, big, b16, b16,
                   jax.ShapeDtypeStruct((n, FOX_HEADS), F32),
                   jax.ShapeDtypeStruct((n // tm, 8, 128), F32)],
        input_output_aliases=aliases,
        compiler_params=_params("parallel"),
        name="in_proj",
    )(*args)


def _pool_kernel(u_ref, hist_ref, w_ref, sc_ref, y_ref, pn_ref, buf, *, tm, pos0):
    t = pl.program_id(1)

    @pl.when(t == 0)
    def _():
        buf[0:1, :] = jnp.zeros((1, W_MIX), F32)
        buf[1:16, :] = hist_ref[0]

    @pl.when(t > 0)
    def _():
        buf[0:16, :] = buf[tm:tm + 16, :]

    buf[16:16 + tm, :] = u_ref[0]
    pos = pos0 + t * tm + lax.broadcasted_iota(jnp.int32, (tm, 1), 0)
    for g, w in enumerate(POOL_WINDOWS):
        lo, hi = g * POOL_GROUP_W, (g + 1) * POOL_GROUP_W
        full = buf[:, lo:hi]
        s = full
        k = 1
        while k < w:
            s = s + pltpu.roll(s, k, axis=0)
            k *= 2
        cnt = jnp.minimum(pos + 1, w).astype(F32)
        d = s[16:] * (1.0 / cnt) - full[16:]
        y = jnp.dot(d.astype(BF16), w_ref[g], preferred_element_type=F32) * sc_ref[:, lo:hi]
        y_ref[0, :, lo:hi] = y.astype(BF16)
    pn_ref[0] = buf[tm + 1:tm + 16, :]


def _pool(u, hist, w_pool, scale, pos0, tm):
    nb, L, _ = u.shape
    return pl.pallas_call(
        functools.partial(_pool_kernel, tm=tm, pos0=pos0),
        grid=(nb, L // tm),
        in_specs=[
            pl.BlockSpec((1, tm, W_MIX), lambda b, t: (b, t, 0)),
            pl.BlockSpec((1, POOL_HIST, W_MIX), lambda b, t: (b, 0, 0)),
            pl.BlockSpec((4, POOL_GROUP_W, POOL_GROUP_W), lambda b, t: (0, 0, 0)),
            pl.BlockSpec((1, W_MIX), lambda b, t: (0, 0)),
        ],
        out_specs=[
            pl.BlockSpec((1, tm, W_MIX), lambda b, t: (b, t, 0)),
            pl.BlockSpec((1, POOL_HIST, W_MIX), lambda b, t: (b, 0, 0)),
        ],
        out_shape=[jax.ShapeDtypeStruct((nb, L, W_MIX), BF16),
                   jax.ShapeDtypeStruct((nb, POOL_HIST, W_MIX), F32)],
        scratch_shapes=[pltpu.VMEM((tm + 16, W_MIX), F32)],
        compiler_params=_params("parallel", "arbitrary"),
        name="pool_mixer",
    )(u, hist, w_pool, scale)


SSM_HALF_CH = 256
SSM_HALF_ST = 1024
SSM_ALL_ST = 2048
SSM_BLOCK = 4


def _ssm_blockdiag(lam_re, lam_im, log_dt, b_re, b_im, c_re, c_im):
    dt = jnp.exp(log_dt)[:, None]
    ar, ai = lam_re * dt, lam_im * dt

    def power(n):
        mag = jnp.exp(n * ar)
        return mag * jnp.cos(n * ai), mag * jnp.sin(n * ai)

    lbr, lbi = power(1.0)
    a16r, a16i = power(float(SSM_CHUNK))
    den = lam_re * lam_re + lam_im * lam_im
    fr = ((lbr - 1.0) * lam_re + lbi * lam_im) / den
    fi = (lbi * lam_re - (lbr - 1.0) * lam_im) / den
    bbr = fr[..., None] * b_re - fi[..., None] * b_im
    bbi = fr[..., None] * b_im + fi[..., None] * b_re
    gh = SSM_GROUPS // 2

    def blockdiag(m):
        _, c, r = m.shape
        rows = jnp.swapaxes(m, 1, 2).reshape(2, gh * r, c)
        same = (jnp.arange(gh * r)[:, None] // r) == (jnp.arange(gh * c)[None, :] // c)
        return jnp.where(same, jnp.tile(rows, (1, 1, gh)), 0.0).astype(BF16)

    bdb = jnp.concatenate([blockdiag(bbr), blockdiag(bbi)], axis=2)
    bdc = jnp.concatenate([blockdiag(c_re), blockdiag(-c_im)], axis=1)
    blk_r, blk_i = [], []
    for i in range(SSM_BLOCK):
        pr, pi = power(float(SSM_BLOCK - 1 - i))
        blk_r.append(blockdiag(pr[..., None] * bbr - pi[..., None] * bbi))
        blk_i.append(blockdiag(pr[..., None] * bbi + pi[..., None] * bbr))
    bd4 = jnp.concatenate([jnp.concatenate(blk_r, axis=1), jnp.concatenate(blk_i, axis=1)],
                          axis=2)
    a4r, a4i = power(float(SSM_BLOCK))
    flat = lambda m: m.reshape(1, SSM_ALL_ST)
    return dict(bdb=bdb, bdc=bdc, bd4=bd4, lam_r=flat(lbr), lam_i=flat(lbi),
                a4r=flat(a4r), a4i=flat(a4i), a16r=flat(a16r), a16i=flat(a16i))


def _chunk_rows(ref, t):
    return pl.ds(t, ref.shape[1] // SSM_CHUNK, stride=SSM_CHUNK)


def _ssm_step(u_ref, bdb_ref, lr, li, t, h, s):
    u_t = _load_lane_blocks(u_ref, _chunk_rows(u_ref, t), 2 * h, 2 * h + 2)
    bu = jnp.dot(u_t.astype(BF16), bdb_ref[h], preferred_element_type=F32)
    bur, bui = bu[:, :SSM_HALF_ST], bu[:, SSM_HALF_ST:]
    sr, si = s
    return lr * sr - li * si + bur, lr * si + li * sr + bui


def _ssm_state_kernel(u_ref, bd4_ref, lr_ref, li_ref, vr_ref, vi_ref):
    for h in range(2):
        st = slice(h * SSM_HALF_ST, (h + 1) * SSM_HALF_ST)
        lr, li = lr_ref[:, st], li_ref[:, st]
        s = None
        for t in range(0, SSM_CHUNK, SSM_BLOCK):
            u4 = jnp.concatenate([_load_lane_blocks(u_ref, _chunk_rows(u_ref, t + i), 2 * h, 2 * h + 2)
                                  for i in range(SSM_BLOCK)], axis=1)
            bu = jnp.dot(u4.astype(BF16), bd4_ref[h], preferred_element_type=F32)
            bur, bui = bu[:, :SSM_HALF_ST], bu[:, SSM_HALF_ST:]
            s = (bur, bui) if s is None else (lr * s[0] - li * s[1] + bur, lr * s[1] + li * s[0] + bui)
        vr_ref[:, st] = s[0]
        vi_ref[:, st] = s[1]


def _ssm_state(u, sp, tc):
    r = u.shape[1] // SSM_CHUNK
    row = lambda w: pl.BlockSpec((tc, w), lambda i: (i, 0))
    const = lambda shape: pl.BlockSpec(shape, lambda i: (0,) * len(shape))
    out = jax.ShapeDtypeStruct((r, SSM_ALL_ST), F32)
    return pl.pallas_call(
        _ssm_state_kernel,
        grid=(r // tc,),
        in_specs=[pl.BlockSpec((LANE_BLOCKS, tc * SSM_CHUNK, 128), lambda i: (0, i, 0)),
                  const((2, SSM_BLOCK * SSM_HALF_CH, 2 * SSM_HALF_ST)),
                  const((1, SSM_ALL_ST)), const((1, SSM_ALL_ST))],
        out_specs=[row(SSM_ALL_ST), row(SSM_ALL_ST)],
        out_shape=[out, out],
        compiler_params=_params("parallel"),
        name="ssm_state",
    )(u, sp["bd4"], sp["a4r"], sp["a4i"])


def _ssm_carry_kernel(vr_ref, vi_ref, s0r_ref, s0i_ref, ar_ref, ai_ref,
                      sr_ref, si_ref, lr_ref, li_ref, *, nc):
    ar, ai = ar_ref[...], ai_ref[...]

    def body(c, s):
        sr, si = s
        sr_ref[0, pl.ds(c, 1), :] = sr
        si_ref[0, pl.ds(c, 1), :] = si
        return (ar * sr - ai * si + vr_ref[0, pl.ds(c, 1), :],
                ar * si + ai * sr + vi_ref[0, pl.ds(c, 1), :])

    sr, si = lax.fori_loop(0, nc, body, (s0r_ref[0], s0i_ref[0]))
    lr_ref[0] = sr
    li_ref[0] = si


def _ssm_carry(vr, vi, s0r, s0i, sp):
    nb, nc, _ = vr.shape
    seq = pl.BlockSpec((1, nc, SSM_ALL_ST), lambda b: (b, 0, 0))
    one = pl.BlockSpec((1, 1, SSM_ALL_ST), lambda b: (b, 0, 0))
    vec = pl.BlockSpec((1, SSM_ALL_ST), lambda b: (0, 0))
    return pl.pallas_call(
        functools.partial(_ssm_carry_kernel, nc=nc),
        grid=(nb,),
        in_specs=[seq, seq, one, one, vec, vec],
        out_specs=[seq, seq, one, one],
        out_shape=[jax.ShapeDtypeStruct((nb, nc, SSM_ALL_ST), F32)] * 2
                  + [jax.ShapeDtypeStruct((nb, 1, SSM_ALL_ST), F32)] * 2,
        compiler_params=_params("parallel"),
        name="ssm_carry",
    )(vr, vi, s0r, s0i, sp["a16r"], sp["a16i"])


def _ssm_out_kernel(u_ref, sr_ref, si_ref, bdb_ref, bdc_ref, lr_ref, li_ref, d_ref, w_ref, b_ref,
                    o_ref):
    lam, s = [], []
    for h in range(2):
        st = slice(h * SSM_HALF_ST, (h + 1) * SSM_HALF_ST)
        lam.append((lr_ref[:, st], li_ref[:, st]))
        s.append((sr_ref[:, st], si_ref[:, st]))
    for t in range(SSM_CHUNK):
        ys = []
        for h in range(2):
            s[h] = _ssm_step(u_ref, bdb_ref, lam[h][0], lam[h][1], t, h, s[h])
            ys.append(jnp.dot(s[h][0].astype(BF16), bdc_ref[h, :SSM_HALF_ST, :],
                              preferred_element_type=F32)
                      + jnp.dot(s[h][1].astype(BF16), bdc_ref[h, SSM_HALF_ST:, :],
                                preferred_element_type=F32))
        y = jnp.concatenate(ys, axis=1) + d_ref[...] * _load_lane_blocks(u_ref, _chunk_rows(u_ref, t))
        y = 0.5 * y * (1.0 + jnp.tanh(math.sqrt(2.0 / math.pi) * (y + 0.044715 * (y * y * y))))
        z = jnp.dot(y.astype(BF16), w_ref[...], preferred_element_type=F32) + b_ref[...]
        _store_lane_blocks(o_ref, _chunk_rows(o_ref, t), y * _sigmoid(z))


def _ssm_out(u, sr, si, sp, d_skip, w_glu, b_glu, tc):
    r = u.shape[1] // SSM_CHUNK
    row = lambda w: pl.BlockSpec((tc, w), lambda i: (i, 0))
    tok = pl.BlockSpec((LANE_BLOCKS, tc * SSM_CHUNK, 128), lambda i: (0, i, 0))
    const = lambda shape: pl.BlockSpec(shape, lambda i: (0,) * len(shape))
    return pl.pallas_call(
        _ssm_out_kernel,
        grid=(r // tc,),
        in_specs=[tok, row(SSM_ALL_ST), row(SSM_ALL_ST),
                  const((2, SSM_HALF_CH, 2 * SSM_HALF_ST)), const((2, 2 * SSM_HALF_ST, SSM_HALF_CH)),
                  const((1, SSM_ALL_ST)), const((1, SSM_ALL_ST)),
                  const((1, W_MIX)), const((W_MIX, W_MIX)), const((1, W_MIX))],
        out_specs=tok,
        out_shape=jax.ShapeDtypeStruct(u.shape, F32),
        compiler_params=_params("parallel"),
        name="ssm_out",
    )(u, sr, si, sp["bdb"], sp["bdc"], sp["lam_r"], sp["lam_i"], d_skip, w_glu, b_glu)


def _cumsum_kernel(lf_ref, c_ref, carry, *, tc):
    @pl.when(pl.program_id(1) == 0)
    def _():
        carry[...] = jnp.zeros_like(carry)

    r = lax.broadcasted_iota(jnp.int32, (tc, tc), 0)
    c = lax.broadcasted_iota(jnp.int32, (tc, tc), 1)
    tri = jnp.where(r >= c, 1.0, 0.0).astype(BF16)
    lf = lf_ref[0]
    hi = lf.astype(BF16)
    rest = lf - hi.astype(F32)
    mid = rest.astype(BF16)
    lo = (rest - mid.astype(F32)).astype(BF16)
    cs = sum(jnp.dot(tri, part, preferred_element_type=F32) for part in (lo, mid, hi)) + carry[...]
    c_ref[0] = cs
    carry[...] = cs[tc - 1:tc, :]


def _cumsum(lf, tc):
    nb, lk, _ = lf.shape
    blk = pl.BlockSpec((1, tc, FOX_HEADS), lambda b, t: (b, t, 0))
    return pl.pallas_call(
        functools.partial(_cumsum_kernel, tc=tc),
        grid=(nb, lk // tc),
        in_specs=[blk],
        out_specs=blk,
        out_shape=jax.ShapeDtypeStruct((nb, lk, FOX_HEADS), F32),
        scratch_shapes=[pltpu.VMEM((1, FOX_HEADS), F32)],
        compiler_params=_params("parallel", "arbitrary"),
        name="logf_cumsum",
    )(lf)


ATTN_ROWS = 16
DEAD_LOGIT = 120.0
NORM_MARGIN = 1.02


def _attn_kernel(b_tab, hp_tab, qi_tab, kj_tab, first_tab, q_ref, k_ref, v_ref, cq_ref, ck_ref,
                 o_ref, s_scr, p_scr, m_scr, a_scr, cq_scr, acc_scr, *, tq, tk, q_off):
    step_id = pl.program_id(0)
    qi = qi_tab[step_id]
    kj = kj_tab[step_id]
    q_lo = q_off + qi * tq
    k_lo = kj * tk
    lane = lax.broadcasted_iota(jnp.int32, (1, 2 * FOX_HEAD_DIM), 1)
    in_heads = [(lane >= hh * FOX_HEAD_DIM) & (lane < (hh + 1) * FOX_HEAD_DIM) for hh in range(2)]

    @pl.when(first_tab[step_id] != 0)
    def _():
        m_scr[...] = jnp.full_like(m_scr, NEG_INF)
        acc_scr[...] = jnp.zeros_like(acc_scr)
        cq = cq_ref[0, 0] * LOG2E
        for hh in range(2):
            cq_scr[hh] = jnp.broadcast_to(cq[:, hh:hh + 1], (tq, 128))

    q = q_ref[0]

    def softmax_rows(hh, masked, ck, cols):
        ckb = [jnp.broadcast_to(ck[hh:hh + 1, lo:hi], (ATTN_ROWS, hi - lo)) for lo, hi in cols]
        for r in range(0, tq, ATTN_ROWS):
            t = [s_scr[hh, r:r + ATTN_ROWS, lo:hi] - ckb[c] for c, (lo, hi) in enumerate(cols)]
            if masked:
                qpos = q_lo + r + lax.broadcasted_iota(jnp.int32, (ATTN_ROWS, 128), 0)
                kpos = k_lo + lax.broadcasted_iota(jnp.int32, (ATTN_ROWS, 128), 1)
                t = [jnp.where((kpos + lo <= qpos)[:, :hi - lo], t[c], NEG_INF)
                     for c, (lo, hi) in enumerate(cols)]
            tmax = None
            for c, (lo, hi) in enumerate(cols):
                if hi - lo == 128:
                    tmax = t[c] if tmax is None else jnp.maximum(tmax, t[c])
            if tmax is not None:
                tmax = jnp.max(tmax, axis=1, keepdims=True)
            if cols[-1][1] - cols[-1][0] < 128:
                ragged = jnp.max(t[-1], axis=1, keepdims=True)
                tmax = ragged if tmax is None else jnp.maximum(tmax, ragged)
            tmax = jnp.broadcast_to(tmax, (ATTN_ROWS, 128))
            cq2 = cq_scr[hh, r:r + ATTN_ROWS, :]
            m_old = m_scr[hh, r:r + ATTN_ROWS, :]
            m_new = jnp.maximum(m_old, tmax + cq2)
            m_scr[hh, r:r + ATTN_ROWS, :] = m_new
            a_scr[hh, r:r + ATTN_ROWS, :] = jnp.exp2(m_old - m_new)
            mt = m_new - cq2
            for c, (lo, hi) in enumerate(cols):
                p_scr[hh, r:r + ATTN_ROWS, lo:hi] = jnp.exp2(t[c] - mt[:, :hi - lo]).astype(BF16)

    def step(masked):
        cols = [(c, min(c + 128, tk)) for c in range(0, tk, 128)]
        kb = k_ref[0].astype(BF16)
        v = v_ref[0]
        ck = ck_ref[0, 0] * LOG2E
        for hh in range(2):
            qm = jnp.where(in_heads[hh], q, jnp.zeros_like(q))
            s_scr[hh] = lax.dot_general(qm, kb, (((1,), (1,)), ((), ())),
                                        preferred_element_type=F32)
        for hh in range(2):
            softmax_rows(hh, masked, ck, cols)
            vh = jnp.where(in_heads[hh], v, 1.0).astype(BF16)
            pv = jnp.dot(p_scr[hh], vh, preferred_element_type=F32)
            acc_scr[hh] = a_scr[hh] * acc_scr[hh] + pv

    needs_mask = k_lo + tk - 1 > q_lo

    @pl.when(needs_mask)
    def _():
        step(True)

    @pl.when(jnp.logical_not(needs_mask))
    def _():
        step(False)

    @pl.when(kj == (q_lo + tq - 1) // tk)
    def _():
        o = []
        for hh in range(2):
            acc = acc_scr[hh]
            o.append(acc * (1.0 / pltpu.roll(acc, FOX_HEAD_DIM, axis=1)))
        o_ref[0] = jnp.where(lane < FOX_HEAD_DIM, o[0], o[1]).astype(BF16)


def _live_tiles(qk_stats, c_all, nb, nq, t):
    qmax = jnp.sqrt(qk_stats[:, 0, :FOX_HEADS]).reshape(nb, nq, FOX_HEADS) * NORM_MARGIN
    kmax = jnp.sqrt(qk_stats[:, 1, :FOX_HEADS]).reshape(nb, nq, FOX_HEADS) * NORM_MARGIN
    c_first = c_all[:, 0::t, :]
    c_last = c_all[:, t - 1::t, :]
    bound = (qmax[:, :, None, :] * (kmax[:, None, :, :] + kmax[:, :, None, :])
             + c_first[:, :, None, :] - c_last[:, None, :, :])
    dead = jnp.all((bound < -DEAD_LOGIT).reshape(nb, nq, nq, HEAD_PAIRS, 2), axis=-1)
    below = (jnp.arange(nq)[None, :] < jnp.arange(nq)[:, None])[None, :, :, None]
    return jnp.transpose(jnp.logical_not(dead & below), (0, 3, 1, 2))


def _attention(q, k_all, v_all, cq, ck, q_off, tq, tk, live=None):
    nb, lq, _ = q.shape
    nq = lq // tq
    nk = k_all.shape[1] // tk
    last = (q_off + np.arange(nq) * tq + tq - 1) // tk
    causal = np.arange(nk)[None, :] <= last[:, None]
    valid = jnp.broadcast_to(causal if live is None else live & causal, (nb, HEAD_PAIRS, nq, nk))
    first = valid & (jnp.cumsum(valid.astype(jnp.int32), axis=3) == 1)
    max_steps = nb * HEAD_PAIRS * int(causal.sum())
    (idx,) = jnp.nonzero(valid.reshape(-1), size=max_steps, fill_value=0)
    idx = idx.astype(jnp.int32)
    n_steps = jnp.sum(valid, dtype=jnp.int32)
    kj_tab = idx % nk
    qi_tab = (idx // nk) % nq
    hp_tab = (idx // (nk * nq)) % HEAD_PAIRS
    b_tab = idx // (nk * nq * HEAD_PAIRS)
    first_tab = first.reshape(-1)[idx].astype(jnp.int32)
    qmap = lambda i, bt, ht, qt, kt, ft: (bt[i], qt[i], ht[i])
    kmap = lambda i, bt, ht, qt, kt, ft: (bt[i], kt[i], ht[i])
    in_specs = [
        pl.BlockSpec((1, tq, 128), qmap),
        pl.BlockSpec((1, tk, 128), kmap),
        pl.BlockSpec((1, tk, 128), kmap),
        pl.BlockSpec((1, 1, tq, 2), lambda i, bt, ht, qt, kt, ft: (bt[i], ht[i], qt[i], 0)),
        pl.BlockSpec((1, 1, 2, tk), lambda i, bt, ht, qt, kt, ft: (bt[i], ht[i], 0, kt[i])),
    ]
    return pl.pallas_call(
        functools.partial(_attn_kernel, tq=tq, tk=tk, q_off=q_off),
        grid_spec=pltpu.PrefetchScalarGridSpec(
            num_scalar_prefetch=5,
            grid=(n_steps,),
            in_specs=in_specs,
            out_specs=pl.BlockSpec((1, tq, 128), qmap),
            scratch_shapes=[pltpu.VMEM((2, tq, tk), F32), pltpu.VMEM((2, tq, tk), BF16),
                            pltpu.VMEM((2, tq, 128), F32), pltpu.VMEM((2, tq, 128), F32),
                            pltpu.VMEM((2, tq, 128), F32), pltpu.VMEM((2, tq, 128), F32)],
        ),
        out_shape=jax.ShapeDtypeStruct((nb, lq, W_MIX), BF16),
        compiler_params=_params("arbitrary"),
        name="fox_attention",
    )(b_tab, hp_tab, qi_tab, kj_tab, first_tab, q, k_all, v_all, cq, ck)


def _merge_kernel(x_ref, g_ref, ya_ref, yb_ref, yc_ref, wg_ref, wb_ref, wo_ref, o_ref):
    x = x_ref[...]
    hb = _rms_bf16(x, g_ref[...])
    merged = None
    for b, y_ref in enumerate((ya_ref, yb_ref, yc_ref)):
        gate = _sigmoid(jnp.dot(hb, wg_ref[:, b * D_MODEL:(b + 1) * D_MODEL],
                                preferred_element_type=F32))
        y = _load_lane_blocks(y_ref, slice(None)) if b == 1 else y_ref[...]
        term = gate * jnp.dot(y.astype(BF16), wb_ref[b], preferred_element_type=F32)
        merged = term if merged is None else merged + term
    o_ref[...] = x + jnp.dot(merged.astype(BF16), wo_ref[...], preferred_element_type=F32)


def _merge(x2, g, ya, yb, yc, w_gate, w_branch, w_out, tm):
    n = x2.shape[0]
    row = lambda w: pl.BlockSpec((tm, w), lambda i: (i, 0))
    const = lambda shape: pl.BlockSpec(shape, lambda i: (0,) * len(shape))
    return pl.pallas_call(
        _merge_kernel,
        grid=(n // tm,),
        in_specs=[row(D_MODEL), const((1, D_MODEL)), row(W_MIX),
                  pl.BlockSpec((LANE_BLOCKS, tm, 128), lambda i: (0, i, 0)), row(W_MIX),
                  const((D_MODEL, 3 * D_MODEL)), const((3, W_MIX, D_MODEL)),
                  const((D_MODEL, D_MODEL))],
        out_specs=row(D_MODEL),
        out_shape=jax.ShapeDtypeStruct((n, D_MODEL), F32),
        compiler_params=_params("parallel"),
        name="branch_merge",
    )(x2, g, ya, yb, yc, w_gate, w_branch, w_out)


FFN_HALO = 16
FFN_SUB = 256


def _ffn_kernel(x_ref, xh_ref, g_ref, hist_ref, wug_ref, wuv_ref, cwg_ref, cwv_ref,
                cbg_ref, cbv_ref, wd_ref, o_ref, cn_ref, h_scr, hh_scr, acc_scr, act_scr,
                *, tm, tn, nj):
    t = pl.program_id(1)
    j = pl.program_id(2)

    @pl.when(j == 0)
    def _():
        h_scr[...] = _rms_bf16(x_ref[...], g_ref[...])
        hh_scr[...] = _rms_bf16(xh_ref[...], g_ref[...])

    first = t == 0
    for lo in range(0, tn, FFN_SUB):
        hi = min(lo + FFN_SUB, tn)
        conv = []
        for hf, (wu_ref, cw_ref, cb_ref) in enumerate(((wug_ref, cwg_ref, cbg_ref),
                                                       (wuv_ref, cwv_ref, cbv_ref))):
            w = wu_ref[:, lo:hi]
            up = jnp.dot(h_scr[...], w, preferred_element_type=F32)
            uph = jnp.dot(hh_scr[...], w, preferred_element_type=F32)
            prev = jnp.where(first, hist_ref[0, hf, :, lo:hi], uph[FFN_HALO - 8:, :])
            full = jnp.concatenate([prev, up], axis=0)
            cn_ref[0, hf, j, :, lo:hi] = up[tm - 2:, :]
            cw = cw_ref[:, lo:hi]
            conv.append(cb_ref[:, lo:hi] + full[6:6 + tm, :] * cw[0:1, :]
                        + full[7:7 + tm, :] * cw[1:2, :] + up * cw[2:3, :])
        gate, val = conv
        act_scr[:, lo:hi] = (gate * _sigmoid(gate) * val).astype(BF16)
    down = jnp.dot(act_scr[...], wd_ref[...], preferred_element_type=F32)

    if nj == 1:
        o_ref[...] = x_ref[...] + down
        return

    @pl.when(j == 0)
    def _():
        acc_scr[...] = down

    @pl.when((j > 0) & (j < nj - 1))
    def _():
        acc_scr[...] += down

    @pl.when(j == nj - 1)
    def _():
        o_ref[...] = x_ref[...] + acc_scr[...] + down


def _ffn(x2, g, hist, wug, wuv, cwg, cwv, cbg, cbv, wd, nb, L, tm, tn):
    n = x2.shape[0]
    nt, nj = L // tm, D_FF // tn
    hpt = tm // FFN_HALO
    row = lambda b, t, j: (b * nt + t, 0)
    col = lambda b, t, j: (0, j)
    resident = dict(pipeline_mode=pl.Buffered(1)) if nj == 1 else {}
    return pl.pallas_call(
        functools.partial(_ffn_kernel, tm=tm, tn=tn, nj=nj),
        grid=(nb, nt, nj),
        in_specs=[
            pl.BlockSpec((tm, D_MODEL), row),
            pl.BlockSpec((FFN_HALO, D_MODEL), lambda b, t, j: (jnp.maximum((b * nt + t) * hpt - 1, 0), 0)),
            pl.BlockSpec((1, D_MODEL), lambda b, t, j: (0, 0)),
            pl.BlockSpec((1, 2, 8, tn), lambda b, t, j: (b, 0, 0, j)),
            pl.BlockSpec((D_MODEL, tn), col, **resident),
            pl.BlockSpec((D_MODEL, tn), col, **resident),
            pl.BlockSpec((3, tn), col),
            pl.BlockSpec((3, tn), col),
            pl.BlockSpec((1, tn), col),
            pl.BlockSpec((1, tn), col),
            pl.BlockSpec((tn, D_MODEL), lambda b, t, j: (j, 0), **resident),
        ],
        out_specs=[pl.BlockSpec((tm, D_MODEL), row),
                   pl.BlockSpec((1, 2, nj, 2, tn), lambda b, t, j: (b, 0, 0, 0, 0))],
        out_shape=[jax.ShapeDtypeStruct((n, D_MODEL), F32),
                   jax.ShapeDtypeStruct((nb, 2, nj, 2, tn), F32)],
        scratch_shapes=[pltpu.VMEM((tm, D_MODEL), BF16), pltpu.VMEM((FFN_HALO, D_MODEL), BF16),
                        pltpu.VMEM((tm, D_MODEL) if nj > 1 else (8, 128), F32),
                        pltpu.VMEM((tm, tn), BF16)],
        compiler_params=_params("arbitrary", "arbitrary", "arbitrary"),
        name="conv_ffn",
    )(x2, x2, g, hist, wug, wuv, cwg, cwv, cbg, cbv, wd)


def _final_norm_kernel(x_ref, g_ref, o_ref):
    x = x_ref[...]
    ms = jnp.mean(x * x, axis=-1, keepdims=True)
    o_ref[...] = x * lax.rsqrt(ms + EPS) * g_ref[...]


def _final_norm(x2, g, tm):
    n = x2.shape[0]
    row = pl.BlockSpec((tm, D_MODEL), lambda i: (i, 0))
    return pl.pallas_call(
        _final_norm_kernel,
        grid=(n // tm,),
        in_specs=[row, pl.BlockSpec((1, D_MODEL), lambda i: (0, 0))],
        out_specs=row,
        out_shape=jax.ShapeDtypeStruct((n, D_MODEL), F32),
        compiler_params=_params("parallel"),
        name="final_norm",
    )(x2, g)


def _layer(x2, nb, L, pos0, pool_hist, s0, k_past, v_past, logf_past, conv_hist, p,
           layer, depth, kbuf, vbuf):
    n = nb * L
    tm = _tile(L, 512)
    u_pool, u_ssm, q, kbuf, vbuf, kb, vb, logf, qk_stats = _inproj(
        x2, p["g_mix"], p["w_main"], p["w_f"], p["b_f"], tm, layer, depth, kbuf, vbuf)

    y_a, pool_new = _pool(u_pool.reshape(nb, L, W_MIX), pool_hist, p["w_pool"], p["pool_scale"],
                          pos0, tm)

    nc = L // SSM_CHUNK
    sp = p["ssm"]
    tc = _tile(nb * nc, 128)
    vr, vi = _ssm_state(u_ssm, sp, tc)
    sr, si, sl_re, sl_im = _ssm_carry(vr.reshape(nb, nc, SSM_ALL_ST), vi.reshape(nb, nc, SSM_ALL_ST),
                                      s0[0], s0[1], sp)
    y_b = _ssm_out(u_ssm, sr.reshape(nb * nc, SSM_ALL_ST), si.reshape(nb * nc, SSM_ALL_ST), sp,
                   p["d_skip"], p["w_glu"], p["b_glu"], tc)

    q3, kb3, vb3 = (a.reshape(nb, L, W_MIX) for a in (q, kb, vb))
    logf3 = logf.reshape(nb, L, FOX_HEADS)
    logf_all = logf3 if k_past is None else jnp.concatenate([logf_past, logf3], axis=1)
    lk = logf_all.shape[1]
    c_all = _cumsum(logf_all, _tile(lk, 512))
    cq = jnp.transpose(c_all[:, lk - L:].reshape(nb, L, HEAD_PAIRS, 2), (0, 2, 1, 3))
    ck = jnp.transpose(c_all.reshape(nb, lk, HEAD_PAIRS, 2), (0, 2, 3, 1))
    if k_past is None:
        tk = _tile(L, 512)
        live = _live_tiles(qk_stats, c_all, nb, L // tm, tm) if tk == tm else None
        y_c = _attention(q3, kb3, vb3, cq, ck, pos0, tm, tk, live)
    else:
        k_all = jnp.concatenate([k_past, kbuf[layer].reshape(nb, L, W_MIX)], axis=1)
        v_all = jnp.concatenate([v_past, vbuf[layer].reshape(nb, L, W_MIX)], axis=1)
        y_c = _attention(q3, k_all, v_all, cq, ck, pos0, tm, lk)

    x2 = _merge(x2, p["g_mix"], y_a.reshape(n, W_MIX), y_b, y_c.reshape(n, W_MIX),
                p["w_gate"], p["w_branch"], p["w_out"], tm)
    x2, conv_new = _ffn(x2, p["g_ffn"], conv_hist, p["wug"], p["wuv"], p["cwg"], p["cwv"],
                        p["cbg"], p["cbv"], p["w_down"], nb, L, tm, D_FF)
    states = (pool_new,
              sl_re.reshape(nb, SSM_GROUPS, SSM_STATE), sl_im.reshape(nb, SSM_GROUPS, SSM_STATE),
              logf3,
              jnp.transpose(conv_new, (0, 3, 1, 2, 4)).reshape(nb, 2, 2 * D_FF))
    return x2, states, kbuf, vbuf


def kernel(x_prompt, x_sample, cache_pool, state_ssm_re, state_ssm_im, cache_k, cache_v, cache_logf, cache_ffn_conv, norm_mix_g, w_in, b_f, w_pool, pool_scale, lam_re, lam_im, log_dt, b_re, b_im, c_re, c_im, d_skip, w_glu, b_glu, w_branch, w_out, norm_ffn_g, w_up, conv_w, conv_b, w_down, norm_final_g):
    bp, lp, _ = x_prompt.shape
    bs, ls, _ = x_sample.shape
    depth = w_in.shape[0]
    past = cache_k.shape[2]
    col_f = 5 * W_MIX
    col_gate = col_f + FOX_HEADS

    hp = x_prompt.reshape(bp * lp, D_MODEL)
    hs = x_sample.reshape(bs * ls, D_MODEL)
    zero_pool = jnp.zeros((bp, POOL_HIST, W_MIX), F32)
    zero_s = (jnp.zeros((bp, 1, SSM_ALL_ST), F32),) * 2
    zero_conv = jnp.zeros((bp, 2, 8, D_FF), F32)

    st_p, st_s = [], []
    kp = vp = ks = vs = None
    for l in range(depth):
        p = dict(
            ssm=_ssm_blockdiag(lam_re[l], lam_im[l], log_dt[l], b_re[l], b_im[l], c_re[l], c_im[l]),
            g_mix=norm_mix_g[l][None], g_ffn=norm_ffn_g[l][None],
            w_main=w_in[l][:, :col_f].astype(BF16),
            w_f=jnp.pad(w_in[l][:, col_f:col_gate], ((0, 0), (0, 128 - FOX_HEADS))).astype(BF16),
            b_f=b_f[l][None],
            w_gate=w_in[l][:, col_gate:].astype(BF16),
            w_pool=w_pool[l].astype(BF16), pool_scale=pool_scale[l][None],
            d_skip=d_skip[l][None], w_glu=w_glu[l].astype(BF16), b_glu=b_glu[l][None],
            w_branch=w_branch[l].astype(BF16), w_out=w_out[l].astype(BF16),
            wug=w_up[l][:, :D_FF].astype(BF16), wuv=w_up[l][:, D_FF:].astype(BF16),
            cwg=conv_w[l][:, :D_FF], cwv=conv_w[l][:, D_FF:],
            cbg=conv_b[l][None, :D_FF], cbv=conv_b[l][None, D_FF:],
            w_down=w_down[l].astype(BF16),
        )
        hp, sp, kp, vp = _layer(hp, bp, lp, 0, zero_pool, zero_s, None, None, None, zero_conv, p,
                                l, depth, kp, vp)
        s0 = (state_ssm_re[l].reshape(bs, 1, SSM_ALL_ST), state_ssm_im[l].reshape(bs, 1, SSM_ALL_ST))
        conv_hist = jnp.pad(jnp.transpose(cache_ffn_conv[l].reshape(bs, 2, 2, D_FF), (0, 2, 1, 3)),
                            ((0, 0), (0, 0), (6, 0), (0, 0)))
        hs, ss, ks, vs = _layer(hs, bs, ls, past, cache_pool[l], s0,
                                cache_k[l].reshape(bs, past, W_MIX), cache_v[l].reshape(bs, past, W_MIX),
                                cache_logf[l], conv_hist, p, l, depth, ks, vs)
        st_p.append(sp)
        st_s.append(ss)

    y_prompt = _final_norm(hp, norm_final_g[None], _tile(bp * lp, 512)).reshape(bp, lp, D_MODEL)
    y_sample = _final_norm(hs, norm_final_g[None], _tile(bs * ls, 512)).reshape(bs, ls, D_MODEL)
    stack = lambda st, i: jnp.stack([s[i] for s in st], axis=0)
    heads_p = (depth, bp, lp, FOX_HEADS, FOX_HEAD_DIM)
    heads_s = (depth, bs, ls, FOX_HEADS, FOX_HEAD_DIM)
    return (y_prompt, y_sample,
            stack(st_p, 0), stack(st_s, 0), stack(st_p, 1), stack(st_s, 1),
            stack(st_p, 2), stack(st_s, 2),
            kp.reshape(heads_p), ks.reshape(heads_s), vp.reshape(heads_p), vs.reshape(heads_s),
            stack(st_p, 3), stack(st_s, 3), stack(st_p, 4), stack(st_s, 4))
```

```python
import functools
import math

import jax
import jax.numpy as jnp
import numpy as np
from jax import lax
from jax.experimental import pallas as pl
from jax.experimental.pallas import tpu as pltpu

F32 = jnp.float32
BF16 = jnp.bfloat16

D_MODEL = 1024
W_MIX = 512
POOL_WINDOWS = (2, 4, 8, 16)
POOL_GROUP_W = 128
POOL_HIST = 15
SSM_GROUPS = 32
SSM_STATE = 64
SSM_CHUNK = 16
FOX_HEADS = 8
FOX_HEAD_DIM = 64
HEAD_PAIRS = FOX_HEADS // 2
D_FF = 2816
EPS = 1e-6
NEG_INF = -1e30
VMEM_LIMIT = 56 * 1024 * 1024


def _params(*sem):
    return pltpu.CompilerParams(dimension_semantics=sem, vmem_limit_bytes=VMEM_LIMIT)


def _tile(n, target, mult=8):
    if n <= target:
        return n
    for t in range(target, 0, -1):
        if n % t == 0 and t % mult == 0:
            return t
    return n


def _rms_bf16(x, g):
    ms = jnp.mean(x * x, axis=-1, keepdims=True)
    return (x * lax.rsqrt(ms + EPS) * g).astype(BF16)


def _sigmoid(x):
    return 1.0 / (1.0 + jnp.exp(-x))


LANE_BLOCKS = W_MIX // 128
LOG2E = math.log2(math.e)


def _load_lane_blocks(ref, rows, lo=0, hi=LANE_BLOCKS):
    return jnp.concatenate([ref[j, rows, :] for j in range(lo, hi)], axis=1)


def _store_lane_blocks(ref, rows, val):
    for j in range(LANE_BLOCKS):
        ref[j, rows, :] = val[:, j * 128:(j + 1) * 128]


def _inproj_kernel(*refs):
    x_ref, g_ref, w_ref, wf_ref, bf_ref, sel_ref = refs[:6]
    up_ref, us_ref, q_ref, k_ref, v_ref, kb_ref, vb_ref, lf_ref, st_ref = refs[-9:]
    hb = _rms_bf16(x_ref[...], g_ref[...])

    def proj(idx):
        return jnp.dot(hb, w_ref[:, idx * W_MIX:(idx + 1) * W_MIX], preferred_element_type=F32)

    def head_norm2_max(a):
        return jnp.max(jnp.dot((a * a).astype(BF16), sel_ref[...], preferred_element_type=F32),
                       axis=0, keepdims=True)

    up_ref[...] = proj(0)
    _store_lane_blocks(us_ref, slice(None), proj(1))
    qn = proj(2) * (FOX_HEAD_DIM ** -0.5)
    q_ref[...] = (qn * LOG2E).astype(BF16)
    k = proj(3)
    k_ref[0] = k
    kb_ref[...] = k.astype(BF16)
    row = lax.broadcasted_iota(jnp.int32, (8, 128), 0)
    st_ref[0] = jnp.where(row == 0, head_norm2_max(qn), jnp.where(row == 1, head_norm2_max(k), 0.0))
    v = proj(4)
    v_ref[0] = v
    vb_ref[...] = v.astype(BF16)
    zf = jnp.dot(hb, wf_ref[...], preferred_element_type=F32)
    a = zf[:, :FOX_HEADS] + bf_ref[...]
    lf_ref[...] = jnp.minimum(a, 0.0) - jnp.log(1.0 + jnp.exp(-jnp.abs(a)))


def _inproj(x2, g, w_main, w_f, b_f, tm, layer, depth, kbuf, vbuf):
    n = x2.shape[0]
    row = lambda w: pl.BlockSpec((tm, w), lambda i: (i, 0))
    const = lambda shape: pl.BlockSpec(shape, lambda i: (0,) * len(shape))
    stacked = pl.BlockSpec((1, tm, W_MIX), lambda i: (layer, i, 0))
    f32 = jax.ShapeDtypeStruct((n, W_MIX), F32)
    b16 = jax.ShapeDtypeStruct((n, W_MIX), BF16)
    big = jax.ShapeDtypeStruct((depth, n, W_MIX), F32)
    head_sel = (jnp.arange(W_MIX)[:, None] // FOX_HEAD_DIM == jnp.arange(128)[None, :]).astype(BF16)
    in_specs = [row(D_MODEL), const((1, D_MODEL)), const((D_MODEL, 5 * W_MIX)),
                const((D_MODEL, 128)), const((1, FOX_HEADS)), const((W_MIX, 128))]
    args = [x2, g, w_main, w_f, b_f, head_sel]
    aliases = {}
    if kbuf is not None:
        in_specs += [pl.BlockSpec(memory_space=pl.ANY)] * 2
        args += [kbuf, vbuf]
        aliases = {6: 3, 7: 4}
    return pl.pallas_call(
        _inproj_kernel,
        grid=(n // tm,),
        in_specs=in_specs,
        out_specs=[row(W_MIX), pl.BlockSpec((LANE_BLOCKS, tm, 128), lambda i: (0, i, 0)), row(W_MIX),
                   stacked, stacked, row(W_MIX), row(W_MIX), row(FOX_HEADS),
                   pl.BlockSpec((1, 8, 128), lambda i: (i, 0, 0))],
        out_shape=[f32, jax.ShapeDtypeStruct((LANE_BLOCKS, n, 128), F32), b16, big, big, b16, b16,
                   jax.ShapeDtypeStruct((n, FOX_HEADS), F32),
                   jax.ShapeDtypeStruct((n // tm, 8, 128), F32)],
        input_output_aliases=aliases,
        compiler_params=_params("parallel"),
        name="in_proj",
    )(*args)


def _pool_kernel(u_ref, hist_ref, w_ref, sc_ref, y_ref, pn_ref, buf, *, tm, pos0):
    t = pl.program_id(1)

    @pl.when(t == 0)
    def _():
        buf[0:1, :] = jnp.zeros((1, W_MIX), F32)
        buf[1:16, :] = hist_ref[0]

    @pl.when(t > 0)
    def _():
        buf[0:16, :] = buf[tm:tm + 16, :]

    buf[16:16 + tm, :] = u_ref[0]
    pos = pos0 + t * tm + lax.broadcasted_iota(jnp.int32, (tm, 1), 0)
    for g, w in enumerate(POOL_WINDOWS):
        lo, hi = g * POOL_GROUP_W, (g + 1) * POOL_GROUP_W
        full = buf[:, lo:hi]
        s = full
        k = 1
        while k < w:
            s = s + pltpu.roll(s, k, axis=0)
            k *= 2
        cnt = jnp.minimum(pos + 1, w).astype(F32)
        d = s[16:] * (1.0 / cnt) - full[16:]
        y = jnp.dot(d.astype(BF16), w_ref[g], preferred_element_type=F32) * sc_ref[:, lo:hi]
        y_ref[0, :, lo:hi] = y.astype(BF16)
    pn_ref[0] = buf[tm + 1:tm + 16, :]


def _pool(u, hist, w_pool, scale, pos0, tm):
    nb, L, _ = u.shape
    return pl.pallas_call(
        functools.partial(_pool_kernel, tm=tm, pos0=pos0),
        grid=(nb, L // tm),
        in_specs=[
            pl.BlockSpec((1, tm, W_MIX), lambda b, t: (b, t, 0)),
            pl.BlockSpec((1, POOL_HIST, W_MIX), lambda b, t: (b, 0, 0)),
            pl.BlockSpec((4, POOL_GROUP_W, POOL_GROUP_W), lambda b, t: (0, 0, 0)),
            pl.BlockSpec((1, W_MIX), lambda b, t: (0, 0)),
        ],
        out_specs=[
            pl.BlockSpec((1, tm, W_MIX), lambda b, t: (b, t, 0)),
            pl.BlockSpec((1, POOL_HIST, W_MIX), lambda b, t: (b, 0, 0)),
        ],
        out_shape=[jax.ShapeDtypeStruct((nb, L, W_MIX), BF16),
                   jax.ShapeDtypeStruct((nb, POOL_HIST, W_MIX), F32)],
        scratch_shapes=[pltpu.VMEM((tm + 16, W_MIX), F32)],
        compiler_params=_params("parallel", "arbitrary"),
        name="pool_mixer",
    )(u, hist, w_pool, scale)


SSM_HALF_CH = 256
SSM_HALF_ST = 1024
SSM_ALL_ST = 2048
SSM_BLOCK = 4


def _ssm_blockdiag(lam_re, lam_im, log_dt, b_re, b_im, c_re, c_im):
    dt = jnp.exp(log_dt)[:, None]
    ar, ai = lam_re * dt, lam_im * dt

    def power(n):
        mag = jnp.exp(n * ar)
        return mag * jnp.cos(n * ai), mag * jnp.sin(n * ai)

    lbr, lbi = power(1.0)
    a16r, a16i = power(float(SSM_CHUNK))
    den = lam_re * lam_re + lam_im * lam_im
    fr = ((lbr - 1.0) * lam_re + lbi * lam_im) / den
    fi = (lbi * lam_re - (lbr - 1.0) * lam_im) / den
    bbr = fr[..., None] * b_re - fi[..., None] * b_im
    bbi = fr[..., None] * b_im + fi[..., None] * b_re
    gh = SSM_GROUPS // 2

    def blockdiag(m):
        _, c, r = m.shape
        rows = jnp.swapaxes(m, 1, 2).reshape(2, gh * r, c)
        same = (jnp.arange(gh * r)[:, None] // r) == (jnp.arange(gh * c)[None, :] // c)
        return jnp.where(same, jnp.tile(rows, (1, 1, gh)), 0.0).astype(BF16)

    bdb = jnp.concatenate([blockdiag(bbr), blockdiag(bbi)], axis=2)
    bdc = jnp.concatenate([blockdiag(c_re), blockdiag(-c_im)], axis=1)
    blk_r, blk_i = [], []
    for i in range(SSM_BLOCK):
        pr, pi = power(float(SSM_BLOCK - 1 - i))
        blk_r.append(blockdiag(pr[..., None] * bbr - pi[..., None] * bbi))
        blk_i.append(blockdiag(pr[..., None] * bbi + pi[..., None] * bbr))
    bd4 = jnp.concatenate([jnp.concatenate(blk_r, axis=1), jnp.concatenate(blk_i, axis=1)],
                          axis=2)
    a4r, a4i = power(float(SSM_BLOCK))
    flat = lambda m: m.reshape(1, SSM_ALL_ST)
    return dict(bdb=bdb, bdc=bdc, bd4=bd4, lam_r=flat(lbr), lam_i=flat(lbi),
                a4r=flat(a4r), a4i=flat(a4i), a16r=flat(a16r), a16i=flat(a16i))


def _chunk_rows(ref, t):
    return pl.ds(t, ref.shape[1] // SSM_CHUNK, stride=SSM_CHUNK)


def _ssm_step(u_ref, bdb_ref, lr, li, t, h, s):
    u_t = _load_lane_blocks(u_ref, _chunk_rows(u_ref, t), 2 * h, 2 * h + 2)
    bu = jnp.dot(u_t.astype(BF16), bdb_ref[h], preferred_element_type=F32)
    bur, bui = bu[:, :SSM_HALF_ST], bu[:, SSM_HALF_ST:]
    sr, si = s
    return lr * sr - li * si + bur, lr * si + li * sr + bui


def _ssm_state_kernel(u_ref, bd4_ref, lr_ref, li_ref, vr_ref, vi_ref):
    for h in range(2):
        st = slice(h * SSM_HALF_ST, (h + 1) * SSM_HALF_ST)
        lr, li = lr_ref[:, st], li_ref[:, st]
        s = None
        for t in range(0, SSM_CHUNK, SSM_BLOCK):
            u4 = jnp.concatenate([_load_lane_blocks(u_ref, _chunk_rows(u_ref, t + i), 2 * h, 2 * h + 2)
                                  for i in range(SSM_BLOCK)], axis=1)
            bu = jnp.dot(u4.astype(BF16), bd4_ref[h], preferred_element_type=F32)
            bur, bui = bu[:, :SSM_HALF_ST], bu[:, SSM_HALF_ST:]
            s = (bur, bui) if s is None else (lr * s[0] - li * s[1] + bur, lr * s[1] + li * s[0] + bui)
        vr_ref[:, st] = s[0]
        vi_ref[:, st] = s[1]


def _ssm_state(u, sp, tc):
    r = u.shape[1] // SSM_CHUNK
    row = lambda w: pl.BlockSpec((tc, w), lambda i: (i, 0))
    const = lambda shape: pl.BlockSpec(shape, lambda i: (0,) * len(shape))
    out = jax.ShapeDtypeStruct((r, SSM_ALL_ST), F32)
    return pl.pallas_call(
        _ssm_state_kernel,
        grid=(r // tc,),
        in_specs=[pl.BlockSpec((LANE_BLOCKS, tc * SSM_CHUNK, 128), lambda i: (0, i, 0)),
                  const((2, SSM_BLOCK * SSM_HALF_CH, 2 * SSM_HALF_ST)),
                  const((1, SSM_ALL_ST)), const((1, SSM_ALL_ST))],
        out_specs=[row(SSM_ALL_ST), row(SSM_ALL_ST)],
        out_shape=[out, out],
        compiler_params=_params("parallel"),
        name="ssm_state",
    )(u, sp["bd4"], sp["a4r"], sp["a4i"])


def _ssm_carry_kernel(vr_ref, vi_ref, s0r_ref, s0i_ref, ar_ref, ai_ref,
                      sr_ref, si_ref, lr_ref, li_ref, *, nc):
    ar, ai = ar_ref[...], ai_ref[...]

    def body(c, s):
        sr, si = s
        sr_ref[0, pl.ds(c, 1), :] = sr
        si_ref[0, pl.ds(c, 1), :] = si
        return (ar * sr - ai * si + vr_ref[0, pl.ds(c, 1), :],
                ar * si + ai * sr + vi_ref[0, pl.ds(c, 1), :])

    sr, si = lax.fori_loop(0, nc, body, (s0r_ref[0], s0i_ref[0]))
    lr_ref[0] = sr
    li_ref[0] = si


def _ssm_carry(vr, vi, s0r, s0i, sp):
    nb, nc, _ = vr.shape
    seq = pl.BlockSpec((1, nc, SSM_ALL_ST), lambda b: (b, 0, 0))
    one = pl.BlockSpec((1, 1, SSM_ALL_ST), lambda b: (b, 0, 0))
    vec = pl.BlockSpec((1, SSM_ALL_ST), lambda b: (0, 0))
    return pl.pallas_call(
        functools.partial(_ssm_carry_kernel, nc=nc),
        grid=(nb,),
        in_specs=[seq, seq, one, one, vec, vec],
        out_specs=[seq, seq, one, one],
        out_shape=[jax.ShapeDtypeStruct((nb, nc, SSM_ALL_ST), F32)] * 2
                  + [jax.ShapeDtypeStruct((nb, 1, SSM_ALL_ST), F32)] * 2,
        compiler_params=_params("parallel"),
        name="ssm_carry",
    )(vr, vi, s0r, s0i, sp["a16r"], sp["a16i"])


def _ssm_out_kernel(u_ref, sr_ref, si_ref, bdb_ref, bdc_ref, lr_ref, li_ref, d_ref, w_ref, b_ref,
                    o_ref):
    lam, s = [], []
    for h in range(2):
        st = slice(h * SSM_HALF_ST, (h + 1) * SSM_HALF_ST)
        lam.append((lr_ref[:, st], li_ref[:, st]))
        s.append((sr_ref[:, st], si_ref[:, st]))
    for t in range(SSM_CHUNK):
        ys = []
        for h in range(2):
            s[h] = _ssm_step(u_ref, bdb_ref, lam[h][0], lam[h][1], t, h, s[h])
            ys.append(jnp.dot(s[h][0].astype(BF16), bdc_ref[h, :SSM_HALF_ST, :],
                              preferred_element_type=F32)
                      + jnp.dot(s[h][1].astype(BF16), bdc_ref[h, SSM_HALF_ST:, :],
                                preferred_element_type=F32))
        y = jnp.concatenate(ys, axis=1) + d_ref[...] * _load_lane_blocks(u_ref, _chunk_rows(u_ref, t))
        y = 0.5 * y * (1.0 + jnp.tanh(math.sqrt(2.0 / math.pi) * (y + 0.044715 * (y * y * y))))
        z = jnp.dot(y.astype(BF16), w_ref[...], preferred_element_type=F32) + b_ref[...]
        _store_lane_blocks(o_ref, _chunk_rows(o_ref, t), y * _sigmoid(z))


def _ssm_out(u, sr, si, sp, d_skip, w_glu, b_glu, tc):
    r = u.shape[1] // SSM_CHUNK
    row = lambda w: pl.BlockSpec((tc, w), lambda i: (i, 0))
    tok = pl.BlockSpec((LANE_BLOCKS, tc * SSM_CHUNK, 128), lambda i: (0, i, 0))
    const = lambda shape: pl.BlockSpec(shape, lambda i: (0,) * len(shape))
    return pl.pallas_call(
        _ssm_out_kernel,
        grid=(r // tc,),
        in_specs=[tok, row(SSM_ALL_ST), row(SSM_ALL_ST),
                  const((2, SSM_HALF_CH, 2 * SSM_HALF_ST)), const((2, 2 * SSM_HALF_ST, SSM_HALF_CH)),
                  const((1, SSM_ALL_ST)), const((1, SSM_ALL_ST)),
                  const((1, W_MIX)), const((W_MIX, W_MIX)), const((1, W_MIX))],
        out_specs=tok,
        out_shape=jax.ShapeDtypeStruct(u.shape, F32),
        compiler_params=_params("parallel"),
        name="ssm_out",
    )(u, sr, si, sp["bdb"], sp["bdc"], sp["lam_r"], sp["lam_i"], d_skip, w_glu, b_glu)


def _cumsum_kernel(lf_ref, c_ref, carry, *, tc):
    @pl.when(pl.program_id(1) == 0)
    def _():
        carry[...] = jnp.zeros_like(carry)

    r = lax.broadcasted_iota(jnp.int32, (tc, tc), 0)
    c = lax.broadcasted_iota(jnp.int32, (tc, tc), 1)
    tri = jnp.where(r >= c, 1.0, 0.0).astype(BF16)
    lf = lf_ref[0]
    hi = lf.astype(BF16)
    rest = lf - hi.astype(F32)
    mid = rest.astype(BF16)
    lo = (rest - mid.astype(F32)).astype(BF16)
    cs = sum(jnp.dot(tri, part, preferred_element_type=F32) for part in (lo, mid, hi)) + carry[...]
    c_ref[0] = cs
    carry[...] = cs[tc - 1:tc, :]


def _cumsum(lf, tc):
    nb, lk, _ = lf.shape
    blk = pl.BlockSpec((1, tc, FOX_HEADS), lambda b, t: (b, t, 0))
    return pl.pallas_call(
        functools.partial(_cumsum_kernel, tc=tc),
        grid=(nb, lk // tc),
        in_specs=[blk],
        out_specs=blk,
        out_shape=jax.ShapeDtypeStruct((nb, lk, FOX_HEADS), F32),
        scratch_shapes=[pltpu.VMEM((1, FOX_HEADS), F32)],
        compiler_params=_params("parallel", "arbitrary"),
        name="logf_cumsum",
    )(lf)


ATTN_ROWS = 16
DEAD_LOGIT = 120.0
NORM_MARGIN = 1.02


def _attn_kernel(b_tab, hp_tab, qi_tab, kj_tab, first_tab, q_ref, k_ref, v_ref, cq_ref, ck_ref,
                 o_ref, s_scr, p_scr, m_scr, a_scr, cq_scr, acc_scr, *, tq, tk, q_off):
    step_id = pl.program_id(0)
    qi = qi_tab[step_id]
    kj = kj_tab[step_id]
    q_lo = q_off + qi * tq
    k_lo = kj * tk
    lane = lax.broadcasted_iota(jnp.int32, (1, 2 * FOX_HEAD_DIM), 1)
    in_heads = [(lane >= hh * FOX_HEAD_DIM) & (lane < (hh + 1) * FOX_HEAD_DIM) for hh in range(2)]

    @pl.when(first_tab[step_id] != 0)
    def _():
        m_scr[...] = jnp.full_like(m_scr, NEG_INF)
        acc_scr[...] = jnp.zeros_like(acc_scr)
        cq = cq_ref[0, 0] * LOG2E
        for hh in range(2):
            cq_scr[hh] = jnp.broadcast_to(cq[:, hh:hh + 1], (tq, 128))

    q = q_ref[0]

    def softmax_rows(hh, masked, ck, cols):
        ckb = [jnp.broadcast_to(ck[hh:hh + 1, lo:hi], (ATTN_ROWS, hi - lo)) for lo, hi in cols]
        for r in range(0, tq, ATTN_ROWS):
            t = [s_scr[hh, r:r + ATTN_ROWS, lo:hi] - ckb[c] for c, (lo, hi) in enumerate(cols)]
            if masked:
                qpos = q_lo + r + lax.broadcasted_iota(jnp.int32, (ATTN_ROWS, 128), 0)
                kpos = k_lo + lax.broadcasted_iota(jnp.int32, (ATTN_ROWS, 128), 1)
                t = [jnp.where((kpos + lo <= qpos)[:, :hi - lo], t[c], NEG_INF)
                     for c, (lo, hi) in enumerate(cols)]
            tmax = None
            for c, (lo, hi) in enumerate(cols):
                if hi - lo == 128:
                    tmax = t[c] if tmax is None else jnp.maximum(tmax, t[c])
            if tmax is not None:
                tmax = jnp.max(tmax, axis=1, keepdims=True)
            if cols[-1][1] - cols[-1][0] < 128:
                ragged = jnp.max(t[-1], axis=1, keepdims=True)
                tmax = ragged if tmax is None else jnp.maximum(tmax, ragged)
            tmax = jnp.broadcast_to(tmax, (ATTN_ROWS, 128))
            cq2 = cq_scr[hh, r:r + ATTN_ROWS, :]
            m_old = m_scr[hh, r:r + ATTN_ROWS, :]
            m_new = jnp.maximum(m_old, tmax + cq2)
            m_scr[hh, r:r + ATTN_ROWS, :] = m_new
            a_scr[hh, r:r + ATTN_ROWS, :] = jnp.exp2(m_old - m_new)
            mt = m_new - cq2
            for c, (lo, hi) in enumerate(cols):
                p_scr[hh, r:r + ATTN_ROWS, lo:hi] = jnp.exp2(t[c] - mt[:, :hi - lo]).astype(BF16)

    def step(masked):
        cols = [(c, min(c + 128, tk)) for c in range(0, tk, 128)]
        kb = k_ref[0].astype(BF16)
        v = v_ref[0]
        ck = ck_ref[0, 0] * LOG2E
        for hh in range(2):
            qm = jnp.where(in_heads[hh], q, jnp.zeros_like(q))
            s_scr[hh] = lax.dot_general(qm, kb, (((1,), (1,)), ((), ())),
                                        preferred_element_type=F32)
        for hh in range(2):
            softmax_rows(hh, masked, ck, cols)
            vh = jnp.where(in_heads[hh], v, 1.0).astype(BF16)
            pv = jnp.dot(p_scr[hh], vh, preferred_element_type=F32)
            acc_scr[hh] = a_scr[hh] * acc_scr[hh] + pv

    needs_mask = k_lo + tk - 1 > q_lo

    @pl.when(needs_mask)
    def _():
        step(True)

    @pl.when(jnp.logical_not(needs_mask))
    def _():
        step(False)

    @pl.when(kj == (q_lo + tq - 1) // tk)
    def _():
        o = []
        for hh in range(2):
            acc = acc_scr[hh]
            o.append(acc * (1.0 / pltpu.roll(acc, FOX_HEAD_DIM, axis=1)))
        o_ref[0] = jnp.where(lane < FOX_HEAD_DIM, o[0], o[1]).astype(BF16)


def _live_tiles(qk_stats, c_all, nb, nq, t):
    qmax = jnp.sqrt(qk_stats[:, 0, :FOX_HEADS]).reshape(nb, nq, FOX_HEADS) * NORM_MARGIN
    kmax = jnp.sqrt(qk_stats[:, 1, :FOX_HEADS]).reshape(nb, nq, FOX_HEADS) * NORM_MARGIN
    c_first = c_all[:, 0::t, :]
    c_last = c_all[:, t - 1::t, :]
    bound = (qmax[:, :, None, :] * (kmax[:, None, :, :] + kmax[:, :, None, :])
             + c_first[:, :, None, :] - c_last[:, None, :, :])
    dead = jnp.all((bound < -DEAD_LOGIT).reshape(nb, nq, nq, HEAD_PAIRS, 2), axis=-1)
    below = (jnp.arange(nq)[None, :] < jnp.arange(nq)[:, None])[None, :, :, None]
    return jnp.transpose(jnp.logical_not(dead & below), (0, 3, 1, 2))


def _attention(q, k_all, v_all, cq, ck, q_off, tq, tk, live=None):
    nb, lq, _ = q.shape
    nq = lq // tq
    nk = k_all.shape[1] // tk
    last = (q_off + np.arange(nq) * tq + tq - 1) // tk
    causal = np.arange(nk)[None, :] <= last[:, None]
    valid = jnp.broadcast_to(causal if live is None else live & causal, (nb, HEAD_PAIRS, nq, nk))
    first = valid & (jnp.cumsum(valid.astype(jnp.int32), axis=3) == 1)
    max_steps = nb * HEAD_PAIRS * int(causal.sum())
    (idx,) = jnp.nonzero(valid.reshape(-1), size=max_steps, fill_value=0)
    idx = idx.astype(jnp.int32)
    n_steps = jnp.sum(valid, dtype=jnp.int32)
    kj_tab = idx % nk
    qi_tab = (idx // nk) % nq
    hp_tab = (idx // (nk * nq)) % HEAD_PAIRS
    b_tab = idx // (nk * nq * HEAD_PAIRS)
    first_tab = first.reshape(-1)[idx].astype(jnp.int32)
    qmap = lambda i, bt, ht, qt, kt, ft: (bt[i], qt[i], ht[i])
    kmap = lambda i, bt, ht, qt, kt, ft: (bt[i], kt[i], ht[i])
    in_specs = [
        pl.BlockSpec((1, tq, 128), qmap),
        pl.BlockSpec((1, tk, 128), kmap),
        pl.BlockSpec((1, tk, 128), kmap),
        pl.BlockSpec((1, 1, tq, 2), lambda i, bt, ht, qt, kt, ft: (bt[i], ht[i], qt[i], 0)),
        pl.BlockSpec((1, 1, 2, tk), lambda i, bt, ht, qt, kt, ft: (bt[i], ht[i], 0, kt[i])),
    ]
    return pl.pallas_call(
        functools.partial(_attn_kernel, tq=tq, tk=tk, q_off=q_off),
        grid_spec=pltpu.PrefetchScalarGridSpec(
            num_scalar_prefetch=5,
            grid=(n_steps,),
            in_specs=in_specs,
            out_specs=pl.BlockSpec((1, tq, 128), qmap),
            scratch_shapes=[pltpu.VMEM((2, tq, tk), F32), pltpu.VMEM((2, tq, tk), BF16),
                            pltpu.VMEM((2, tq, 128), F32), pltpu.VMEM((2, tq, 128), F32),
                            pltpu.VMEM((2, tq, 128), F32), pltpu.VMEM((2, tq, 128), F32)],
        ),
        out_shape=jax.ShapeDtypeStruct((nb, lq, W_MIX), BF16),
        compiler_params=_params("arbitrary"),
        name="fox_attention",
    )(b_tab, hp_tab, qi_tab, kj_tab, first_tab, q, k_all, v_all, cq, ck)


def _merge_kernel(x_ref, g_ref, ya_ref, yb_ref, yc_ref, wg_ref, wb_ref, wo_ref, o_ref):
    x = x_ref[...]
    hb = _rms_bf16(x, g_ref[...])
    merged = None
    for b, y_ref in enumerate((ya_ref, yb_ref, yc_ref)):
        gate = _sigmoid(jnp.dot(hb, wg_ref[:, b * D_MODEL:(b + 1) * D_MODEL],
                                preferred_element_type=F32))
        y = _load_lane_blocks(y_ref, slice(None)) if b == 1 else y_ref[...]
        term = gate * jnp.dot(y.astype(BF16), wb_ref[b], preferred_element_type=F32)
        merged = term if merged is None else merged + term
    o_ref[...] = x + jnp.dot(merged.astype(BF16), wo_ref[...], preferred_element_type=F32)


def _merge(x2, g, ya, yb, yc, w_gate, w_branch, w_out, tm):
    n = x2.shape[0]
    row = lambda w: pl.BlockSpec((tm, w), lambda i: (i, 0))
    const = lambda shape: pl.BlockSpec(shape, lambda i: (0,) * len(shape))
    return pl.pallas_call(
        _merge_kernel,
        grid=(n // tm,),
        in_specs=[row(D_MODEL), const((1, D_MODEL)), row(W_MIX),
                  pl.BlockSpec((LANE_BLOCKS, tm, 128), lambda i: (0, i, 0)), row(W_MIX),
                  const((D_MODEL, 3 * D_MODEL)), const((3, W_MIX, D_MODEL)),
                  const((D_MODEL, D_MODEL))],
        out_specs=row(D_MODEL),
        out_shape=jax.ShapeDtypeStruct((n, D_MODEL), F32),
        compiler_params=_params("parallel"),
        name="branch_merge",
    )(x2, g, ya, yb, yc, w_gate, w_branch, w_out)


FFN_HALO = 16
FFN_SUB = 256


def _ffn_kernel(x_ref, xh_ref, g_ref, hist_ref, wug_ref, wuv_ref, cwg_ref, cwv_ref,
                cbg_ref, cbv_ref, wd_ref, o_ref, cn_ref, h_scr, hh_scr, acc_scr, act_scr,
                *, tm, tn, nj):
    t = pl.program_id(1)
    j = pl.program_id(2)

    @pl.when(j == 0)
    def _():
        h_scr[...] = _rms_bf16(x_ref[...], g_ref[...])
        hh_scr[...] = _rms_bf16(xh_ref[...], g_ref[...])

    first = t == 0
    for lo in range(0, tn, FFN_SUB):
        hi = min(lo + FFN_SUB, tn)
        conv = []
        for hf, (wu_ref, cw_ref, cb_ref) in enumerate(((wug_ref, cwg_ref, cbg_ref),
                                                       (wuv_ref, cwv_ref, cbv_ref))):
            w = wu_ref[:, lo:hi]
            up = jnp.dot(h_scr[...], w, preferred_element_type=F32)
            uph = jnp.dot(hh_scr[...], w, preferred_element_type=F32)
            prev = jnp.where(first, hist_ref[0, hf, :, lo:hi], uph[FFN_HALO - 8:, :])
            full = jnp.concatenate([prev, up], axis=0)
            cn_ref[0, hf, j, :, lo:hi] = up[tm - 2:, :]
            cw = cw_ref[:, lo:hi]
            conv.append(cb_ref[:, lo:hi] + full[6:6 + tm, :] * cw[0:1, :]
                        + full[7:7 + tm, :] * cw[1:2, :] + up * cw[2:3, :])
        gate, val = conv
        act_scr[:, lo:hi] = (gate * _sigmoid(gate) * val).astype(BF16)
    down = jnp.dot(act_scr[...], wd_ref[...], preferred_element_type=F32)

    if nj == 1:
        o_ref[...] = x_ref[...] + down
        return

    @pl.when(j == 0)
    def _():
        acc_scr[...] = down

    @pl.when((j > 0) & (j < nj - 1))
    def _():
        acc_scr[...] += down

    @pl.when(j == nj - 1)
    def _():
        o_ref[...] = x_ref[...] + acc_scr[...] + down


def _ffn(x2, g, hist, wug, wuv, cwg, cwv, cbg, cbv, wd, nb, L, tm, tn):
    n = x2.shape[0]
    nt, nj = L // tm, D_FF // tn
    hpt = tm // FFN_HALO
    row = lambda b, t, j: (b * nt + t, 0)
    col = lambda b, t, j: (0, j)
    resident = dict(pipeline_mode=pl.Buffered(1)) if nj == 1 else {}
    return pl.pallas_call(
        functools.partial(_ffn_kernel, tm=tm, tn=tn, nj=nj),
        grid=(nb, nt, nj),
        in_specs=[
            pl.BlockSpec((tm, D_MODEL), row),
            pl.BlockSpec((FFN_HALO, D_MODEL), lambda b, t, j: (jnp.maximum((b * nt + t) * hpt - 1, 0), 0)),
            pl.BlockSpec((1, D_MODEL), lambda b, t, j: (0, 0)),
            pl.BlockSpec((1, 2, 8, tn), lambda b, t, j: (b, 0, 0, j)),
            pl.BlockSpec((D_MODEL, tn), col, **resident),
            pl.BlockSpec((D_MODEL, tn), col, **resident),
            pl.BlockSpec((3, tn), col),
            pl.BlockSpec((3, tn), col),
            pl.BlockSpec((1, tn), col),
            pl.BlockSpec((1, tn), col),
            pl.BlockSpec((tn, D_MODEL), lambda b, t, j: (j, 0), **resident),
        ],
        out_specs=[pl.BlockSpec((tm, D_MODEL), row),
                   pl.BlockSpec((1, 2, nj, 2, tn), lambda b, t, j: (b, 0, 0, 0, 0))],
        out_shape=[jax.ShapeDtypeStruct((n, D_MODEL), F32),
                   jax.ShapeDtypeStruct((nb, 2, nj, 2, tn), F32)],
        scratch_shapes=[pltpu.VMEM((tm, D_MODEL), BF16), pltpu.VMEM((FFN_HALO, D_MODEL), BF16),
                        pltpu.VMEM((tm, D_MODEL) if nj > 1 else (8, 128), F32),
                        pltpu.VMEM((tm, tn), BF16)],
        compiler_params=_params("arbitrary", "arbitrary", "arbitrary"),
        name="conv_ffn",
    )(x2, x2, g, hist, wug, wuv, cwg, cwv, cbg, cbv, wd)


def _final_norm_kernel(x_ref, g_ref, o_ref):
    x = x_ref[...]
    ms = jnp.mean(x * x, axis=-1, keepdims=True)
    o_ref[...] = x * lax.rsqrt(ms + EPS) * g_ref[...]


def _final_norm(x2, g, tm):
    n = x2.shape[0]
    row = pl.BlockSpec((tm, D_MODEL), lambda i: (i, 0))
    return pl.pallas_call(
        _final_norm_kernel,
        grid=(n // tm,),
        in_specs=[row, pl.BlockSpec((1, D_MODEL), lambda i: (0, 0))],
        out_specs=row,
        out_shape=jax.ShapeDtypeStruct((n, D_MODEL), F32),
        compiler_params=_params("parallel"),
        name="final_norm",
    )(x2, g)


def _layer(x2, nb, L, pos0, pool_hist, s0, k_past, v_past, logf_past, conv_hist, p,
           layer, depth, kbuf, vbuf):
    n = nb * L
    tm = _tile(L, 512)
    u_pool, u_ssm, q, kbuf, vbuf, kb, vb, logf, qk_stats = _inproj(
        x2, p["g_mix"], p["w_main"], p["w_f"], p["b_f"], tm, layer, depth, kbuf, vbuf)

    y_a, pool_new = _pool(u_pool.reshape(nb, L, W_MIX), pool_hist, p["w_pool"], p["pool_scale"],
                          pos0, tm)

    nc = L // SSM_CHUNK
    sp = p["ssm"]
    tc = _tile(nb * nc, 256)
    vr, vi = _ssm_state(u_ssm, sp, tc)
    sr, si, sl_re, sl_im = _ssm_carry(vr.reshape(nb, nc, SSM_ALL_ST), vi.reshape(nb, nc, SSM_ALL_ST),
                                      s0[0], s0[1], sp)
    y_b = _ssm_out(u_ssm, sr.reshape(nb * nc, SSM_ALL_ST), si.reshape(nb * nc, SSM_ALL_ST), sp,
                   p["d_skip"], p["w_glu"], p["b_glu"], tc)

    q3, kb3, vb3 = (a.reshape(nb, L, W_MIX) for a in (q, kb, vb))
    logf3 = logf.reshape(nb, L, FOX_HEADS)
    logf_all = logf3 if k_past is None else jnp.concatenate([logf_past, logf3], axis=1)
    lk = logf_all.shape[1]
    c_all = _cumsum(logf_all, _tile(lk, 512))
    cq = jnp.transpose(c_all[:, lk - L:].reshape(nb, L, HEAD_PAIRS, 2), (0, 2, 1, 3))
    ck = jnp.transpose(c_all.reshape(nb, lk, HEAD_PAIRS, 2), (0, 2, 3, 1))
    if k_past is None:
        tk = _tile(L, 512)
        live = _live_tiles(qk_stats, c_all, nb, L // tm, tm) if tk == tm else None
        y_c = _attention(q3, kb3, vb3, cq, ck, pos0, tm, tk, live)
    else:
        k_all = jnp.concatenate([k_past, kbuf[layer].reshape(nb, L, W_MIX)], axis=1)
        v_all = jnp.concatenate([v_past, vbuf[layer].reshape(nb, L, W_MIX)], axis=1)
        y_c = _attention(q3, k_all, v_all, cq, ck, pos0, tm, lk)

    x2 = _merge(x2, p["g_mix"], y_a.reshape(n, W_MIX), y_b, y_c.reshape(n, W_MIX),
                p["w_gate"], p["w_branch"], p["w_out"], tm)
    x2, conv_new = _ffn(x2, p["g_ffn"], conv_hist, p["wug"], p["wuv"], p["cwg"], p["cwv"],
                        p["cbg"], p["cbv"], p["w_down"], nb, L, tm, D_FF)
    states = (pool_new,
              sl_re.reshape(nb, SSM_GROUPS, SSM_STATE), sl_im.reshape(nb, SSM_GROUPS, SSM_STATE),
              logf3,
              jnp.transpose(conv_new, (0, 3, 1, 2, 4)).reshape(nb, 2, 2 * D_FF))
    return x2, states, kbuf, vbuf


def kernel(x_prompt, x_sample, cache_pool, state_ssm_re, state_ssm_im, cache_k, cache_v, cache_logf, cache_ffn_conv, norm_mix_g, w_in, b_f, w_pool, pool_scale, lam_re, lam_im, log_dt, b_re, b_im, c_re, c_im, d_skip, w_glu, b_glu, w_branch, w_out, norm_ffn_g, w_up, conv_w, conv_b, w_down, norm_final_g):
    bp, lp, _ = x_prompt.shape
    bs, ls, _ = x_sample.shape
    depth = w_in.shape[0]
    past = cache_k.shape[2]
    col_f = 5 * W_MIX
    col_gate = col_f + FOX_HEADS

    hp = x_prompt.reshape(bp * lp, D_MODEL)
    hs = x_sample.reshape(bs * ls, D_MODEL)
    zero_pool = jnp.zeros((bp, POOL_HIST, W_MIX), F32)
    zero_s = (jnp.zeros((bp, 1, SSM_ALL_ST), F32),) * 2
    zero_conv = jnp.zeros((bp, 2, 8, D_FF), F32)

    st_p, st_s = [], []
    kp = vp = ks = vs = None
    for l in range(depth):
        p = dict(
            ssm=_ssm_blockdiag(lam_re[l], lam_im[l], log_dt[l], b_re[l], b_im[l], c_re[l], c_im[l]),
            g_mix=norm_mix_g[l][None], g_ffn=norm_ffn_g[l][None],
            w_main=w_in[l][:, :col_f].astype(BF16),
            w_f=jnp.pad(w_in[l][:, col_f:col_gate], ((0, 0), (0, 128 - FOX_HEADS))).astype(BF16),
            b_f=b_f[l][None],
            w_gate=w_in[l][:, col_gate:].astype(BF16),
            w_pool=w_pool[l].astype(BF16), pool_scale=pool_scale[l][None],
            d_skip=d_skip[l][None], w_glu=w_glu[l].astype(BF16), b_glu=b_glu[l][None],
            w_branch=w_branch[l].astype(BF16), w_out=w_out[l].astype(BF16),
            wug=w_up[l][:, :D_FF].astype(BF16), wuv=w_up[l][:, D_FF:].astype(BF16),
            cwg=conv_w[l][:, :D_FF], cwv=conv_w[l][:, D_FF:],
            cbg=conv_b[l][None, :D_FF], cbv=conv_b[l][None, D_FF:],
            w_down=w_down[l].astype(BF16),
        )
        hp, sp, kp, vp = _layer(hp, bp, lp, 0, zero_pool, zero_s, None, None, None, zero_conv, p,
                                l, depth, kp, vp)
        s0 = (state_ssm_re[l].reshape(bs, 1, SSM_ALL_ST), state_ssm_im[l].reshape(bs, 1, SSM_ALL_ST))
        conv_hist = jnp.pad(jnp.transpose(cache_ffn_conv[l].reshape(bs, 2, 2, D_FF), (0, 2, 1, 3)),
                            ((0, 0), (0, 0), (6, 0), (0, 0)))
        hs, ss, ks, vs = _layer(hs, bs, ls, past, cache_pool[l], s0,
                                cache_k[l].reshape(bs, past, W_MIX), cache_v[l].reshape(bs, past, W_MIX),
                                cache_logf[l], conv_hist, p, l, depth, ks, vs)
        st_p.append(sp)
        st_s.append(ss)

    y_prompt = _final_norm(hp, norm_final_g[None], _tile(bp * lp, 512)).reshape(bp, lp, D_MODEL)
    y_sample = _final_norm(hs, norm_final_g[None], _tile(bs * ls, 512)).reshape(bs, ls, D_MODEL)
    stack = lambda st, i: jnp.stack([s[i] for s in st], axis=0)
    heads_p = (depth, bp, lp, FOX_HEADS, FOX_HEAD_DIM)
    heads_s = (depth, bs, ls, FOX_HEADS, FOX_HEAD_DIM)
    return (y_prompt, y_sample,
            stack(st_p, 0), stack(st_s, 0), stack(st_p, 1), stack(st_s, 1),
            stack(st_p, 2), stack(st_s, 2),
            kp.reshape(heads_p), ks.reshape(heads_s), vp.reshape(heads_p), vs.reshape(heads_s),
            stack(st_p, 3), stack(st_s, 3), stack(st_p, 4), stack(st_s, 4))
```
